```python
import math
import jax, jax.numpy as jnp
from jax import lax
import numpy as np

D_MODEL = 1024
BATCH = 8
SEQ = 2048
DEPTH = 2
DEC_BATCH = 32
DEC_SEQ = 8
PAST_LEN = 8192
PAGE_SIZE = 128

HEAD_DIM = 64
N_MIX_HEADS = D_MODEL // HEAD_DIM
POOL_GROUPS = 4
POOL_WINDOWS = (2, 4, 8, 16)
POOL_GRP = HEAD_DIM
POOL_DIM = POOL_GROUPS * POOL_GRP
POOL_PAST = max(POOL_WINDOWS) - 1
ATT_HEADS = (N_MIX_HEADS - POOL_GROUPS) // 2
ATT_DIM = ATT_HEADS * HEAD_DIM
RET_HEADS = N_MIX_HEADS - POOL_GROUPS - ATT_HEADS
RET_DIM = RET_HEADS * HEAD_DIM
MIX_DIM = POOL_DIM + ATT_DIM + RET_DIM
IN_COLS = POOL_DIM + 3 * ATT_DIM + 4 * RET_DIM
MOBA_BLOCK = 256
MOBA_TOPK = 3
Q_BLOCK = 128
RET_CHUNK = 128
D_FF = ((8 * D_MODEL // 3 + 127) // 128) * 128
EPS = 1e-6

kernel_name = "hybrid_pool_moba_retention_step"


def rmsnorm(x, g):
    xf = x.astype(jnp.float32)
    y = xf * lax.rsqrt(jnp.mean(xf * xf, axis=-1, keepdims=True) + EPS)
    return (y * g.astype(jnp.float32)).astype(x.dtype)


def swiglu(h, w_in, w_out):
    gate, up = jnp.split(h @ w_in, 2, axis=-1)
    return (jax.nn.silu(gate) * up) @ w_out


def alibi_slopes():
    return jnp.exp2(-8.0 * jnp.arange(1, ATT_HEADS + 1, dtype=jnp.float32) / ATT_HEADS)


def retention_log_decay():
    return jnp.log1p(-jnp.exp2(-5.0 - jnp.arange(RET_HEADS, dtype=jnp.float32)))


def pool_mixer(u, prefix, pos, w_grp, scale):
    n, t, _ = u.shape
    ext = jnp.concatenate([prefix.astype(u.dtype), u], axis=1)
    cs = jnp.pad(jnp.cumsum(ext.astype(jnp.float32), axis=1), ((0, 0), (1, 0), (0, 0)))
    p = POOL_PAST
    end = cs[:, p + 1:p + 1 + t]
    pooled = []
    for g, w in enumerate(POOL_WINDOWS):
        sl = slice(g * POOL_GRP, (g + 1) * POOL_GRP)
        start = cs[:, p + 1 - w:p + 1 - w + t, sl]
        cnt = jnp.minimum(w, pos + 1).astype(jnp.float32)[None, :, None]
        pooled.append((end[..., sl] - start) / cnt)
    d = (jnp.concatenate(pooled, axis=-1) - u.astype(jnp.float32)).astype(u.dtype)
    d = d.reshape(n, t, POOL_GROUPS, POOL_GRP)
    y = jnp.einsum('ntgc,gcd->ntgd', d, w_grp).reshape(n, t, POOL_DIM) * scale
    return y, ext[:, -POOL_PAST:]


def moba_seq(q, k, v, q_pos):
    t = q.shape[0]
    l_keys = k.shape[0]
    nb = -(-l_keys // MOBA_BLOCK)
    pad = nb * MOBA_BLOCK - l_keys
    kb = jnp.pad(k, ((0, pad), (0, 0), (0, 0))).reshape(nb, MOBA_BLOCK, ATT_HEADS, HEAD_DIM).transpose(2, 0, 1, 3)
    vb = jnp.pad(v, ((0, pad), (0, 0), (0, 0))).reshape(nb, MOBA_BLOCK, ATT_HEADS, HEAD_DIM).transpose(2, 0, 1, 3)
    k_mean = jnp.mean(kb.astype(jnp.float32), axis=2)
    n_top = min(MOBA_TOPK, nb)
    qc = math.gcd(t, Q_BLOCK)
    qs = q.reshape(t // qc, qc, ATT_HEADS, HEAD_DIM).transpose(0, 2, 1, 3)
    ps = q_pos.reshape(t // qc, qc)
    blk_ids = jnp.arange(nb, dtype=jnp.int32)
    h_idx = jnp.arange(ATT_HEADS)[:, None, None]
    slopes = alibi_slopes()
    scale = HEAD_DIM ** -0.5

    def one_block(args):
        qh, pos = args
        own = pos // MOBA_BLOCK
        gate = jnp.einsum('hqd,hnd->hqn', qh.astype(jnp.float32), k_mean)
        fully_past = blk_ids[None, :] < own[:, None]
        gate = jnp.where(fully_past[None], gate, -jnp.inf)
        _, top = lax.top_k(gate, n_top)
        slot_ok = jnp.arange(n_top)[None, :] < jnp.minimum(own, MOBA_TOPK)[:, None]
        sel = jnp.concatenate([top.astype(jnp.int32), jnp.broadcast_to(own[None, :, None], (ATT_HEADS, qc, 1))], axis=-1)
        ok = jnp.concatenate([jnp.broadcast_to(slot_ok[None], (ATT_HEADS, qc, n_top)),
                              jnp.ones((ATT_HEADS, qc, 1), dtype=bool)], axis=-1)
        kg = kb[h_idx, sel]
        vg = vb[h_idx, sel]
        key_pos = sel[..., None] * MOBA_BLOCK + jnp.arange(MOBA_BLOCK, dtype=jnp.int32)
        qpos = pos[None, :, None, None]
        valid = ok[..., None] & (key_pos <= qpos)
        s = jnp.einsum('hqd,hqskd->hqsk', qh, kg, preferred_element_type=jnp.float32) * scale
        s = s - slopes[:, None, None, None] * (qpos - key_pos).astype(jnp.float32)
        s = jnp.where(valid, s, -jnp.inf)
        pr = jax.nn.softmax(s.reshape(ATT_HEADS, qc, -1), axis=-1).reshape(s.shape)
        return jnp.einsum('hqsk,hqskd->hqd', pr.astype(vg.dtype), vg)

    out = lax.map(one_block, (qs, ps))
    return out.transpose(0, 2, 1, 3).reshape(t, ATT_DIM)


def retention(q, k, v, s0):
    n, t, h, _ = q.shape
    dv = v.shape[-1]
    c = math.gcd(t, RET_CHUNK)
    nc = t // c
    lg = retention_log_decay()
    idx = jnp.arange(c, dtype=jnp.float32)
    diff = idx[:, None] - idx[None, :]
    d_in = jnp.exp(jnp.where(diff >= 0, lg[:, None, None] * diff, -jnp.inf))
    q_dec = jnp.exp(lg[:, None] * (idx + 1.0))
    k_dec = jnp.exp(lg[:, None] * (c - 1.0 - idx))
    c_dec = jnp.exp(lg * c)

    def chunks(a):
        return a.astype(jnp.float32).reshape(n, nc, c, h, a.shape[-1]).transpose(1, 0, 3, 2, 4)

    def step(s, xs):
        qc_, kc_, vc_ = xs
        att = jnp.einsum('nhid,nhjd->nhij', qc_, kc_) * d_in
        o = jnp.einsum('nhij,nhjv->nhiv', att, vc_) + jnp.einsum('nhid,nhdv->nhiv', qc_ * q_dec[..., None], s)
        s = s * c_dec[:, None, None] + jnp.einsum('nhjd,nhjv->nhdv', kc_ * k_dec[..., None], vc_)
        return s, o

    s, o = lax.scan(step, s0.astype(jnp.float32), (chunks(q), chunks(k), chunks(v)))
    return o.transpose(1, 0, 3, 2, 4).reshape(n, t, h, dv), s


def layer_fwd(x, pos0, pool_prefix, ret_s0, past_k, past_v,
              g_ffn1, f1_in, f1_out, g_mix, w_in, pool_w, pool_scale, w_out, g_ffn2, f2_in, f2_out):
    n, t, _ = x.shape
    pos = pos0 + jnp.arange(t, dtype=jnp.int32)
    x = x + 0.5 * swiglu(rmsnorm(x, g_ffn1), f1_in, f1_out)
    z = rmsnorm(x, g_mix) @ w_in
    sizes = (POOL_DIM, ATT_DIM, ATT_DIM, ATT_DIM, RET_DIM, RET_DIM, RET_DIM, RET_DIM)
    offs = [sum(sizes[:i + 1]) for i in range(len(sizes) - 1)]
    u, qa, ka, va, qr, kr, vr, gr = jnp.split(z, offs, axis=-1)
    y_pool, pool_new = pool_mixer(u, pool_prefix, pos, pool_w, pool_scale)
    qa = qa.reshape(n, t, ATT_HEADS, HEAD_DIM)
    ka = ka.reshape(n, t, ATT_HEADS, HEAD_DIM)
    va = va.reshape(n, t, ATT_HEADS, HEAD_DIM)
    if past_k is None:
        kf, vf = ka, va
    else:
        kf = jnp.concatenate([past_k.astype(ka.dtype), ka], axis=1)
        vf = jnp.concatenate([past_v.astype(va.dtype), va], axis=1)
    y_att = lax.map(lambda a: moba_seq(a[0], a[1], a[2], pos), (qa, kf, vf))
    o_r, ret_new = retention(qr.reshape(n, t, RET_HEADS, HEAD_DIM),
                             kr.reshape(n, t, RET_HEADS, HEAD_DIM) * (HEAD_DIM ** -0.5),
                             vr.reshape(n, t, RET_HEADS, HEAD_DIM), ret_s0)
    o_r = o_r * lax.rsqrt(jnp.mean(o_r * o_r, axis=-1, keepdims=True) + EPS)
    y_ret = (jax.nn.silu(gr.astype(jnp.float32)) * o_r.reshape(n, t, RET_DIM)).astype(x.dtype)
    x = x + jnp.concatenate([y_pool.astype(x.dtype), y_att.astype(x.dtype), y_ret], axis=-1) @ w_out
    x = x + 0.5 * swiglu(rmsnorm(x, g_ffn2), f2_in, f2_out)
    return x, ka, va, pool_new, ret_new.astype(x.dtype)


def setup_inputs(seed: int = 0) -> dict:
    key = jax.random.key(seed)
    ks = jax.random.split(key, 20)
    f32 = jnp.float32
    n_pages = PAST_LEN // PAGE_SIZE
    n_pool = (DEC_BATCH * n_pages * 5 + 3) // 4

    def nrm(k, shape, s):
        return jax.random.normal(k, shape, f32) * s

    page_table = jax.random.permutation(ks[6], n_pool)[:DEC_BATCH * n_pages].reshape(DEC_BATCH, n_pages).astype(jnp.int32)
    return {
        "x_prompt": nrm(ks[0], (BATCH, SEQ, D_MODEL), 1.0),
        "x_sample": nrm(ks[1], (DEC_BATCH, DEC_SEQ, D_MODEL), 1.0),
        "cache_k": nrm(ks[2], (DEPTH, n_pool, PAGE_SIZE, ATT_HEADS, HEAD_DIM), 1.0),
        "cache_v": nrm(ks[3], (DEPTH, n_pool, PAGE_SIZE, ATT_HEADS, HEAD_DIM), 1.0),
        "state_pool": nrm(ks[4], (DEPTH, DEC_BATCH, POOL_PAST, POOL_DIM), 1.0),
        "state_ret": nrm(ks[5], (DEPTH, DEC_BATCH, RET_HEADS, HEAD_DIM, HEAD_DIM), 0.1),
        "page_table": page_table,
        "norm_ffn1": 1.0 + nrm(ks[7], (DEPTH, D_MODEL), 0.02),
        "ffn1_w_in": nrm(ks[8], (DEPTH, D_MODEL, 2 * D_FF), D_MODEL ** -0.5),
        "ffn1_w_out": nrm(ks[9], (DEPTH, D_FF, D_MODEL), D_FF ** -0.5),
        "norm_mix": 1.0 + nrm(ks[10], (DEPTH, D_MODEL), 0.02),
        "w_in": nrm(ks[11], (DEPTH, D_MODEL, IN_COLS), D_MODEL ** -0.5),
        "pool_w": nrm(ks[12], (DEPTH, POOL_GROUPS, POOL_GRP, POOL_GRP), POOL_GRP ** -0.5),
        "pool_scale": 1.0 + nrm(ks[13], (DEPTH, POOL_DIM), 0.1),
        "w_out": nrm(ks[14], (DEPTH, MIX_DIM, D_MODEL), MIX_DIM ** -0.5),
        "norm_ffn2": 1.0 + nrm(ks[15], (DEPTH, D_MODEL), 0.02),
        "ffn2_w_in": nrm(ks[16], (DEPTH, D_MODEL, 2 * D_FF), D_MODEL ** -0.5),
        "ffn2_w_out": nrm(ks[17], (DEPTH, D_FF, D_MODEL), D_FF ** -0.5),
        "norm_final": 1.0 + nrm(ks[18], (D_MODEL,), 0.02),
    }


def reference(x_prompt, x_sample, cache_k, cache_v, state_pool, state_ret, page_table,
              norm_ffn1, ffn1_w_in, ffn1_w_out, norm_mix, w_in, pool_w, pool_scale, w_out,
              norm_ffn2, ffn2_w_in, ffn2_w_out, norm_final):
    n_seq_dec, n_pages = page_table.shape
    past_len = n_pages * PAGE_SIZE
    n_prompt = x_prompt.shape[0]
    hp, hs = x_prompt, x_sample
    kp_l, vp_l, pp_l, rp_l, ks_l, vs_l, ps_l, rs_l = [], [], [], [], [], [], [], []
    for l in range(DEPTH):
        w = (norm_ffn1[l], ffn1_w_in[l], ffn1_w_out[l], norm_mix[l], w_in[l], pool_w[l],
             pool_scale[l], w_out[l], norm_ffn2[l], ffn2_w_in[l], ffn2_w_out[l])
        hp, kp, vp, pp, rp = layer_fwd(
            hp, 0, jnp.zeros((n_prompt, POOL_PAST, POOL_DIM), hp.dtype),
            jnp.zeros((n_prompt, RET_HEADS, HEAD_DIM, HEAD_DIM), jnp.float32), None, None, *w)
        past_k = cache_k[l][page_table].reshape(n_seq_dec, past_len, ATT_HEADS, HEAD_DIM)
        past_v = cache_v[l][page_table].reshape(n_seq_dec, past_len, ATT_HEADS, HEAD_DIM)
        hs, ks_, vs_, ps_, rs_ = layer_fwd(hs, past_len, state_pool[l], state_ret[l], past_k, past_v, *w)
        kp_l.append(kp); vp_l.append(vp); pp_l.append(pp); rp_l.append(rp)
        ks_l.append(ks_); vs_l.append(vs_); ps_l.append(ps_); rs_l.append(rs_)
    return (rmsnorm(hp, norm_final), rmsnorm(hs, norm_final),
            jnp.stack(kp_l), jnp.stack(vp_l), jnp.stack(pp_l), jnp.stack(rp_l),
            jnp.stack(ks_l), jnp.stack(vs_l), jnp.stack(ps_l), jnp.stack(rs_l))
```

```python
import functools
import math

import numpy as np
import jax
import jax.numpy as jnp
from jax import lax
from jax.experimental import pallas as pl
from jax.experimental.pallas import tpu as pltpu

F32 = jnp.float32
BF16 = jnp.bfloat16

HEAD_DIM = 64
POOL_WINDOWS = (2, 4, 8, 16)
POOL_GRP = HEAD_DIM
POOL_DIM = len(POOL_WINDOWS) * POOL_GRP
POOL_PAST = max(POOL_WINDOWS) - 1
ATT_HEADS = 6
ATT_DIM = ATT_HEADS * HEAD_DIM
RET_HEADS = 6
RET_DIM = RET_HEADS * HEAD_DIM
MOBA_BLOCK = 256
MOBA_TOPK = 3
PAGE_SIZE = 128
RET_CHUNK = 128
EPS = 1e-6

LANES = 128
SUBLANES = 8
HEADS_PER_TILE = LANES // HEAD_DIM
ATT_PAIRS = ATT_DIM // LANES
RET_PAIRS = RET_DIM // LANES
VMEM_LIMIT = 56 * 1024 * 1024

TOKEN_TILE = 256
FF_CHUNK = 256
POOL_TILE = 512
POOL_HALO = 16
Q_TILE = 128
PAGES_PER_STEP = 8

NT_DIMS = (((1,), (1,)), ((), ()))
TN_DIMS = (((0,), (0,)), ((), ()))


def _params(*semantics):
    return pltpu.CompilerParams(dimension_semantics=semantics, vmem_limit_bytes=VMEM_LIMIT)


def _resident(a):
    return pl.BlockSpec(a.shape, lambda *_: (0,) * a.ndim, pipeline_mode=pl.Buffered(1))


def _rmsnorm(x, g):
    return x * lax.rsqrt(jnp.mean(x * x, axis=-1, keepdims=True) + EPS) * g


def _silu(x):
    return x * (1.0 / (1.0 + jnp.exp(-x)))


def _dot(a, b, dims=None, **kw):
    if dims is None:
        return jnp.dot(a, b, preferred_element_type=F32, **kw)
    return lax.dot_general(a, b, dims, preferred_element_type=F32, **kw)


def _stack_pair(x, lane):
    return jnp.concatenate([jnp.where(lane < HEAD_DIM, x, 0.0), jnp.where(lane >= HEAD_DIM, x, 0.0)], axis=0)


def _unstack_pair(y, lane):
    r = y.shape[0] // 2
    return jnp.where(lane < HEAD_DIM, y[:r], y[r:])


def _pair_column(rows_per_head, values):
    row = lax.broadcasted_iota(jnp.int32, (2 * rows_per_head, 1), 0)
    return jnp.where(row < rows_per_head, values[0], values[1]).astype(F32)


def _alibi_slope(h):
    return 2.0 ** (-8.0 * (h + 1) / ATT_HEADS)


def _swiglu_half_step(x, g_ref, wi_ref, wo_ref):
    d_ff = wo_ref.shape[0]
    h = _rmsnorm(x, g_ref[...]).astype(BF16)
    acc = jnp.zeros(x.shape, F32)
    for c in range(d_ff // FF_CHUNK):
        lo = c * FF_CHUNK
        gate = _dot(h, wi_ref[:, lo:lo + FF_CHUNK])
        up = _dot(h, wi_ref[:, d_ff + lo:d_ff + lo + FF_CHUNK])
        acc = acc + _dot((_silu(gate) * up).astype(BF16), wo_ref[lo:lo + FF_CHUNK, :])
    return x + 0.5 * acc


def _dense_in_kernel(x_ref, g1_ref, f1i_ref, f1o_ref, gm_ref, win_ref, x1_ref, *z_refs):
    x1 = _swiglu_half_step(x_ref[...], g1_ref, f1i_ref, f1o_ref)
    x1_ref[...] = x1
    h = _rmsnorm(x1, gm_ref[...]).astype(BF16)
    off = 0
    for z_ref in z_refs:
        w = z_ref.shape[-1]
        z_ref[...] = _dot(h, win_ref[:, off:off + w])
        off += w


def _dense_in(x, g1, f1i, f1o, gm, win):
    t, d = x.shape
    tm = min(TOKEN_TILE, t)
    widths = (POOL_DIM,) + (ATT_DIM,) * 3 + (RET_DIM,) * 4
    assert t % tm == 0 and sum(widths) == win.shape[1] and f1o.shape[0] % FF_CHUNK == 0
    rows = lambda w: pl.BlockSpec((tm, w), lambda i: (i, 0))
    return pl.pallas_call(
        _dense_in_kernel,
        grid=(t // tm,),
        in_specs=[rows(d)] + [_resident(a) for a in (g1, f1i, f1o, gm, win)],
        out_specs=[rows(d)] + [rows(w) for w in widths],
        out_shape=[jax.ShapeDtypeStruct((t, d), F32)] + [jax.ShapeDtypeStruct((t, w), F32) for w in widths],
        compiler_params=_params("parallel"),
        name="dense_in",
    )(x, g1, f1i, f1o, gm, win)


def _dense_out_kernel(x_ref, yp_ref, ya_ref, yr_ref, wo_ref, g2_ref, f2i_ref, f2o_ref, gf_ref, o_ref, *, final):
    off, mix = 0, None
    for y_ref in (yp_ref, ya_ref, yr_ref):
        w = y_ref.shape[-1]
        part = _dot(y_ref[...].astype(BF16), wo_ref[off:off + w, :])
        mix = part if mix is None else mix + part
        off += w
    x = _swiglu_half_step(x_ref[...] + mix, g2_ref, f2i_ref, f2o_ref)
    if final:
        x = _rmsnorm(x, gf_ref[...])
    o_ref[...] = x


def _dense_out(x, yp, ya, yr, wo, g2, f2i, f2o, gf, final):
    t, d = x.shape
    tm = min(TOKEN_TILE, t)
    assert t % tm == 0
    rows = lambda w: pl.BlockSpec((tm, w), lambda i: (i, 0))
    return pl.pallas_call(
        functools.partial(_dense_out_kernel, final=final),
        grid=(t // tm,),
        in_specs=[rows(d), rows(yp.shape[1]), rows(ya.shape[1]), rows(yr.shape[1])]
        + [_resident(a) for a in (wo, g2, f2i, f2o, gf)],
        out_specs=rows(d),
        out_shape=jax.ShapeDtypeStruct((t, d), F32),
        compiler_params=_params("parallel"),
        name="dense_out",
    )(x, yp, ya, yr, wo, g2, f2i, f2o, gf)


def _pool_kernel(*refs, tt, pos0, has_prefix):
    if has_prefix:
        u_ref, pre_ref, w_ref, scale_ref, y_ref, carry_ref = refs
    else:
        u_ref, w_ref, scale_ref, y_ref, carry_ref = refs
    t = pl.program_id(1)

    @pl.when(t == 0)
    def _():
        carry_ref[...] = pre_ref[...] if has_prefix else jnp.zeros(carry_ref.shape, F32)

    u = u_ref[...]
    ext = jnp.concatenate([carry_ref[...], u], axis=0)
    sums, s = [], ext
    for w in POOL_WINDOWS:
        s = s + pltpu.roll(s, w // 2, axis=0)
        sums.append(s[POOL_HALO:])
    pos = pos0 + t * tt + lax.broadcasted_iota(jnp.int32, (tt, 1), 0)
    lane = lax.broadcasted_iota(jnp.int32, (1, POOL_DIM), 1)
    pooled = None
    for g, w in reversed(list(enumerate(POOL_WINDOWS))):
        mean = sums[g] * (1.0 / jnp.minimum(w, pos + 1).astype(F32))
        pooled = mean if pooled is None else jnp.where(lane < (g + 1) * POOL_GRP, mean, pooled)
    d = (pooled - u).astype(BF16)
    y_ref[...] = _dot(d, w_ref[...]) * scale_ref[...]
    carry_ref[...] = ext[tt:]


def _pool(u, prefix, w_blockdiag, scale, pos0):
    n, t, c = u.shape
    tt = min(POOL_TILE, t)
    assert t % tt == 0 and tt % SUBLANES == 0 and POOL_WINDOWS == (2, 4, 8, 16)
    seq = pl.BlockSpec((None, tt, c), lambda i, j: (i, j, 0))
    in_specs, args = [seq], [u]
    if prefix is not None:
        in_specs.append(pl.BlockSpec((None, POOL_HALO, c), lambda i, j: (i, 0, 0)))
        args.append(prefix)
    in_specs += [_resident(w_blockdiag), _resident(scale)]
    args += [w_blockdiag, scale]
    return pl.pallas_call(
        functools.partial(_pool_kernel, tt=tt, pos0=pos0, has_prefix=prefix is not None),
        grid=(n, t // tt),
        in_specs=in_specs,
        out_specs=seq,
        out_shape=jax.ShapeDtypeStruct((n, t, c), F32),
        scratch_shapes=[pltpu.VMEM((POOL_HALO, c), F32)],
        compiler_params=_params("parallel", "arbitrary"),
        name="pool_mixer",
    )(*args)


def _topk_select(gate, j, n_valid, blk):
    gj = gate[:, j:j + 1]
    ahead = jnp.where(blk < j, jnp.where(gate >= gj, 1.0, 0.0), jnp.where(gate > gj, 1.0, 0.0))
    count = jnp.sum(jnp.where(blk < n_valid, ahead, 0.0), axis=1, keepdims=True)
    return count < MOBA_TOPK


def _moba_prompt_kernel(q_ref, k_ref, v_ref, o_ref, kmean_ref, m_ref, l_ref, acc_ref, *, n_blocks):
    qi = pl.program_id(1)
    tq = q_ref.shape[0]
    q0 = qi * tq
    own = q0 // MOBA_BLOCK

    @pl.when(qi == 0)
    def _():
        kmean_ref[...] = jnp.zeros(kmean_ref.shape, F32)
        for j in range(n_blocks):
            kmean_ref[j:j + 1, :] = jnp.mean(k_ref[j * MOBA_BLOCK:(j + 1) * MOBA_BLOCK, :], axis=0, keepdims=True)

    lane = lax.broadcasted_iota(jnp.int32, (tq, LANES), 1)
    blk = lax.broadcasted_iota(jnp.int32, (1, LANES), 1)
    kcol = lax.broadcasted_iota(jnp.int32, (1, MOBA_BLOCK), 1)
    qrow = q0 + lax.rem(lax.broadcasted_iota(jnp.int32, (2 * tq, 1), 0), tq)

    for p in range(ATT_PAIRS):
        cols = slice(p * LANES, (p + 1) * LANES)
        slope = _pair_column(tq, (_alibi_slope(2 * p), _alibi_slope(2 * p + 1)))
        qs = _stack_pair(q_ref[:, cols] * HEAD_DIM ** -0.5, lane)
        qs_b = qs.astype(BF16)
        gate = _dot(qs, kmean_ref[:, cols], NT_DIMS, precision=lax.Precision.HIGHEST)

        def scores(k_blk, first_pos):
            s = _dot(qs_b, k_blk.astype(BF16), NT_DIMS)
            return s + slope * (kcol + (first_pos - q0)).astype(F32)

        k0 = pl.multiple_of(own * MOBA_BLOCK, MOBA_BLOCK)
        s = scores(k_ref[pl.ds(k0, MOBA_BLOCK), cols], k0)
        s = jnp.where(kcol + k0 <= qrow, s, -jnp.inf)
        m = jnp.max(s, axis=1, keepdims=True)
        e = jnp.exp(s - m)
        m_ref[...] = m
        l_ref[...] = jnp.sum(e, axis=1, keepdims=True)
        acc_ref[...] = _dot(e.astype(BF16), v_ref[pl.ds(k0, MOBA_BLOCK), cols].astype(BF16))

        for j in range(n_blocks - 1):
            @pl.when(j < own)
            def _(j=j):
                rows = slice(j * MOBA_BLOCK, (j + 1) * MOBA_BLOCK)
                sel = _topk_select(gate, j, own, blk)
                s = jnp.where(sel, scores(k_ref[rows, cols], j * MOBA_BLOCK), -jnp.inf)
                m_old = m_ref[...]
                m_new = jnp.maximum(m_old, jnp.max(s, axis=1, keepdims=True))
                alpha = jnp.exp(m_old - m_new)
                e = jnp.exp(s - m_new)
                m_ref[...] = m_new
                l_ref[...] = alpha * l_ref[...] + jnp.sum(e, axis=1, keepdims=True)
                acc_ref[...] = alpha * acc_ref[...] + _dot(e.astype(BF16), v_ref[rows, cols].astype(BF16))

        o_ref[:, cols] = _unstack_pair(acc_ref[...] / l_ref[...], lane)


def _moba_prompt(q, k, v):
    n, t, c = q.shape
    tq = min(Q_TILE, t)
    assert t % MOBA_BLOCK == 0 and MOBA_BLOCK % tq == 0 and t // MOBA_BLOCK <= LANES
    tile = pl.BlockSpec((None, tq, c), lambda i, j: (i, j, 0))
    whole = pl.BlockSpec((None, t, c), lambda i, j: (i, 0, 0))
    return pl.pallas_call(
        functools.partial(_moba_prompt_kernel, n_blocks=t // MOBA_BLOCK),
        grid=(n, t // tq),
        in_specs=[tile, whole, whole],
        out_specs=tile,
        out_shape=jax.ShapeDtypeStruct((n, t, c), F32),
        scratch_shapes=[pltpu.VMEM((LANES, c), F32), pltpu.VMEM((2 * tq, 1), F32), pltpu.VMEM((2 * tq, 1), F32),
                        pltpu.VMEM((2 * tq, LANES), F32)],
        compiler_params=_params("parallel", "arbitrary"),
        name="moba_prompt",
    )(q, k, v)


def _moba_sample_kernel(pt_ref, q_ref, kn_ref, vn_ref, *refs, n_past_blocks):
    del pt_ref
    k_pages = refs[:PAGES_PER_STEP]
    v_pages = refs[PAGES_PER_STEP:2 * PAGES_PER_STEP]
    o_ref, kmean_ref, m_ref, l_ref, acc_ref = refs[2 * PAGES_PER_STEP:]
    c = pl.program_id(1)
    t = q_ref.shape[0]
    pages_per_block = MOBA_BLOCK // PAGE_SIZE
    blocks_per_step = PAGES_PER_STEP // pages_per_block
    past_len = n_past_blocks * MOBA_BLOCK

    lane = lax.broadcasted_iota(jnp.int32, (t, LANES), 1)
    blk = lax.broadcasted_iota(jnp.int32, (1, LANES), 1)
    blk_row = lax.broadcasted_iota(jnp.int32, (LANES, 1), 0)
    kcol = lax.broadcasted_iota(jnp.int32, (1, MOBA_BLOCK), 1)

    @pl.when(c == 0)
    def _():
        kmean_ref[...] = jnp.zeros(kmean_ref.shape, F32)

    def stacked_q(p):
        return _stack_pair(q_ref[:, p * LANES:(p + 1) * LANES] * HEAD_DIM ** -0.5, lane)

    def slopes(p):
        return _pair_column(t, (_alibi_slope(2 * p), _alibi_slope(2 * p + 1)))

    for b in range(blocks_per_step):
        j = c * blocks_per_step + b
        pages = range(b * pages_per_block, (b + 1) * pages_per_block)
        k_blk = jnp.concatenate([k_pages[i][...] for i in pages], axis=0)
        v_blk = jnp.concatenate([v_pages[i][...] for i in pages], axis=0)
        kmean_ref[...] = jnp.where(blk_row == j, jnp.mean(k_blk, axis=0, keepdims=True), kmean_ref[...])
        kpos = (kcol + (j * MOBA_BLOCK - past_len)).astype(F32)
        for p in range(ATT_PAIRS):
            cols = slice(p * LANES, (p + 1) * LANES)
            s = _dot(stacked_q(p).astype(BF16), k_blk[:, cols].astype(BF16), NT_DIMS) + slopes(p) * kpos
            m = jnp.max(s, axis=1, keepdims=True)
            e = jnp.exp(s - m)
            m_ref[p, j] = m
            l_ref[p, j] = jnp.sum(e, axis=1, keepdims=True)
            acc_ref[p, j] = _dot(e.astype(BF16), v_blk[:, cols].astype(BF16))

    @pl.when(c == pl.num_programs(1) - 1)
    def _():
        pad = jnp.zeros((PAGE_SIZE - t, kn_ref.shape[1]), F32)
        k_new = jnp.concatenate([kn_ref[...], pad], axis=0)
        v_new = jnp.concatenate([vn_ref[...], pad], axis=0)
        ncol = lax.broadcasted_iota(jnp.int32, (1, PAGE_SIZE), 1)
        qrow = lax.rem(lax.broadcasted_iota(jnp.int32, (2 * t, 1), 0), t)
        for p in range(ATT_PAIRS):
            cols = slice(p * LANES, (p + 1) * LANES)
            qs = stacked_q(p)
            gate = _dot(qs, kmean_ref[:, cols], NT_DIMS, precision=lax.Precision.HIGHEST)
            s = _dot(qs.astype(BF16), k_new[:, cols].astype(BF16), NT_DIMS) + slopes(p) * ncol.astype(F32)
            s = jnp.where(ncol <= qrow, s, -jnp.inf)
            m = jnp.max(s, axis=1, keepdims=True)
            e = jnp.exp(s - m)
            sel = [_topk_select(gate, j, n_past_blocks, blk) for j in range(n_past_blocks)]
            m_all = m
            for j in range(n_past_blocks):
                m_all = jnp.maximum(m_all, jnp.where(sel[j], m_ref[p, j], -jnp.inf))
            w = jnp.exp(m - m_all)
            den = w * jnp.sum(e, axis=1, keepdims=True)
            num = w * _dot(e.astype(BF16), v_new[:, cols].astype(BF16))
            for j in range(n_past_blocks):
                w = jnp.where(sel[j], jnp.exp(m_ref[p, j] - m_all), 0.0)
                den = den + w * l_ref[p, j]
                num = num + w * acc_ref[p, j]
            o_ref[:, cols] = _unstack_pair(num / den, lane)


def _moba_sample(q, k_new, v_new, cache_k, cache_v, page_ids):
    n, t, c = q.shape
    pages_per_seq = page_ids.shape[0] // n
    past_len = pages_per_seq * PAGE_SIZE
    n_past_blocks = past_len // MOBA_BLOCK
    assert past_len % MOBA_BLOCK == 0 and t <= PAGE_SIZE and t % SUBLANES == 0
    assert pages_per_seq % PAGES_PER_STEP == 0 and MOBA_TOPK <= n_past_blocks <= LANES
    new = pl.BlockSpec((None, t, c), lambda i, j, pt: (i, 0, 0))

    def page(slot):
        return pl.BlockSpec((None, PAGE_SIZE, c),
                            lambda i, j, pt: (pt[i * pages_per_seq + j * PAGES_PER_STEP + slot], 0, 0))

    pages = [page(s) for s in range(PAGES_PER_STEP)]
    stat = pltpu.VMEM((ATT_PAIRS, n_past_blocks, 2 * t, 1), F32)
    return pl.pallas_call(
        functools.partial(_moba_sample_kernel, n_past_blocks=n_past_blocks),
        grid_spec=pltpu.PrefetchScalarGridSpec(
            num_scalar_prefetch=1,
            grid=(n, pages_per_seq // PAGES_PER_STEP),
            in_specs=[new, new, new] + pages + pages,
            out_specs=new,
            scratch_shapes=[pltpu.VMEM((LANES, c), F32), stat, stat,
                            pltpu.VMEM((ATT_PAIRS, n_past_blocks, 2 * t, LANES), F32)],
        ),
        out_shape=jax.ShapeDtypeStruct((n, t, c), F32),
        compiler_params=_params("parallel", "arbitrary"),
        name="moba_sample",
    )(page_ids, q, k_new, v_new, *([cache_k] * PAGES_PER_STEP), *([cache_v] * PAGES_PER_STEP))


def _retention_constants(c):
    lg = np.log1p(-np.exp2(-5.0 - np.arange(RET_HEADS, dtype=np.float64)))
    idx = np.arange(c, dtype=np.float64)
    diff = idx[:, None] - idx[None, :]
    d_in = np.where(diff >= 0, np.exp(lg[:, None, None] * np.maximum(diff, 0.0)), 0.0)
    q_dec = np.exp(lg[:, None] * (idx + 1.0))
    k_dec = np.exp(lg[:, None] * (c - 1.0 - idx))
    c_dec = np.exp(lg * c)
    per_lane = lambda a: np.repeat(a.reshape(RET_PAIRS, HEADS_PER_TILE, -1).transpose(0, 2, 1), HEAD_DIM, axis=2)
    block = np.kron(np.eye(HEADS_PER_TILE), np.ones((HEAD_DIM, HEAD_DIM)))
    state_dec = block[None] * np.repeat(c_dec.reshape(RET_PAIRS, HEADS_PER_TILE), HEAD_DIM, axis=1)[:, :, None]
    as_f32 = lambda a: jnp.asarray(a, F32)
    return (as_f32(d_in.reshape(RET_PAIRS, HEADS_PER_TILE * c, c)), as_f32(per_lane(q_dec)), as_f32(per_lane(k_dec)),
            as_f32(state_dec), as_f32(block))


def _retention_kernel(*refs, has_state):
    if has_state:
        q_ref, k_ref, v_ref, g_ref, s0_ref, din_ref, qd_ref, kd_ref, sd_ref, blk_ref, y_ref, sout_ref, s_ref = refs
    else:
        q_ref, k_ref, v_ref, g_ref, din_ref, qd_ref, kd_ref, sd_ref, blk_ref, y_ref, sout_ref, s_ref = refs
    ci = pl.program_id(1)
    c = q_ref.shape[0]

    @pl.when(ci == 0)
    def _():
        s_ref[...] = s0_ref[...] if has_state else jnp.zeros(s_ref.shape, F32)

    lane = lax.broadcasted_iota(jnp.int32, (c, LANES), 1)
    for p in range(RET_PAIRS):
        cols = slice(p * LANES, (p + 1) * LANES)
        q = q_ref[:, cols]
        k = k_ref[:, cols] * HEAD_DIM ** -0.5
        v = v_ref[:, cols].astype(BF16)
        state = s_ref[p]
        att = _dot(_stack_pair(q, lane).astype(BF16), k.astype(BF16), NT_DIMS) * din_ref[p]
        o = _unstack_pair(_dot(att.astype(BF16), v), lane)
        o = o + _dot((q * qd_ref[p]).astype(BF16), state.astype(BF16))
        s_ref[p] = state * sd_ref[p] + _dot((k * kd_ref[p]).astype(BF16), v, TN_DIMS) * blk_ref[...]
        sq = o * o
        even = jnp.sum(jnp.where(lane < HEAD_DIM, sq, 0.0), axis=1, keepdims=True)
        odd = jnp.sum(jnp.where(lane >= HEAD_DIM, sq, 0.0), axis=1, keepdims=True)
        ms = jnp.where(lane < HEAD_DIM, even, odd) * (1.0 / HEAD_DIM)
        y_ref[:, cols] = _silu(g_ref[:, cols]) * (o * lax.rsqrt(ms + EPS))

    @pl.when(ci == pl.num_programs(1) - 1)
    def _():
        sout_ref[...] = s_ref[...]


def _retention(q, k, v, g, state):
    n, t, d = q.shape
    c = math.gcd(t, RET_CHUNK)
    consts = _retention_constants(c)
    chunk = pl.BlockSpec((None, c, d), lambda i, j: (i, j, 0))
    st = pl.BlockSpec((None, RET_PAIRS, LANES, LANES), lambda i, j: (i, 0, 0, 0))
    in_specs, args = [chunk] * 4, [q, k, v, g]
    if state is not None:
        in_specs.append(st)
        args.append(state)
    in_specs += [_resident(a) for a in consts]
    args += list(consts)
    return pl.pallas_call(
        functools.partial(_retention_kernel, has_state=state is not None),
        grid=(n, t // c),
        in_specs=in_specs,
        out_specs=[chunk, st],
        out_shape=[jax.ShapeDtypeStruct((n, t, d), F32), jax.ShapeDtypeStruct((n, RET_PAIRS, LANES, LANES), F32)],
        scratch_shapes=[pltpu.VMEM((RET_PAIRS, LANES, LANES), F32)],
        compiler_params=_params("parallel", "arbitrary"),
        name="retention",
    )(*args)


def _state_to_pairs(s):
    n = s.shape[0]
    s = s.astype(F32).reshape(n, RET_PAIRS, HEADS_PER_TILE, HEAD_DIM, HEAD_DIM)
    eye = jnp.eye(HEADS_PER_TILE, dtype=F32)
    return jnp.einsum("npaij,ab->npaibj", s, eye).reshape(n, RET_PAIRS, LANES, LANES)


def _state_from_pairs(s):
    n = s.shape[0]
    s = s.reshape(n, RET_PAIRS, HEADS_PER_TILE, HEAD_DIM, HEADS_PER_TILE, HEAD_DIM)
    return jnp.stack([s[:, :, a, :, a, :] for a in range(HEADS_PER_TILE)], axis=2).reshape(n, RET_HEADS, HEAD_DIM, HEAD_DIM)


def _layer(x, n, pool_prefix, ret_state, moba, w, final_norm, final):
    t = x.shape[0] // n
    x1, u, qa, ka, va, qr, kr, vr, gr = _dense_in(x, w["g1"], w["f1i"], w["f1o"], w["gm"], w["win"])
    seq = lambda a: a.reshape(n, t, a.shape[-1])
    u = seq(u)
    pos0 = 0 if pool_prefix is None else moba["past_len"]
    if pool_prefix is None:
        y_pool, pool_new = _pool(u, None, w["pool_w"], w["pool_scale"], pos0), u[:, t - POOL_PAST:]
    else:
        halo = jnp.pad(pool_prefix.astype(F32), ((0, 0), (POOL_HALO - POOL_PAST, 0), (0, 0)))
        y_pool = _pool(u, halo, w["pool_w"], w["pool_scale"], pos0)
        pool_new = jnp.concatenate([pool_prefix.astype(F32), u], axis=1)[:, -POOL_PAST:]
    if moba is None:
        y_att = _moba_prompt(seq(qa), seq(ka), seq(va))
    else:
        y_att = _moba_sample(seq(qa), seq(ka), seq(va), moba["cache_k"], moba["cache_v"], moba["page_ids"])
    y_ret, ret_new = _retention(seq(qr), seq(kr), seq(vr), seq(gr), ret_state)
    flat = lambda a: a.reshape(n * t, a.shape[-1])
    x2 = _dense_out(x1, flat(y_pool), flat(y_att), flat(y_ret), w["wo"], w["g2"], w["f2i"], w["f2o"], final_norm, final)
    heads = lambda a: a.reshape(n, t, ATT_HEADS, HEAD_DIM)
    return x2, heads(ka), heads(va), pool_new, _state_from_pairs(ret_new)


def kernel(x_prompt, x_sample, cache_k, cache_v, state_pool, state_ret, page_table, norm_ffn1, ffn1_w_in, ffn1_w_out,
           norm_mix, w_in, pool_w, pool_scale, w_out, norm_ffn2, ffn2_w_in, ffn2_w_out, norm_final):
    depth = w_in.shape[0]
    n_p, t_p, d = x_prompt.shape
    n_s, t_s, _ = x_sample.shape
    n_pool = cache_k.shape[1]
    pages_per_seq = page_table.shape[1]
    assert cache_k.shape[2] == PAGE_SIZE and d % LANES == 0
    cache_k = cache_k.reshape(depth * n_pool, PAGE_SIZE, ATT_DIM)
    cache_v = cache_v.reshape(depth * n_pool, PAGE_SIZE, ATT_DIM)
    row = lambda a: a.reshape(1, -1).astype(F32)
    final_norm = row(norm_final)
    hp, hs = x_prompt.reshape(n_p * t_p, d), x_sample.reshape(n_s * t_s, d)
    outs = [[] for _ in range(8)]
    for l in range(depth):
        w = dict(g1=row(norm_ffn1[l]), f1i=ffn1_w_in[l].astype(BF16), f1o=ffn1_w_out[l].astype(BF16),
                 gm=row(norm_mix[l]), win=w_in[l].astype(BF16),
                 pool_w=jax.scipy.linalg.block_diag(*pool_w[l]).astype(BF16), pool_scale=row(pool_scale[l]),
                 wo=w_out[l].astype(BF16), g2=row(norm_ffn2[l]), f2i=ffn2_w_in[l].astype(BF16),
                 f2o=ffn2_w_out[l].astype(BF16))
        final = l == depth - 1
        hp, kp, vp, pp, rp = _layer(hp, n_p, None, None, None, w, final_norm, final)
        moba = dict(cache_k=cache_k, cache_v=cache_v, past_len=pages_per_seq * PAGE_SIZE,
                    page_ids=(page_table.astype(jnp.int32) + l * n_pool).reshape(-1))
        hs, ks, vs, ps, rs = _layer(hs, n_s, state_pool[l], _state_to_pairs(state_ret[l]), moba, w, final_norm, final)
        for lst, a in zip(outs, (kp, vp, pp, rp, ks, vs, ps, rs)):
            lst.append(a)
    return (hp.reshape(n_p, t_p, d), hs.reshape(n_s, t_s, d)) + tuple(jnp.stack(lst) for lst in outs)
```

```python
import functools
import math

import numpy as np
import jax
import jax.numpy as jnp
from jax import lax
from jax.experimental import pallas as pl
from jax.experimental.pallas import tpu as pltpu

F32 = jnp.float32
BF16 = jnp.bfloat16

HEAD_DIM = 64
POOL_WINDOWS = (2, 4, 8, 16)
POOL_GRP = HEAD_DIM
POOL_DIM = len(POOL_WINDOWS) * POOL_GRP
POOL_PAST = max(POOL_WINDOWS) - 1
ATT_HEADS = 6
ATT_DIM = ATT_HEADS * HEAD_DIM
RET_HEADS = 6
RET_DIM = RET_HEADS * HEAD_DIM
MOBA_BLOCK = 256
MOBA_TOPK = 3
PAGE_SIZE = 128
RET_CHUNK = 128
EPS = 1e-6

LANES = 128
SUBLANES = 8
HEADS_PER_TILE = LANES // HEAD_DIM
ATT_PAIRS = ATT_DIM // LANES
RET_PAIRS = RET_DIM // LANES
VMEM_LIMIT = 56 * 1024 * 1024

TOKEN_TILE = 256
FF_CHUNK = 256
POOL_TILE = 512
POOL_HALO = 16
Q_TILE = 128
PAGES_PER_STEP = 8

NT_DIMS = (((1,), (1,)), ((), ()))
TN_DIMS = (((0,), (0,)), ((), ()))


def _params(*semantics):
    return pltpu.CompilerParams(dimension_semantics=semantics, vmem_limit_bytes=VMEM_LIMIT)


def _resident(a):
    return pl.BlockSpec(a.shape, lambda *_: (0,) * a.ndim, pipeline_mode=pl.Buffered(1))


def _rmsnorm(x, g):
    return x * lax.rsqrt(jnp.mean(x * x, axis=-1, keepdims=True) + EPS) * g


def _silu(x):
    return x * (1.0 / (1.0 + jnp.exp(-x)))


def _dot(a, b, dims=None, **kw):
    if dims is None:
        return jnp.dot(a, b, preferred_element_type=F32, **kw)
    return lax.dot_general(a, b, dims, preferred_element_type=F32, **kw)


def _stack_pair(x, lane):
    return jnp.concatenate([jnp.where(lane < HEAD_DIM, x, 0.0), jnp.where(lane >= HEAD_DIM, x, 0.0)], axis=0)


def _unstack_pair(y, lane):
    r = y.shape[0] // 2
    return jnp.where(lane < HEAD_DIM, y[:r], y[r:])


def _pair_column(rows_per_head, values):
    row = lax.broadcasted_iota(jnp.int32, (2 * rows_per_head, 1), 0)
    return jnp.where(row < rows_per_head, values[0], values[1]).astype(F32)


def _alibi_slope(h):
    return 2.0 ** (-8.0 * (h + 1) / ATT_HEADS)


def _swiglu_half_step(x, g_ref, wi_ref, wo_ref):
    d_ff = wo_ref.shape[0]
    h = _rmsnorm(x, g_ref[...]).astype(BF16)
    acc = jnp.zeros(x.shape, F32)
    for c in range(d_ff // FF_CHUNK):
        lo = c * FF_CHUNK
        gate = _dot(h, wi_ref[:, lo:lo + FF_CHUNK])
        up = _dot(h, wi_ref[:, d_ff + lo:d_ff + lo + FF_CHUNK])
        acc = acc + _dot((_silu(gate) * up).astype(BF16), wo_ref[lo:lo + FF_CHUNK, :])
    return x + 0.5 * acc


def _dense_in_kernel(x_ref, g1_ref, f1i_ref, f1o_ref, gm_ref, win_ref, x1_ref, *z_refs):
    x1 = _swiglu_half_step(x_ref[...], g1_ref, f1i_ref, f1o_ref)
    x1_ref[...] = x1
    h = _rmsnorm(x1, gm_ref[...]).astype(BF16)
    off = 0
    for z_ref in z_refs:
        w = z_ref.shape[-1]
        z_ref[...] = _dot(h, win_ref[:, off:off + w])
        off += w


def _dense_in(x, g1, f1i, f1o, gm, win):
    t, d = x.shape
    tm = min(TOKEN_TILE, t)
    widths = (POOL_DIM,) + (ATT_DIM,) * 3 + (RET_DIM,) * 4
    assert t % tm == 0 and sum(widths) == win.shape[1] and f1o.shape[0] % FF_CHUNK == 0
    rows = lambda w: pl.BlockSpec((tm, w), lambda i: (i, 0))
    return pl.pallas_call(
        _dense_in_kernel,
        grid=(t // tm,),
        in_specs=[rows(d)] + [_resident(a) for a in (g1, f1i, f1o, gm, win)],
        out_specs=[rows(d)] + [rows(w) for w in widths],
        out_shape=[jax.ShapeDtypeStruct((t, d), F32)] + [jax.ShapeDtypeStruct((t, w), F32) for w in widths],
        compiler_params=_params("parallel"),
        name="dense_in",
    )(x, g1, f1i, f1o, gm, win)


def _dense_out_kernel(x_ref, yp_ref, ya_ref, yr_ref, wo_ref, g2_ref, f2i_ref, f2o_ref, gf_ref, o_ref, *, final):
    off, mix = 0, None
    for y_ref in (yp_ref, ya_ref, yr_ref):
        w = y_ref.shape[-1]
        part = _dot(y_ref[...].astype(BF16), wo_ref[off:off + w, :])
        mix = part if mix is None else mix + part
        off += w
    x = _swiglu_half_step(x_ref[...] + mix, g2_ref, f2i_ref, f2o_ref)
    if final:
        x = _rmsnorm(x, gf_ref[...])
    o_ref[...] = x


def _dense_out(x, yp, ya, yr, wo, g2, f2i, f2o, gf, final):
    t, d = x.shape
    tm = min(TOKEN_TILE, t)
    assert t % tm == 0
    rows = lambda w: pl.BlockSpec((tm, w), lambda i: (i, 0))
    return pl.pallas_call(
        functools.partial(_dense_out_kernel, final=final),
        grid=(t // tm,),
        in_specs=[rows(d), rows(yp.shape[1]), rows(ya.shape[1]), rows(yr.shape[1])]
        + [_resident(a) for a in (wo, g2, f2i, f2o, gf)],
        out_specs=rows(d),
        out_shape=jax.ShapeDtypeStruct((t, d), F32),
        compiler_params=_params("parallel"),
        name="dense_out",
    )(x, yp, ya, yr, wo, g2, f2i, f2o, gf)


def _pool_kernel(*refs, tt, pos0, has_prefix):
    if has_prefix:
        u_ref, pre_ref, w_ref, scale_ref, y_ref, carry_ref = refs
    else:
        u_ref, w_ref, scale_ref, y_ref, carry_ref = refs
    t = pl.program_id(1)

    @pl.when(t == 0)
    def _():
        carry_ref[...] = pre_ref[...] if has_prefix else jnp.zeros(carry_ref.shape, F32)

    u = u_ref[...]
    ext = jnp.concatenate([carry_ref[...], u], axis=0)
    sums, s = [], ext
    for w in POOL_WINDOWS:
        s = s + pltpu.roll(s, w // 2, axis=0)
        sums.append(s[POOL_HALO:])
    pos = pos0 + t * tt + lax.broadcasted_iota(jnp.int32, (tt, 1), 0)
    lane = lax.broadcasted_iota(jnp.int32, (1, POOL_DIM), 1)
    pooled = None
    for g, w in reversed(list(enumerate(POOL_WINDOWS))):
        mean = sums[g] * (1.0 / jnp.minimum(w, pos + 1).astype(F32))
        pooled = mean if pooled is None else jnp.where(lane < (g + 1) * POOL_GRP, mean, pooled)
    d = (pooled - u).astype(BF16)
    y_ref[...] = _dot(d, w_ref[...]) * scale_ref[...]
    carry_ref[...] = ext[tt:]


def _pool(u, prefix, w_blockdiag, scale, pos0):
    n, t, c = u.shape
    tt = min(POOL_TILE, t)
    assert t % tt == 0 and tt % SUBLANES == 0 and POOL_WINDOWS == (2, 4, 8, 16)
    seq = pl.BlockSpec((None, tt, c), lambda i, j: (i, j, 0))
    in_specs, args = [seq], [u]
    if prefix is not None:
        in_specs.append(pl.BlockSpec((None, POOL_HALO, c), lambda i, j: (i, 0, 0)))
        args.append(prefix)
    in_specs += [_resident(w_blockdiag), _resident(scale)]
    args += [w_blockdiag, scale]
    return pl.pallas_call(
        functools.partial(_pool_kernel, tt=tt, pos0=pos0, has_prefix=prefix is not None),
        grid=(n, t // tt),
        in_specs=in_specs,
        out_specs=seq,
        out_shape=jax.ShapeDtypeStruct((n, t, c), F32),
        scratch_shapes=[pltpu.VMEM((POOL_HALO, c), F32)],
        compiler_params=_params("parallel", "arbitrary"),
        name="pool_mixer",
    )(*args)


def _topk_select(gate, j, n_valid, blk):
    gj = gate[:, j:j + 1]
    ahead = jnp.where(blk < j, jnp.where(gate >= gj, 1.0, 0.0), jnp.where(gate > gj, 1.0, 0.0))
    count = jnp.sum(jnp.where(blk < n_valid, ahead, 0.0), axis=1, keepdims=True)
    return count < MOBA_TOPK


def _moba_prompt_kernel(q_ref, k_ref, v_ref, o_ref, kmean_ref, kb_ref, vt_ref, qst_ref, sel_ref, bias_ref,
                        m_ref, l_ref, acc_ref, *, n_blocks):
    qi = pl.program_id(1)
    tq = q_ref.shape[0]
    q0 = qi * tq
    own = q0 // MOBA_BLOCK

    @pl.when(qi == 0)
    def _():
        kmean_ref[...] = jnp.zeros(kmean_ref.shape, F32)
        for j in range(n_blocks):
            k_blk = k_ref[j * MOBA_BLOCK:(j + 1) * MOBA_BLOCK, :]
            kmean_ref[j:j + 1, :] = jnp.mean(k_blk, axis=0, keepdims=True)
            kb_ref[j] = k_blk.astype(BF16)
            vt_ref[j] = v_ref[j * MOBA_BLOCK:(j + 1) * MOBA_BLOCK, :].T.astype(BF16)

    dim = lax.broadcasted_iota(jnp.int32, (LANES, 1), 0)
    key = lax.broadcasted_iota(jnp.int32, (MOBA_BLOCK, 2 * tq), 0)
    col = lax.broadcasted_iota(jnp.int32, (1, 2 * tq), 1)
    blk = lax.broadcasted_iota(jnp.int32, (kmean_ref.shape[0], 1), 0)
    q_in_blk = (q0 - own * MOBA_BLOCK) + lax.rem(col, tq)
    slopes = [jnp.where(col < tq, _alibi_slope(2 * p), _alibi_slope(2 * p + 1)).astype(F32) for p in range(ATT_PAIRS)]

    for p in range(ATT_PAIRS):
        cols = slice(p * LANES, (p + 1) * LANES)
        q_t = (q_ref[:, cols] * HEAD_DIM ** -0.5).T
        qs_t = jnp.concatenate([jnp.where(dim < HEAD_DIM, q_t, 0.0), jnp.where(dim >= HEAD_DIM, q_t, 0.0)], axis=1)
        qst_ref[p] = qs_t.astype(BF16)
        gate = _dot(kmean_ref[:, cols], qs_t, precision=lax.Precision.HIGHEST)
        sel = jnp.zeros(gate.shape, F32)
        for j in range(n_blocks - 1):
            gj = gate[j:j + 1, :]
            ahead = jnp.where(blk < j, jnp.where(gate >= gj, 1.0, 0.0), jnp.where(gate > gj, 1.0, 0.0))
            count = jnp.sum(jnp.where(blk < own, ahead, 0.0), axis=0, keepdims=True)
            sel = jnp.where(blk == j, jnp.where(count < MOBA_TOPK, 1.0, 0.0), sel)
        sel_ref[p] = sel
        bias_ref[p] = slopes[p] * key.astype(F32)

    def partial_softmax(p, j, first_pos, mask):
        cols = slice(p * LANES, (p + 1) * LANES)
        t = _dot(kb_ref[j, :, cols], qst_ref[p]) + bias_ref[p]
        t = jnp.where(mask(p), t, -jnp.inf)
        c = slopes[p] * (first_pos - q0).astype(F32)
        return t, jnp.max(t, axis=0, keepdims=True) + c, c

    for p in range(ATT_PAIRS):
        t, m, c = partial_softmax(p, own, own * MOBA_BLOCK, lambda p: key <= q_in_blk)
        e = jnp.exp(t - (m - c))
        m_ref[p] = m
        l_ref[p] = jnp.sum(e, axis=0, keepdims=True)
        acc_ref[p] = _dot(vt_ref[own, p * LANES:(p + 1) * LANES, :], e.astype(BF16))

    for j in range(n_blocks - 1):
        @pl.when(j < own)
        def _(j=j):
            for p in range(ATT_PAIRS):
                t, m_blk, c = partial_softmax(p, j, jnp.int32(j * MOBA_BLOCK), lambda p: sel_ref[p, j:j + 1, :] > 0.0)
                m_old = m_ref[p]
                m_new = jnp.maximum(m_old, m_blk)
                alpha = jnp.exp(m_old - m_new)
                e = jnp.exp(t - (m_new - c))
                m_ref[p] = m_new
                l_ref[p] = alpha * l_ref[p] + jnp.sum(e, axis=0, keepdims=True)
                acc_ref[p] = alpha * acc_ref[p] + _dot(vt_ref[j, p * LANES:(p + 1) * LANES, :], e.astype(BF16))

    for p in range(ATT_PAIRS):
        out_t = acc_ref[p] / l_ref[p]
        o_ref[:, p * LANES:(p + 1) * LANES] = jnp.where(dim < HEAD_DIM, out_t[:, :tq], out_t[:, tq:]).T


def _moba_prompt(q, k, v):
    n, t, c = q.shape
    tq = min(Q_TILE, t)
    n_blocks = t // MOBA_BLOCK
    gate_rows = -(-n_blocks // SUBLANES) * SUBLANES
    assert t % MOBA_BLOCK == 0 and MOBA_BLOCK % tq == 0 and tq % LANES == 0 and c == ATT_DIM
    tile = pl.BlockSpec((None, tq, c), lambda i, j: (i, j, 0))
    whole = pl.BlockSpec((None, t, c), lambda i, j: (i, 0, 0))
    return pl.pallas_call(
        functools.partial(_moba_prompt_kernel, n_blocks=n_blocks),
        grid=(n, t // tq),
        in_specs=[tile, whole, whole],
        out_specs=tile,
        out_shape=jax.ShapeDtypeStruct((n, t, c), F32),
        scratch_shapes=[pltpu.VMEM((gate_rows, c), F32),
                        pltpu.VMEM((n_blocks, MOBA_BLOCK, c), BF16),
                        pltpu.VMEM((n_blocks, c, MOBA_BLOCK), BF16),
                        pltpu.VMEM((ATT_PAIRS, LANES, 2 * tq), BF16),
                        pltpu.VMEM((ATT_PAIRS, gate_rows, 2 * tq), F32),
                        pltpu.VMEM((ATT_PAIRS, MOBA_BLOCK, 2 * tq), F32),
                        pltpu.VMEM((ATT_PAIRS, 1, 2 * tq), F32), pltpu.VMEM((ATT_PAIRS, 1, 2 * tq), F32),
                        pltpu.VMEM((ATT_PAIRS, LANES, 2 * tq), F32)],
        compiler_params=_params("parallel", "arbitrary"),
        name="moba_prompt",
    )(q, k, v)


def _head_rows(x):
    t = x.shape[0]
    lane_head = lax.broadcasted_iota(jnp.int32, (1, ATT_DIM), 1) // HEAD_DIM
    row_head = lax.broadcasted_iota(jnp.int32, (ATT_HEADS * t, 1), 0) // t
    return jnp.where(lane_head == row_head, jnp.concatenate([x] * ATT_HEADS, axis=0), 0.0)


def _own_head_columns(y, t):
    lane_head = lax.broadcasted_iota(jnp.int32, (1, ATT_DIM), 1) // HEAD_DIM
    out = jnp.zeros((t, ATT_DIM), F32)
    for h in range(ATT_HEADS):
        out = jnp.where(lane_head == h, y[h * t:(h + 1) * t], out)
    return out


def _moba_sample_kernel(pt_ref, q_ref, kn_ref, vn_ref, *refs, n_past_blocks):
    del pt_ref
    k_pages = refs[:PAGES_PER_STEP]
    v_pages = refs[PAGES_PER_STEP:2 * PAGES_PER_STEP]
    o_ref, m_ref, l_ref, g_ref, acc_ref = refs[2 * PAGES_PER_STEP:]
    c = pl.program_id(1)
    t = q_ref.shape[0]
    rows = ATT_HEADS * t
    pages_per_block = MOBA_BLOCK // PAGE_SIZE
    blocks_per_step = PAGES_PER_STEP // pages_per_block
    past_len = n_past_blocks * MOBA_BLOCK

    row_head = lax.broadcasted_iota(jnp.int32, (rows, 1), 0) // t
    slope = jnp.zeros((rows, 1), F32)
    for h in range(ATT_HEADS):
        slope = jnp.where(row_head == h, _alibi_slope(h), slope)
    q_all = _head_rows(q_ref[...] * HEAD_DIM ** -0.5).astype(BF16)
    kcol = lax.broadcasted_iota(jnp.int32, (1, MOBA_BLOCK), 1)

    raw = [_dot(q_all, k_ref[...].astype(BF16)) for k_ref in k_pages]
    for b in range(blocks_per_step):
        j = c * blocks_per_step + b
        pages = range(b * pages_per_block, (b + 1) * pages_per_block)
        s = jnp.concatenate([raw[i] for i in pages], axis=1)
        g_ref[j] = jnp.sum(s, axis=1, keepdims=True)
        s = s + slope * (kcol + (j * MOBA_BLOCK - past_len)).astype(F32)
        m = jnp.max(s, axis=1, keepdims=True)
        e = jnp.exp(s - m)
        m_ref[j] = m
        l_ref[j] = jnp.sum(e, axis=1, keepdims=True)
        e = e.astype(BF16)
        acc = None
        for n, i in enumerate(pages):
            part = _dot(e[:, n * PAGE_SIZE:(n + 1) * PAGE_SIZE], v_pages[i][...].astype(BF16), NT_DIMS)
            acc = part if acc is None else acc + part
        acc_ref[j] = acc

    @pl.when(c == pl.num_programs(1) - 1)
    def _():
        pad = jnp.zeros((PAGE_SIZE - t, ATT_DIM), F32)
        k_new = jnp.concatenate([kn_ref[...], pad], axis=0).astype(BF16)
        v_new = jnp.concatenate([vn_ref[...], pad], axis=0).astype(BF16)
        ncol = lax.broadcasted_iota(jnp.int32, (1, PAGE_SIZE), 1)
        qrow = lax.rem(lax.broadcasted_iota(jnp.int32, (rows, 1), 0), t)
        s = _dot(q_all, k_new, NT_DIMS) + slope * ncol.astype(F32)
        s = jnp.where(ncol <= qrow, s, -jnp.inf)
        m = jnp.max(s, axis=1, keepdims=True)
        e = jnp.exp(s - m)
        blk = lax.broadcasted_iota(jnp.int32, (1, LANES), 1)
        gate = jnp.zeros((rows, LANES), F32)
        for j in range(n_past_blocks):
            gate = jnp.where(blk == j, g_ref[j], gate)
        sel = [_topk_select(gate, j, n_past_blocks, blk) for j in range(n_past_blocks)]
        m_all = m
        for j in range(n_past_blocks):
            m_all = jnp.maximum(m_all, jnp.where(sel[j], m_ref[j], -jnp.inf))
        w = jnp.exp(m - m_all)
        den = w * jnp.sum(e, axis=1, keepdims=True)
        num = w * _dot(e.astype(BF16), v_new)
        for j in range(n_past_blocks):
            w = jnp.where(sel[j], jnp.exp(m_ref[j] - m_all), 0.0)
            den = den + w * l_ref[j]
            num = num + w * acc_ref[j]
        o_ref[...] = _own_head_columns(num / den, t)


def _moba_sample(q, k_new, v_new, cache_kt, cache_vt, page_ids):
    n, t, c = q.shape
    pages_per_seq = page_ids.shape[0] // n
    past_len = pages_per_seq * PAGE_SIZE
    n_past_blocks = past_len // MOBA_BLOCK
    assert past_len % MOBA_BLOCK == 0 and t <= PAGE_SIZE and t % SUBLANES == 0 and c == ATT_DIM
    assert pages_per_seq % PAGES_PER_STEP == 0 and MOBA_TOPK <= n_past_blocks <= LANES
    new = pl.BlockSpec((None, t, c), lambda i, j, pt: (i, 0, 0))

    def page(slot):
        return pl.BlockSpec((None, c, PAGE_SIZE),
                            lambda i, j, pt: (pt[i * pages_per_seq + j * PAGES_PER_STEP + slot], 0, 0))

    pages = [page(s) for s in range(PAGES_PER_STEP)]
    rows = ATT_HEADS * t
    stat = pltpu.VMEM((n_past_blocks, rows, 1), F32)
    return pl.pallas_call(
        functools.partial(_moba_sample_kernel, n_past_blocks=n_past_blocks),
        grid_spec=pltpu.PrefetchScalarGridSpec(
            num_scalar_prefetch=1,
            grid=(n, pages_per_seq // PAGES_PER_STEP),
            in_specs=[new, new, new] + pages + pages,
            out_specs=new,
            scratch_shapes=[stat, stat, stat, pltpu.VMEM((n_past_blocks, rows, c), F32)],
        ),
        out_shape=jax.ShapeDtypeStruct((n, t, c), F32),
        compiler_params=_params("parallel", "arbitrary"),
        name="moba_sample",
    )(page_ids, q, k_new, v_new, *([cache_kt] * PAGES_PER_STEP), *([cache_vt] * PAGES_PER_STEP))


def _retention_constants(c):
    lg = np.log1p(-np.exp2(-5.0 - np.arange(RET_HEADS, dtype=np.float64)))
    idx = np.arange(c, dtype=np.float64)
    diff = idx[:, None] - idx[None, :]
    d_in = np.where(diff >= 0, np.exp(lg[:, None, None] * np.maximum(diff, 0.0)), 0.0)
    q_dec = np.exp(lg[:, None] * (idx + 1.0))
    k_dec = np.exp(lg[:, None] * (c - 1.0 - idx))
    c_dec = np.exp(lg * c)
    per_lane = lambda a: np.repeat(a.reshape(RET_PAIRS, HEADS_PER_TILE, -1).transpose(0, 2, 1), HEAD_DIM, axis=2)
    block = np.kron(np.eye(HEADS_PER_TILE), np.ones((HEAD_DIM, HEAD_DIM)))
    state_dec = block[None] * np.repeat(c_dec.reshape(RET_PAIRS, HEADS_PER_TILE), HEAD_DIM, axis=1)[:, :, None]
    as_f32 = lambda a: jnp.asarray(a, F32)
    return (as_f32(d_in.reshape(RET_PAIRS, HEADS_PER_TILE * c, c)), as_f32(per_lane(q_dec)), as_f32(per_lane(k_dec)),
            as_f32(state_dec), as_f32(block))


def _retention_kernel(*refs, has_state):
    if has_state:
        q_ref, k_ref, v_ref, g_ref, s0_ref, din_ref, qd_ref, kd_ref, sd_ref, blk_ref, y_ref, sout_ref, s_ref = refs
    else:
        q_ref, k_ref, v_ref, g_ref, din_ref, qd_ref, kd_ref, sd_ref, blk_ref, y_ref, sout_ref, s_ref = refs
    ci = pl.program_id(1)
    c = q_ref.shape[0]

    @pl.when(ci == 0)
    def _():
        s_ref[...] = s0_ref[...] if has_state else jnp.zeros(s_ref.shape, F32)

    lane = lax.broadcasted_iota(jnp.int32, (c, LANES), 1)
    for p in range(RET_PAIRS):
        cols = slice(p * LANES, (p + 1) * LANES)
        q = q_ref[:, cols]
        k = k_ref[:, cols] * HEAD_DIM ** -0.5
        v = v_ref[:, cols].astype(BF16)
        state = s_ref[p]
        att = _dot(_stack_pair(q, lane).astype(BF16), k.astype(BF16), NT_DIMS) * din_ref[p]
        o = _unstack_pair(_dot(att.astype(BF16), v), lane)
        o = o + _dot((q * qd_ref[p]).astype(BF16), state.astype(BF16))
        s_ref[p] = state * sd_ref[p] + _dot((k * kd_ref[p]).astype(BF16), v, TN_DIMS) * blk_ref[...]
        sq = o * o
        even = jnp.sum(jnp.where(lane < HEAD_DIM, sq, 0.0), axis=1, keepdims=True)
        odd = jnp.sum(jnp.where(lane >= HEAD_DIM, sq, 0.0), axis=1, keepdims=True)
        ms = jnp.where(lane < HEAD_DIM, even, odd) * (1.0 / HEAD_DIM)
        y_ref[:, cols] = _silu(g_ref[:, cols]) * (o * lax.rsqrt(ms + EPS))

    @pl.when(ci == pl.num_programs(1) - 1)
    def _():
        sout_ref[...] = s_ref[...]


def _retention(q, k, v, g, state):
    n, t, d = q.shape
    c = math.gcd(t, RET_CHUNK)
    consts = _retention_constants(c)
    chunk = pl.BlockSpec((None, c, d), lambda i, j: (i, j, 0))
    st = pl.BlockSpec((None, RET_PAIRS, LANES, LANES), lambda i, j: (i, 0, 0, 0))
    in_specs, args = [chunk] * 4, [q, k, v, g]
    if state is not None:
        in_specs.append(st)
        args.append(state)
    in_specs += [_resident(a) for a in consts]
    args += list(consts)
    return pl.pallas_call(
        functools.partial(_retention_kernel, has_state=state is not None),
        grid=(n, t // c),
        in_specs=in_specs,
        out_specs=[chunk, st],
        out_shape=[jax.ShapeDtypeStruct((n, t, d), F32), jax.ShapeDtypeStruct((n, RET_PAIRS, LANES, LANES), F32)],
        scratch_shapes=[pltpu.VMEM((RET_PAIRS, LANES, LANES), F32)],
        compiler_params=_params("parallel", "arbitrary"),
        name="retention",
    )(*args)


def _state_to_pairs(s):
    n = s.shape[0]
    s = s.astype(F32).reshape(n, RET_PAIRS, HEADS_PER_TILE, HEAD_DIM, HEAD_DIM)
    eye = jnp.eye(HEADS_PER_TILE, dtype=F32)
    return jnp.einsum("npaij,ab->npaibj", s, eye).reshape(n, RET_PAIRS, LANES, LANES)


def _state_from_pairs(s):
    n = s.shape[0]
    s = s.reshape(n, RET_PAIRS, HEADS_PER_TILE, HEAD_DIM, HEADS_PER_TILE, HEAD_DIM)
    return jnp.stack([s[:, :, a, :, a, :] for a in range(HEADS_PER_TILE)], axis=2).reshape(n, RET_HEADS, HEAD_DIM, HEAD_DIM)


def _layer(x, n, pool_prefix, ret_state, moba, w, final_norm, final):
    t = x.shape[0] // n
    x1, u, qa, ka, va, qr, kr, vr, gr = _dense_in(x, w["g1"], w["f1i"], w["f1o"], w["gm"], w["win"])
    seq = lambda a: a.reshape(n, t, a.shape[-1])
    u = seq(u)
    pos0 = 0 if pool_prefix is None else moba["past_len"]
    if pool_prefix is None:
        y_pool, pool_new = _pool(u, None, w["pool_w"], w["pool_scale"], pos0), u[:, t - POOL_PAST:]
    else:
        halo = jnp.pad(pool_prefix.astype(F32), ((0, 0), (POOL_HALO - POOL_PAST, 0), (0, 0)))
        y_pool = _pool(u, halo, w["pool_w"], w["pool_scale"], pos0)
        pool_new = jnp.concatenate([pool_prefix.astype(F32), u], axis=1)[:, -POOL_PAST:]
    if moba is None:
        y_att = _moba_prompt(seq(qa), seq(ka), seq(va))
    else:
        y_att = _moba_sample(seq(qa), seq(ka), seq(va), moba["cache_k"], moba["cache_v"], moba["page_ids"])
    y_ret, ret_new = _retention(seq(qr), seq(kr), seq(vr), seq(gr), ret_state)
    flat = lambda a: a.reshape(n * t, a.shape[-1])
    x2 = _dense_out(x1, flat(y_pool), flat(y_att), flat(y_ret), w["wo"], w["g2"], w["f2i"], w["f2o"], final_norm, final)
    heads = lambda a: a.reshape(n, t, ATT_HEADS, HEAD_DIM)
    return x2, heads(ka), heads(va), pool_new, _state_from_pairs(ret_new)


def kernel(x_prompt, x_sample, cache_k, cache_v, state_pool, state_ret, page_table, norm_ffn1, ffn1_w_in, ffn1_w_out,
           norm_mix, w_in, pool_w, pool_scale, w_out, norm_ffn2, ffn2_w_in, ffn2_w_out, norm_final):
    depth = w_in.shape[0]
    n_p, t_p, d = x_prompt.shape
    n_s, t_s, _ = x_sample.shape
    n_pool = cache_k.shape[1]
    pages_per_seq = page_table.shape[1]
    assert cache_k.shape[2] == PAGE_SIZE and d % LANES == 0
    pages_t = lambda a: jnp.transpose(a, (0, 1, 3, 4, 2)).reshape(depth * n_pool, ATT_DIM, PAGE_SIZE)
    cache_k, cache_v = pages_t(cache_k), pages_t(cache_v)
    row = lambda a: a.reshape(1, -1).astype(F32)
    final_norm = row(norm_final)
    hp, hs = x_prompt.reshape(n_p * t_p, d), x_sample.reshape(n_s * t_s, d)
    outs = [[] for _ in range(8)]
    for l in range(depth):
        w = dict(g1=row(norm_ffn1[l]), f1i=ffn1_w_in[l].astype(BF16), f1o=ffn1_w_out[l].astype(BF16),
                 gm=row(norm_mix[l]), win=w_in[l].astype(BF16),
                 pool_w=jax.scipy.linalg.block_diag(*pool_w[l]).astype(BF16), pool_scale=row(pool_scale[l]),
                 wo=w_out[l].astype(BF16), g2=row(norm_ffn2[l]), f2i=ffn2_w_in[l].astype(BF16),
                 f2o=ffn2_w_out[l].astype(BF16))
        final = l == depth - 1
        hp, kp, vp, pp, rp = _layer(hp, n_p, None, None, None, w, final_norm, final)
        moba = dict(cache_k=cache_k, cache_v=cache_v, past_len=pages_per_seq * PAGE_SIZE,
                    page_ids=(page_table.astype(jnp.int32) + l * n_pool).reshape(-1))
        hs, ks, vs, ps, rs = _layer(hs, n_s, state_pool[l], _state_to_pairs(state_ret[l]), moba, w, final_norm, final)
        for lst, a in zip(outs, (kp, vp, pp, rp, ks, vs, ps, rs)):
            lst.append(a)
    return (hp.reshape(n_p, t_p, d), hs.reshape(n_s, t_s, d)) + tuple(jnp.stack(lst) for lst in outs)
```

```python
import functools
import math

import numpy as np
import jax
import jax.numpy as jnp
from jax import lax
from jax.experimental import pallas as pl
from jax.experimental.pallas import tpu as pltpu

F32 = jnp.float32
BF16 = jnp.bfloat16

HEAD_DIM = 64
POOL_WINDOWS = (2, 4, 8, 16)
POOL_GRP = HEAD_DIM
POOL_DIM = len(POOL_WINDOWS) * POOL_GRP
POOL_PAST = max(POOL_WINDOWS) - 1
ATT_HEADS = 6
ATT_DIM = ATT_HEADS * HEAD_DIM
RET_HEADS = 6
RET_DIM = RET_HEADS * HEAD_DIM
MOBA_BLOCK = 256
MOBA_TOPK = 3
PAGE_SIZE = 128
RET_CHUNK = 128
EPS = 1e-6

LANES = 128
SUBLANES = 8
HEADS_PER_TILE = LANES // HEAD_DIM
ATT_PAIRS = ATT_DIM // LANES
RET_PAIRS = RET_DIM // LANES
VMEM_LIMIT = 56 * 1024 * 1024

TOKEN_TILE = 512
FF_CHUNK = 256
POOL_TILE = 512
POOL_HALO = 16
LOG2_E = math.log2(math.e)
PAGES_PER_STEP = 8

NT_DIMS = (((1,), (1,)), ((), ()))
TN_DIMS = (((0,), (0,)), ((), ()))


def _params(*semantics):
    return pltpu.CompilerParams(dimension_semantics=semantics, vmem_limit_bytes=VMEM_LIMIT)


def _resident(a):
    return pl.BlockSpec(a.shape, lambda *_: (0,) * a.ndim, pipeline_mode=pl.Buffered(1))


def _rmsnorm(x, g):
    return x * lax.rsqrt(jnp.mean(x * x, axis=-1, keepdims=True) + EPS) * g


def _silu(x):
    return x * (1.0 / (1.0 + jnp.exp(-x)))


def _dot(a, b, dims=None, **kw):
    if dims is None:
        return jnp.dot(a, b, preferred_element_type=F32, **kw)
    return lax.dot_general(a, b, dims, preferred_element_type=F32, **kw)


def _stack_pair(x, lane):
    return jnp.concatenate([jnp.where(lane < HEAD_DIM, x, 0.0), jnp.where(lane >= HEAD_DIM, x, 0.0)], axis=0)


def _unstack_pair(y, lane):
    r = y.shape[0] // 2
    return jnp.where(lane < HEAD_DIM, y[:r], y[r:])


def _alibi_slope(h):
    return 2.0 ** (-8.0 * (h + 1) / ATT_HEADS)


def _swiglu_half_step(x, g_ref, wi_ref, wo_ref):
    d_ff = wo_ref.shape[0]
    h = _rmsnorm(x, g_ref[...]).astype(BF16)
    acc = jnp.zeros(x.shape, F32)
    for c in range(d_ff // FF_CHUNK):
        lo = c * FF_CHUNK
        gate = _dot(h, wi_ref[:, lo:lo + FF_CHUNK])
        up = _dot(h, wi_ref[:, d_ff + lo:d_ff + lo + FF_CHUNK])
        acc = acc + _dot((_silu(gate) * up).astype(BF16), wo_ref[lo:lo + FF_CHUNK, :])
    return x + 0.5 * acc


def _dense_in_kernel(x_ref, g1_ref, f1i_ref, f1o_ref, gm_ref, win_ref, *refs, col_offsets, feature_major):
    if feature_major:
        wt_ref, refs = refs[0], refs[1:]
    x1_ref, z_refs, zt_refs = refs[0], refs[1:1 + len(col_offsets)], refs[1 + len(col_offsets):]
    x1 = _swiglu_half_step(x_ref[...], g1_ref, f1i_ref, f1o_ref)
    x1_ref[...] = x1
    h = _rmsnorm(x1, gm_ref[...]).astype(BF16)
    for z_ref, off in zip(z_refs, col_offsets):
        z_ref[...] = _dot(h, win_ref[:, off:off + z_ref.shape[-1]]).astype(z_ref.dtype)
    if feature_major:
        zt = _dot(wt_ref[...], h, NT_DIMS)
        off = 0
        for zt_ref in zt_refs:
            zt_ref[...] = zt[off:off + zt_ref.shape[0]]
            off += zt_ref.shape[0]


Z_COLUMNS = dict(zip(("u", "qa", "ka", "va", "qr", "kr", "vr", "gr"),
                     ((0, POOL_DIM), (POOL_DIM, ATT_DIM), (POOL_DIM + ATT_DIM, ATT_DIM), (POOL_DIM + 2 * ATT_DIM, ATT_DIM))
                     + tuple((POOL_DIM + 3 * ATT_DIM + i * RET_DIM, RET_DIM) for i in range(4))))


def _dense_in(x, g1, f1i, f1o, gm, win, seq_len=None):
    t, d = x.shape
    tm = min(TOKEN_TILE, t)
    assert t % tm == 0 and win.shape[1] == sum(w for _, w in Z_COLUMNS.values()) and f1o.shape[0] % FF_CHUNK == 0
    rows = lambda w: pl.BlockSpec((tm, w), lambda i: (i, 0))
    names = [k for k in Z_COLUMNS if not (seq_len and k == "va")]
    dtypes = [BF16 if (seq_len and k == "ka") else F32 for k in names]
    in_specs = [rows(d)] + [_resident(a) for a in (g1, f1i, f1o, gm, win)]
    args = [x, g1, f1i, f1o, gm, win]
    out_specs = [rows(d)] + [rows(Z_COLUMNS[k][1]) for k in names]
    out_shape = [jax.ShapeDtypeStruct((t, d), F32)] + [jax.ShapeDtypeStruct((t, Z_COLUMNS[k][1]), dt)
                                                      for k, dt in zip(names, dtypes)]
    if seq_len:
        assert seq_len % tm == 0 and Z_COLUMNS["va"][0] == sum(Z_COLUMNS["ka"])
        tiles = seq_len // tm
        kv_t = win[:, Z_COLUMNS["ka"][0]:sum(Z_COLUMNS["va"])].T
        in_specs.append(_resident(kv_t))
        args.append(kv_t)
        out_specs += [pl.BlockSpec((None, ATT_DIM, tm), lambda i: (i // tiles, 0, i % tiles))] * 2
        out_shape += [jax.ShapeDtypeStruct((t // seq_len, ATT_DIM, seq_len), F32)] * 2
        names += ["kt", "vt"]
    outs = pl.pallas_call(
        functools.partial(_dense_in_kernel, col_offsets=tuple(Z_COLUMNS[k][0] for k in names if k in Z_COLUMNS),
                          feature_major=bool(seq_len)),
        grid=(t // tm,),
        in_specs=in_specs,
        out_specs=out_specs,
        out_shape=out_shape,
        compiler_params=_params("parallel"),
        name="dense_in",
    )(*args)
    return dict(zip(["x1"] + names, outs))


def _dense_out_kernel(x_ref, yp_ref, ya_ref, yr_ref, wo_ref, g2_ref, f2i_ref, f2o_ref, gf_ref, o_ref, *, final):
    off, mix = 0, None
    for y_ref in (yp_ref, ya_ref, yr_ref):
        w = y_ref.shape[-1]
        part = _dot(y_ref[...].astype(BF16), wo_ref[off:off + w, :])
        mix = part if mix is None else mix + part
        off += w
    x = _swiglu_half_step(x_ref[...] + mix, g2_ref, f2i_ref, f2o_ref)
    if final:
        x = _rmsnorm(x, gf_ref[...])
    o_ref[...] = x


def _dense_out(x, yp, ya, yr, wo, g2, f2i, f2o, gf, final):
    t, d = x.shape
    tm = min(TOKEN_TILE, t)
    assert t % tm == 0
    rows = lambda w: pl.BlockSpec((tm, w), lambda i: (i, 0))
    return pl.pallas_call(
        functools.partial(_dense_out_kernel, final=final),
        grid=(t // tm,),
        in_specs=[rows(d), rows(yp.shape[1]), rows(ya.shape[1]), rows(yr.shape[1])]
        + [_resident(a) for a in (wo, g2, f2i, f2o, gf)],
        out_specs=rows(d),
        out_shape=jax.ShapeDtypeStruct((t, d), F32),
        compiler_params=_params("parallel"),
        name="dense_out",
    )(x, yp, ya, yr, wo, g2, f2i, f2o, gf)


def _pool_kernel(*refs, tt, pos0, has_prefix):
    if has_prefix:
        u_ref, pre_ref, w_ref, scale_ref, y_ref, carry_ref = refs
    else:
        u_ref, w_ref, scale_ref, y_ref, carry_ref = refs
    t = pl.program_id(1)

    @pl.when(t == 0)
    def _():
        carry_ref[...] = pre_ref[...] if has_prefix else jnp.zeros(carry_ref.shape, F32)

    u = u_ref[...]
    ext = jnp.concatenate([carry_ref[...], u], axis=0)
    sums, s = [], ext
    for w in POOL_WINDOWS:
        s = s + pltpu.roll(s, w // 2, axis=0)
        sums.append(s[POOL_HALO:])
    pos = pos0 + t * tt + lax.broadcasted_iota(jnp.int32, (tt, 1), 0)
    lane = lax.broadcasted_iota(jnp.int32, (1, POOL_DIM), 1)
    pooled = None
    for g, w in reversed(list(enumerate(POOL_WINDOWS))):
        mean = sums[g] * (1.0 / jnp.minimum(w, pos + 1).astype(F32))
        pooled = mean if pooled is None else jnp.where(lane < (g + 1) * POOL_GRP, mean, pooled)
    d = (pooled - u).astype(BF16)
    y_ref[...] = _dot(d, w_ref[...]) * scale_ref[...]
    carry_ref[...] = ext[tt:]


def _pool(u, prefix, w_blockdiag, scale, pos0):
    n, t, c = u.shape
    tt = min(POOL_TILE, t)
    assert t % tt == 0 and tt % SUBLANES == 0 and POOL_WINDOWS == (2, 4, 8, 16)
    seq = pl.BlockSpec((None, tt, c), lambda i, j: (i, j, 0))
    in_specs, args = [seq], [u]
    if prefix is not None:
        in_specs.append(pl.BlockSpec((None, POOL_HALO, c), lambda i, j: (i, 0, 0)))
        args.append(prefix)
    in_specs += [_resident(w_blockdiag), _resident(scale)]
    args += [w_blockdiag, scale]
    return pl.pallas_call(
        functools.partial(_pool_kernel, tt=tt, pos0=pos0, has_prefix=prefix is not None),
        grid=(n, t // tt),
        in_specs=in_specs,
        out_specs=seq,
        out_shape=jax.ShapeDtypeStruct((n, t, c), F32),
        scratch_shapes=[pltpu.VMEM((POOL_HALO, c), F32)],
        compiler_params=_params("parallel", "arbitrary"),
        name="pool_mixer",
    )(*args)


def _moba_prompt_kernel(q_ref, k_ref, vt_in_ref, o_ref, kmean_ref, vt_ref, bias_ref, t_ref, *, n_blocks):
    qi = pl.program_id(1)
    tq = q_ref.shape[0]
    rows = HEADS_PER_TILE * tq
    dim = lax.broadcasted_iota(jnp.int32, (LANES, 1), 0)
    key = lax.broadcasted_iota(jnp.int32, (MOBA_BLOCK, rows), 0)
    col = lax.broadcasted_iota(jnp.int32, (1, rows), 1)
    blk = lax.broadcasted_iota(jnp.int32, (kmean_ref.shape[0], 1), 0)
    slopes = [jnp.where(col < tq, _alibi_slope(2 * p) * LOG2_E, _alibi_slope(2 * p + 1) * LOG2_E).astype(F32)
              for p in range(ATT_PAIRS)]

    @pl.when(qi == 0)
    def _():
        kmean_ref[...] = jnp.zeros(kmean_ref.shape, F32)
        for j in range(n_blocks):
            keys = slice(j * MOBA_BLOCK, (j + 1) * MOBA_BLOCK)
            kmean_ref[j:j + 1, :] = jnp.mean(k_ref[keys, :].astype(F32), axis=0, keepdims=True)
            vt_ref[j] = vt_in_ref[:, keys].astype(BF16)
        for p in range(ATT_PAIRS):
            bias_ref[p] = slopes[p] * key.astype(F32)

    def tile(own):
        causal = key <= lax.rem(col, tq)
        for p in range(ATT_PAIRS):
            cols = slice(p * LANES, (p + 1) * LANES)
            q_t = (q_ref[:, cols] * (HEAD_DIM ** -0.5 * LOG2_E)).T
            qs_t = jnp.concatenate([jnp.where(dim < HEAD_DIM, q_t, 0.0), jnp.where(dim >= HEAD_DIM, q_t, 0.0)], axis=1)
            qs_b = qs_t.astype(BF16)
            sel = None
            if own > MOBA_TOPK:
                gate = _dot(kmean_ref[:, cols], qs_t, precision=lax.Precision.HIGHEST)
                sel = []
                for j in range(own):
                    gj = gate[j:j + 1, :]
                    ahead = jnp.where(blk < j, jnp.where(gate >= gj, 1.0, 0.0), jnp.where(gate > gj, 1.0, 0.0))
                    sel.append(jnp.sum(jnp.where(blk < own, ahead, 0.0), axis=0, keepdims=True) < MOBA_TOPK)
            block_bias = lambda j: slopes[p] * float((j - own) * MOBA_BLOCK)
            maxima = []
            for j in range(own + 1):
                t = _dot(k_ref[j * MOBA_BLOCK:(j + 1) * MOBA_BLOCK, cols], qs_b) + bias_ref[p]
                if j == own:
                    t = jnp.where(causal, t, -jnp.inf)
                t_ref[p, j] = t
                m = jnp.max(t, axis=0, keepdims=True) + block_bias(j)
                maxima.append(m if sel is None or j == own else jnp.where(sel[j], m, -jnp.inf))
            m_all = functools.reduce(jnp.maximum, maxima)
            acc, den = None, None
            for j in range(own + 1):
                shift = m_all - block_bias(j)
                if sel is not None and j < own:
                    shift = jnp.where(sel[j], shift, jnp.inf)
                e = jnp.exp2(t_ref[p, j] - shift)
                part_den = jnp.sum(e, axis=0, keepdims=True)
                part = _dot(vt_ref[j, cols, :], e.astype(BF16))
                acc, den = (part, part_den) if acc is None else (acc + part, den + part_den)
            out_t = acc / den
            o_ref[:, cols] = jnp.where(dim < HEAD_DIM, out_t[:, :tq], out_t[:, tq:]).T

    for own in range(n_blocks):
        pl.when(qi == own)(functools.partial(tile, own))


def _moba_prompt(q, k, vt):
    n, t, c = q.shape
    tq = MOBA_BLOCK
    n_blocks = t // MOBA_BLOCK
    gate_rows = -(-n_blocks // SUBLANES) * SUBLANES
    rows = HEADS_PER_TILE * tq
    assert t % MOBA_BLOCK == 0 and c == ATT_DIM and vt.shape == (n, c, t)
    tile = pl.BlockSpec((None, tq, c), lambda i, j: (i, j, 0))
    return pl.pallas_call(
        functools.partial(_moba_prompt_kernel, n_blocks=n_blocks),
        grid=(n, n_blocks),
        in_specs=[tile, pl.BlockSpec((None, t, c), lambda i, j: (i, 0, 0)), pl.BlockSpec((None, c, t), lambda i, j: (i, 0, 0))],
        out_specs=tile,
        out_shape=jax.ShapeDtypeStruct((n, t, c), F32),
        scratch_shapes=[pltpu.VMEM((gate_rows, c), F32),
                        pltpu.VMEM((n_blocks, c, MOBA_BLOCK), BF16),
                        pltpu.VMEM((ATT_PAIRS, MOBA_BLOCK, rows), F32),
                        pltpu.VMEM((ATT_PAIRS, n_blocks, MOBA_BLOCK, rows), F32)],
        compiler_params=_params("parallel", "arbitrary"),
        name="moba_prompt",
    )(q, k, vt)


def _head_rows(x):
    t = x.shape[0]
    lane_head = lax.broadcasted_iota(jnp.int32, (1, ATT_DIM), 1) // HEAD_DIM
    row_head = lax.broadcasted_iota(jnp.int32, (ATT_HEADS * t, 1), 0) // t
    return jnp.where(lane_head == row_head, jnp.concatenate([x] * ATT_HEADS, axis=0), 0.0)


def _own_head_columns(y, t):
    lane_head = lax.broadcasted_iota(jnp.int32, (1, ATT_DIM), 1) // HEAD_DIM
    out = jnp.zeros((t, ATT_DIM), F32)
    for h in range(ATT_HEADS):
        out = jnp.where(lane_head == h, y[h * t:(h + 1) * t], out)
    return out


def _moba_sample_kernel(pt_ref, q_ref, kn_ref, vn_ref, *refs, n_past_blocks):
    del pt_ref
    k_pages = refs[:PAGES_PER_STEP]
    v_pages = refs[PAGES_PER_STEP:2 * PAGES_PER_STEP]
    o_ref, m_ref, l_ref, g_ref, acc_ref = refs[2 * PAGES_PER_STEP:]
    c = pl.program_id(1)
    t = q_ref.shape[0]
    rows = ATT_HEADS * t
    pages_per_block = MOBA_BLOCK // PAGE_SIZE
    blocks_per_step = PAGES_PER_STEP // pages_per_block
    past_len = n_past_blocks * MOBA_BLOCK

    row_head = lax.broadcasted_iota(jnp.int32, (rows, 1), 0) // t
    slope = jnp.zeros((rows, 1), F32)
    for h in range(ATT_HEADS):
        slope = jnp.where(row_head == h, _alibi_slope(h), slope)
    q_all = _head_rows(q_ref[...] * HEAD_DIM ** -0.5).astype(BF16)
    kcol = lax.broadcasted_iota(jnp.int32, (1, MOBA_BLOCK), 1)
    blk = lax.broadcasted_iota(jnp.int32, (1, LANES), 1)

    @pl.when(c == 0)
    def _():
        g_ref[...] = jnp.zeros(g_ref.shape, F32)
        m_ref[...] = jnp.full(m_ref.shape, -jnp.inf, F32)
        l_ref[...] = jnp.zeros(l_ref.shape, F32)

    raw = [_dot(q_all, k_ref[...].astype(BF16)) for k_ref in k_pages]
    for b in range(blocks_per_step):
        j = c * blocks_per_step + b
        pages = range(b * pages_per_block, (b + 1) * pages_per_block)
        s = jnp.concatenate([raw[i] for i in pages], axis=1)
        g_ref[...] = jnp.where(blk == j, jnp.sum(s, axis=1, keepdims=True), g_ref[...])
        s = s + slope * (kcol + (j * MOBA_BLOCK - past_len)).astype(F32)
        m = jnp.max(s, axis=1, keepdims=True)
        e = jnp.exp(s - m)
        m_ref[...] = jnp.where(blk == j, m, m_ref[...])
        l_ref[...] = jnp.where(blk == j, jnp.sum(e, axis=1, keepdims=True), l_ref[...])
        e = e.astype(BF16)
        acc = None
        for n, i in enumerate(pages):
            part = _dot(e[:, n * PAGE_SIZE:(n + 1) * PAGE_SIZE], v_pages[i][...].astype(BF16), NT_DIMS)
            acc = part if acc is None else acc + part
        acc_ref[j] = acc

    @pl.when(c == pl.num_programs(1) - 1)
    def _():
        pad = jnp.zeros((PAGE_SIZE - t, ATT_DIM), F32)
        k_new = jnp.concatenate([kn_ref[...], pad], axis=0).astype(BF16)
        v_new = jnp.concatenate([vn_ref[...], pad], axis=0).astype(BF16)
        ncol = lax.broadcasted_iota(jnp.int32, (1, PAGE_SIZE), 1)
        qrow = lax.rem(lax.broadcasted_iota(jnp.int32, (rows, 1), 0), t)
        s = _dot(q_all, k_new, NT_DIMS) + slope * ncol.astype(F32)
        s = jnp.where(ncol <= qrow, s, -jnp.inf)
        m = jnp.max(s, axis=1, keepdims=True)
        e = jnp.exp(s - m)
        blk_f = blk.astype(F32)
        gate = jnp.where(blk < n_past_blocks, g_ref[...], -jnp.inf)
        picked = jnp.zeros(gate.shape, F32)
        for _ in range(MOBA_TOPK):
            top = jnp.max(gate, axis=1, keepdims=True)
            first = jnp.min(jnp.where(gate == top, blk_f, float(LANES)), axis=1, keepdims=True)
            picked = jnp.where(blk_f == first, 1.0, picked)
            gate = jnp.where(blk_f == first, -jnp.inf, gate)
        m_all = jnp.maximum(m, jnp.max(jnp.where(picked > 0.0, m_ref[...], -jnp.inf), axis=1, keepdims=True))
        w = jnp.where(picked > 0.0, jnp.exp(m_ref[...] - m_all), 0.0)
        w_new = jnp.exp(m - m_all)
        den = w_new * jnp.sum(e, axis=1, keepdims=True) + jnp.sum(w * l_ref[...], axis=1, keepdims=True)
        parts = [w_new * _dot(e.astype(BF16), v_new)] + [w[:, j:j + 1] * acc_ref[j] for j in range(n_past_blocks)]
        while len(parts) > 1:
            parts = [a + b for a, b in zip(parts[::2], parts[1::2])] + parts[len(parts) - len(parts) % 2:]
        o_ref[...] = _own_head_columns(parts[0] / den, t)


def _moba_sample(q, k_new, v_new, cache_kt, cache_vt, page_ids):
    n, t, c = q.shape
    pages_per_seq = page_ids.shape[0] // n
    past_len = pages_per_seq * PAGE_SIZE
    n_past_blocks = past_len // MOBA_BLOCK
    assert past_len % MOBA_BLOCK == 0 and t <= PAGE_SIZE and t % SUBLANES == 0 and c == ATT_DIM
    assert pages_per_seq % PAGES_PER_STEP == 0 and MOBA_TOPK <= n_past_blocks <= LANES
    new = pl.BlockSpec((None, t, c), lambda i, j, pt: (i, 0, 0))

    def page(slot):
        return pl.BlockSpec((None, c, PAGE_SIZE),
                            lambda i, j, pt: (pt[i * pages_per_seq + j * PAGES_PER_STEP + slot], 0, 0))

    pages = [page(s) for s in range(PAGES_PER_STEP)]
    rows = ATT_HEADS * t
    stat = pltpu.VMEM((rows, LANES), F32)
    return pl.pallas_call(
        functools.partial(_moba_sample_kernel, n_past_blocks=n_past_blocks),
        grid_spec=pltpu.PrefetchScalarGridSpec(
            num_scalar_prefetch=1,
            grid=(n, pages_per_seq // PAGES_PER_STEP),
            in_specs=[new, new, new] + pages + pages,
            out_specs=new,
            scratch_shapes=[stat, stat, stat, pltpu.VMEM((n_past_blocks, rows, c), F32)],
        ),
        out_shape=jax.ShapeDtypeStruct((n, t, c), F32),
        compiler_params=_params("parallel", "arbitrary"),
        name="moba_sample",
    )(page_ids, q, k_new, v_new, *([cache_kt] * PAGES_PER_STEP), *([cache_vt] * PAGES_PER_STEP))


def _retention_constants(c):
    lg = np.log1p(-np.exp2(-5.0 - np.arange(RET_HEADS, dtype=np.float64)))
    idx = np.arange(c, dtype=np.float64)
    diff = idx[:, None] - idx[None, :]
    d_in = np.where(diff >= 0, np.exp(lg[:, None, None] * np.maximum(diff, 0.0)), 0.0)
    q_dec = np.exp(lg[:, None] * (idx + 1.0))
    k_dec = np.exp(lg[:, None] * (c - 1.0 - idx))
    c_dec = np.exp(lg * c)
    per_lane = lambda a: np.repeat(a.reshape(RET_PAIRS, HEADS_PER_TILE, -1).transpose(0, 2, 1), HEAD_DIM, axis=2)
    block = np.kron(np.eye(HEADS_PER_TILE), np.ones((HEAD_DIM, HEAD_DIM)))
    state_dec = block[None] * np.repeat(c_dec.reshape(RET_PAIRS, HEADS_PER_TILE), HEAD_DIM, axis=1)[:, :, None]
    as_f32 = lambda a: jnp.asarray(a, F32)
    return (as_f32(d_in.reshape(RET_PAIRS, HEADS_PER_TILE * c, c)), as_f32(per_lane(q_dec)), as_f32(per_lane(k_dec)),
            as_f32(state_dec), as_f32(block))


def _retention_kernel(*refs, has_state):
    if has_state:
        q_ref, k_ref, v_ref, g_ref, s0_ref, din_ref, qd_ref, kd_ref, sd_ref, blk_ref, y_ref, sout_ref, s_ref = refs
    else:
        q_ref, k_ref, v_ref, g_ref, din_ref, qd_ref, kd_ref, sd_ref, blk_ref, y_ref, sout_ref, s_ref = refs
    ci = pl.program_id(1)
    c = q_ref.shape[0]

    @pl.when(ci == 0)
    def _():
        s_ref[...] = s0_ref[...] if has_state else jnp.zeros(s_ref.shape, F32)

    lane = lax.broadcasted_iota(jnp.int32, (c, LANES), 1)
    for p in range(RET_PAIRS):
        cols = slice(p * LANES, (p + 1) * LANES)
        q = q_ref[:, cols]
        k = k_ref[:, cols] * HEAD_DIM ** -0.5
        v = v_ref[:, cols].astype(BF16)
        state = s_ref[p]
        att = _dot(_stack_pair(q, lane).astype(BF16), k.astype(BF16), NT_DIMS) * din_ref[p]
        o = _unstack_pair(_dot(att.astype(BF16), v), lane)
        o = o + _dot((q * qd_ref[p]).astype(BF16), state.astype(BF16))
        s_ref[p] = state * sd_ref[p] + _dot((k * kd_ref[p]).astype(BF16), v, TN_DIMS) * blk_ref[...]
        sq = o * o
        even = jnp.sum(jnp.where(lane < HEAD_DIM, sq, 0.0), axis=1, keepdims=True)
        odd = jnp.sum(jnp.where(lane >= HEAD_DIM, sq, 0.0), axis=1, keepdims=True)
        ms = jnp.where(lane < HEAD_DIM, even, odd) * (1.0 / HEAD_DIM)
        y_ref[:, cols] = _silu(g_ref[:, cols]) * (o * lax.rsqrt(ms + EPS))

    @pl.when(ci == pl.num_programs(1) - 1)
    def _():
        sout_ref[...] = s_ref[...]


def _retention(q, k, v, g, state):
    n, t, d = q.shape
    c = math.gcd(t, RET_CHUNK)
    consts = _retention_constants(c)
    chunk = pl.BlockSpec((None, c, d), lambda i, j: (i, j, 0))
    st = pl.BlockSpec((None, RET_PAIRS, LANES, LANES), lambda i, j: (i, 0, 0, 0))
    in_specs, args = [chunk] * 4, [q, k, v, g]
    if state is not None:
        in_specs.append(st)
        args.append(state)
    in_specs += [_resident(a) for a in consts]
    args += list(consts)
    return pl.pallas_call(
        functools.partial(_retention_kernel, has_state=state is not None),
        grid=(n, t // c),
        in_specs=in_specs,
        out_specs=[chunk, st],
        out_shape=[jax.ShapeDtypeStruct((n, t, d), F32), jax.ShapeDtypeStruct((n, RET_PAIRS, LANES, LANES), F32)],
        scratch_shapes=[pltpu.VMEM((RET_PAIRS, LANES, LANES), F32)],
        compiler_params=_params("parallel", "arbitrary"),
        name="retention",
    )(*args)


def _state_to_pairs(s):
    n = s.shape[0]
    s = s.astype(F32).reshape(n, RET_PAIRS, HEADS_PER_TILE, HEAD_DIM, HEAD_DIM)
    eye = jnp.eye(HEADS_PER_TILE, dtype=F32)
    return jnp.einsum("npaij,ab->npaibj", s, eye).reshape(n, RET_PAIRS, LANES, LANES)


def _state_from_pairs(s):
    n = s.shape[0]
    s = s.reshape(n, RET_PAIRS, HEADS_PER_TILE, HEAD_DIM, HEADS_PER_TILE, HEAD_DIM)
    return jnp.stack([s[:, :, a, :, a, :] for a in range(HEADS_PER_TILE)], axis=2).reshape(n, RET_HEADS, HEAD_DIM, HEAD_DIM)


def _layer(x, n, pool_prefix, ret_state, moba, w, final_norm, final):
    t = x.shape[0] // n
    z = _dense_in(x, w["g1"], w["f1i"], w["f1o"], w["gm"], w["win"], seq_len=t if moba is None else None)
    seq = lambda a: a.reshape(n, t, a.shape[-1])
    u = seq(z["u"])
    pos0 = 0 if pool_prefix is None else moba["past_len"]
    if pool_prefix is None:
        y_pool, pool_new = _pool(u, None, w["pool_w"], w["pool_scale"], pos0), u[:, t - POOL_PAST:]
    else:
        halo = jnp.pad(pool_prefix.astype(F32), ((0, 0), (POOL_HALO - POOL_PAST, 0), (0, 0)))
        y_pool = _pool(u, halo, w["pool_w"], w["pool_scale"], pos0)
        pool_new = jnp.concatenate([pool_prefix.astype(F32), u], axis=1)[:, -POOL_PAST:]
    if moba is None:
        y_att = _moba_prompt(seq(z["qa"]), seq(z["ka"]), z["vt"])
        heads = lambda a: jnp.transpose(a.reshape(n, ATT_HEADS, HEAD_DIM, t), (0, 3, 1, 2))
        k_new, v_new = heads(z["kt"]), heads(z["vt"])
    else:
        y_att = _moba_sample(seq(z["qa"]), seq(z["ka"]), seq(z["va"]), moba["cache_k"], moba["cache_v"], moba["page_ids"])
        heads = lambda a: a.reshape(n, t, ATT_HEADS, HEAD_DIM)
        k_new, v_new = heads(z["ka"]), heads(z["va"])
    y_ret, ret_new = _retention(seq(z["qr"]), seq(z["kr"]), seq(z["vr"]), seq(z["gr"]), ret_state)
    flat = lambda a: a.reshape(n * t, a.shape[-1])
    x2 = _dense_out(z["x1"], flat(y_pool), flat(y_att), flat(y_ret), w["wo"], w["g2"], w["f2i"], w["f2o"], final_norm,
                    final)
    return x2, k_new, v_new, pool_new, _state_from_pairs(ret_new)


def kernel(x_prompt, x_sample, cache_k, cache_v, state_pool, state_ret, page_table, norm_ffn1, ffn1_w_in, ffn1_w_out,
           norm_mix, w_in, pool_w, pool_scale, w_out, norm_ffn2, ffn2_w_in, ffn2_w_out, norm_final):
    depth = w_in.shape[0]
    n_p, t_p, d = x_prompt.shape
    n_s, t_s, _ = x_sample.shape
    n_pool = cache_k.shape[1]
    pages_per_seq = page_table.shape[1]
    assert cache_k.shape[2] == PAGE_SIZE and d % LANES == 0
    pages_t = lambda a: jnp.transpose(a, (0, 1, 3, 4, 2)).reshape(depth * n_pool, ATT_DIM, PAGE_SIZE)
    cache_k, cache_v = pages_t(cache_k), pages_t(cache_v)
    row = lambda a: a.reshape(1, -1).astype(F32)
    final_norm = row(norm_final)
    hp, hs = x_prompt.reshape(n_p * t_p, d), x_sample.reshape(n_s * t_s, d)
    outs = [[] for _ in range(8)]
    for l in range(depth):
        w = dict(g1=row(norm_ffn1[l]), f1i=ffn1_w_in[l].astype(BF16), f1o=ffn1_w_out[l].astype(BF16),
                 gm=row(norm_mix[l]), win=w_in[l].astype(BF16),
                 pool_w=jax.scipy.linalg.block_diag(*pool_w[l]).astype(BF16), pool_scale=row(pool_scale[l]),
                 wo=w_out[l].astype(BF16), g2=row(norm_ffn2[l]), f2i=ffn2_w_in[l].astype(BF16),
                 f2o=ffn2_w_out[l].astype(BF16))
        final = l == depth - 1
        hp, kp, vp, pp, rp = _layer(hp, n_p, None, None, None, w, final_norm, final)
        moba = dict(cache_k=cache_k, cache_v=cache_v, past_len=pages_per_seq * PAGE_SIZE,
                    page_ids=(page_table.astype(jnp.int32) + l * n_pool).reshape(-1))
        hs, ks, vs, ps, rs = _layer(hs, n_s, state_pool[l], _state_to_pairs(state_ret[l]), moba, w, final_norm, final)
        for lst, a in zip(outs, (kp, vp, pp, rp, ks, vs, ps, rs)):
            lst.append(a)
    return (hp.reshape(n_p, t_p, d), hs.reshape(n_s, t_s, d)) + tuple(jnp.stack(lst) for lst in outs)
```

```python
import functools
import math

import numpy as np
import jax
import jax.numpy as jnp
from jax import lax
from jax.experimental import pallas as pl
from jax.experimental.pallas import tpu as pltpu

F32 = jnp.float32
BF16 = jnp.bfloat16

HEAD_DIM = 64
POOL_WINDOWS = (2, 4, 8, 16)
POOL_GRP = HEAD_DIM
POOL_DIM = len(POOL_WINDOWS) * POOL_GRP
POOL_PAST = max(POOL_WINDOWS) - 1
ATT_HEADS = 6
ATT_DIM = ATT_HEADS * HEAD_DIM
RET_HEADS = 6
RET_DIM = RET_HEADS * HEAD_DIM
MOBA_BLOCK = 256
MOBA_TOPK = 3
PAGE_SIZE = 128
RET_CHUNK = 128
EPS = 1e-6

LANES = 128
SUBLANES = 8
HEADS_PER_TILE = LANES // HEAD_DIM
ATT_PAIRS = ATT_DIM // LANES
RET_PAIRS = RET_DIM // LANES
VMEM_LIMIT = 56 * 1024 * 1024

TOKEN_TILE = 512
FF_CHUNK = 256
POOL_TILE = 512
POOL_HALO = 16
LOG2_E = math.log2(math.e)
PAGES_PER_STEP = 16
RET_SEQS = 4
NT_DIMS = (((1,), (1,)), ((), ()))
TN_DIMS = (((0,), (0,)), ((), ()))


def _params(*semantics):
    return pltpu.CompilerParams(dimension_semantics=semantics, vmem_limit_bytes=VMEM_LIMIT)


def _resident(a):
    return pl.BlockSpec(a.shape, lambda *_: (0,) * a.ndim, pipeline_mode=pl.Buffered(1))


def _rmsnorm(x, g):
    return x * lax.rsqrt(jnp.mean(x * x, axis=-1, keepdims=True) + EPS) * g


def _silu(x):
    return x * (1.0 / (1.0 + jnp.exp(-x)))


def _dot(a, b, dims=None, **kw):
    if dims is None:
        return jnp.dot(a, b, preferred_element_type=F32, **kw)
    return lax.dot_general(a, b, dims, preferred_element_type=F32, **kw)


def _stack_pair(x, lane):
    return jnp.concatenate([jnp.where(lane < HEAD_DIM, x, 0.0), jnp.where(lane >= HEAD_DIM, x, 0.0)], axis=0)


def _unstack_pair(y, lane):
    r = y.shape[0] // 2
    return jnp.where(lane < HEAD_DIM, y[:r], y[r:])


def _alibi_slope(h):
    return 2.0 ** (-8.0 * (h + 1) / ATT_HEADS)


def _swiglu_half_step(x, g_ref, wi_ref, wo_ref):
    d_ff = wo_ref.shape[0]
    h = _rmsnorm(x, g_ref[...]).astype(BF16)
    acc = jnp.zeros(x.shape, F32)
    for c in range(d_ff // FF_CHUNK):
        lo = c * FF_CHUNK
        gate = _dot(h, wi_ref[:, lo:lo + FF_CHUNK])
        up = _dot(h, wi_ref[:, d_ff + lo:d_ff + lo + FF_CHUNK])
        acc = acc + _dot((_silu(gate) * up).astype(BF16), wo_ref[lo:lo + FF_CHUNK, :])
    return x + 0.5 * acc


def _dense_in_kernel(x_ref, g1_ref, f1i_ref, f1o_ref, gm_ref, win_ref, *refs, col_offsets, feature_major):
    if feature_major:
        wt_ref, refs = refs[0], refs[1:]
    x1_ref, z_refs, zt_refs = refs[0], refs[1:1 + len(col_offsets)], refs[1 + len(col_offsets):]
    x1 = _swiglu_half_step(x_ref[...], g1_ref, f1i_ref, f1o_ref)
    x1_ref[...] = x1
    h = _rmsnorm(x1, gm_ref[...]).astype(BF16)
    for z_ref, off in zip(z_refs, col_offsets):
        z_ref[...] = _dot(h, win_ref[:, off:off + z_ref.shape[-1]]).astype(z_ref.dtype)
    if feature_major:
        zt = _dot(wt_ref[...], h, NT_DIMS)
        off = 0
        for zt_ref in zt_refs:
            zt_ref[...] = zt[off:off + zt_ref.shape[0]]
            off += zt_ref.shape[0]


Z_COLUMNS = dict(zip(("u", "qa", "ka", "va", "qr", "kr", "vr", "gr"),
                     ((0, POOL_DIM), (POOL_DIM, ATT_DIM), (POOL_DIM + ATT_DIM, ATT_DIM), (POOL_DIM + 2 * ATT_DIM, ATT_DIM))
                     + tuple((POOL_DIM + 3 * ATT_DIM + i * RET_DIM, RET_DIM) for i in range(4))))


def _dense_in(x, g1, f1i, f1o, gm, win, seq_len=None):
    t, d = x.shape
    tm = min(TOKEN_TILE, t)
    assert t % tm == 0 and win.shape[1] == sum(w for _, w in Z_COLUMNS.values()) and f1o.shape[0] % FF_CHUNK == 0
    rows = lambda w: pl.BlockSpec((tm, w), lambda i: (i, 0))
    names = [k for k in Z_COLUMNS if not (seq_len and k == "va")]
    dtypes = [BF16 if (seq_len and k == "ka") else F32 for k in names]
    in_specs = [rows(d)] + [_resident(a) for a in (g1, f1i, f1o, gm, win)]
    args = [x, g1, f1i, f1o, gm, win]
    out_specs = [rows(d)] + [rows(Z_COLUMNS[k][1]) for k in names]
    out_shape = [jax.ShapeDtypeStruct((t, d), F32)] + [jax.ShapeDtypeStruct((t, Z_COLUMNS[k][1]), dt)
                                                      for k, dt in zip(names, dtypes)]
    if seq_len:
        assert seq_len % tm == 0 and Z_COLUMNS["va"][0] == sum(Z_COLUMNS["ka"])
        tiles = seq_len // tm
        kv_t = win[:, Z_COLUMNS["ka"][0]:sum(Z_COLUMNS["va"])].T
        in_specs.append(_resident(kv_t))
        args.append(kv_t)
        out_specs += [pl.BlockSpec((None, ATT_DIM, tm), lambda i: (i // tiles, 0, i % tiles))] * 2
        out_shape += [jax.ShapeDtypeStruct((t // seq_len, ATT_DIM, seq_len), F32)] * 2
        names += ["kt", "vt"]
    outs = pl.pallas_call(
        functools.partial(_dense_in_kernel, col_offsets=tuple(Z_COLUMNS[k][0] for k in names if k in Z_COLUMNS),
                          feature_major=bool(seq_len)),
        grid=(t // tm,),
        in_specs=in_specs,
        out_specs=out_specs,
        out_shape=out_shape,
        compiler_params=_params("parallel"),
        name="dense_in",
    )(*args)
    return dict(zip(["x1"] + names, outs))


def _dense_out_kernel(x_ref, yp_ref, ya_ref, yr_ref, wo_ref, g2_ref, f2i_ref, f2o_ref, gf_ref, o_ref, *, final):
    off, mix = 0, None
    for y_ref in (yp_ref, ya_ref, yr_ref):
        w = y_ref.shape[-1]
        part = _dot(y_ref[...].astype(BF16), wo_ref[off:off + w, :])
        mix = part if mix is None else mix + part
        off += w
    x = _swiglu_half_step(x_ref[...] + mix, g2_ref, f2i_ref, f2o_ref)
    if final:
        x = _rmsnorm(x, gf_ref[...])
    o_ref[...] = x


def _dense_out(x, yp, ya, yr, wo, g2, f2i, f2o, gf, final):
    t, d = x.shape
    tm = min(TOKEN_TILE, t)
    assert t % tm == 0
    rows = lambda w: pl.BlockSpec((tm, w), lambda i: (i, 0))
    return pl.pallas_call(
        functools.partial(_dense_out_kernel, final=final),
        grid=(t // tm,),
        in_specs=[rows(d), rows(yp.shape[1]), rows(ya.shape[1]), rows(yr.shape[1])]
        + [_resident(a) for a in (wo, g2, f2i, f2o, gf)],
        out_specs=rows(d),
        out_shape=jax.ShapeDtypeStruct((t, d), F32),
        compiler_params=_params("parallel"),
        name="dense_out",
    )(x, yp, ya, yr, wo, g2, f2i, f2o, gf)


def _pool_kernel(*refs, tt, pos0, has_prefix):
    if has_prefix:
        u_ref, pre_ref, w_ref, scale_ref, y_ref, carry_ref = refs
    else:
        u_ref, w_ref, scale_ref, y_ref, carry_ref = refs
    t = pl.program_id(1)

    @pl.when(t == 0)
    def _():
        carry_ref[...] = pre_ref[...] if has_prefix else jnp.zeros(carry_ref.shape, F32)

    pos = pos0 + t * tt + lax.broadcasted_iota(jnp.int32, (tt, 1), 0)
    lane = lax.broadcasted_iota(jnp.int32, (1, POOL_DIM), 1)
    for b in range(u_ref.shape[0]):
        u = u_ref[b]
        ext = jnp.concatenate([carry_ref[b], u], axis=0)
        sums, s = [], ext
        for w in POOL_WINDOWS:
            s = s + pltpu.roll(s, w // 2, axis=0)
            sums.append(s[POOL_HALO:])
        pooled = None
        for g, w in reversed(list(enumerate(POOL_WINDOWS))):
            mean = sums[g] * (1.0 / jnp.minimum(w, pos + 1).astype(F32))
            pooled = mean if pooled is None else jnp.where(lane < (g + 1) * POOL_GRP, mean, pooled)
        d = (pooled - u).astype(BF16)
        y_ref[b] = _dot(d, w_ref[...]) * scale_ref[...]
        carry_ref[b] = ext[tt:]


def _pool(u, prefix, w_blockdiag, scale, pos0):
    n, t, c = u.shape
    tt = min(POOL_TILE, t)
    nb = math.gcd(n, POOL_TILE // tt)
    assert t % tt == 0 and tt % SUBLANES == 0 and POOL_WINDOWS == (2, 4, 8, 16)
    seq = pl.BlockSpec((nb, tt, c), lambda i, j: (i, j, 0))
    in_specs, args = [seq], [u]
    if prefix is not None:
        in_specs.append(pl.BlockSpec((nb, POOL_HALO, c), lambda i, j: (i, 0, 0)))
        args.append(prefix)
    in_specs += [_resident(w_blockdiag), _resident(scale)]
    args += [w_blockdiag, scale]
    return pl.pallas_call(
        functools.partial(_pool_kernel, tt=tt, pos0=pos0, has_prefix=prefix is not None),
        grid=(n // nb, t // tt),
        in_specs=in_specs,
        out_specs=seq,
        out_shape=jax.ShapeDtypeStruct((n, t, c), F32),
        scratch_shapes=[pltpu.VMEM((nb, POOL_HALO, c), F32)],
        compiler_params=_params("parallel", "arbitrary"),
        name="pool_mixer",
    )(*args)


def _moba_prompt_kernel(q_ref, k_ref, vt_in_ref, o_ref, kmean_ref, vt_ref, bias_ref, t_ref, *, n_blocks):
    qi = pl.program_id(1)
    tq = q_ref.shape[0]
    rows = HEADS_PER_TILE * tq
    dim = lax.broadcasted_iota(jnp.int32, (LANES, 1), 0)
    key = lax.broadcasted_iota(jnp.int32, (MOBA_BLOCK, rows), 0)
    col = lax.broadcasted_iota(jnp.int32, (1, rows), 1)
    blk = lax.broadcasted_iota(jnp.int32, (kmean_ref.shape[0], 1), 0)
    slopes = [jnp.where(col < tq, _alibi_slope(2 * p) * LOG2_E, _alibi_slope(2 * p + 1) * LOG2_E).astype(F32)
              for p in range(ATT_PAIRS)]

    @pl.when(qi == 0)
    def _():
        kmean_ref[...] = jnp.zeros(kmean_ref.shape, F32)
        for j in range(n_blocks):
            keys = slice(j * MOBA_BLOCK, (j + 1) * MOBA_BLOCK)
            kmean_ref[j:j + 1, :] = jnp.mean(k_ref[keys, :].astype(F32), axis=0, keepdims=True)
            vt_ref[j] = vt_in_ref[:, keys].astype(BF16)
        for p in range(ATT_PAIRS):
            bias_ref[p] = slopes[p] * key.astype(F32)

    def tile(own):
        causal = key <= lax.rem(col, tq)
        for p in range(ATT_PAIRS):
            cols = slice(p * LANES, (p + 1) * LANES)
            q_t = (q_ref[:, cols] * (HEAD_DIM ** -0.5 * LOG2_E)).T
            qs_t = jnp.concatenate([jnp.where(dim < HEAD_DIM, q_t, 0.0), jnp.where(dim >= HEAD_DIM, q_t, 0.0)], axis=1)
            qs_b = qs_t.astype(BF16)
            sel = None
            if own > MOBA_TOPK:
                gate = _dot(kmean_ref[:, cols], qs_t, precision=lax.Precision.HIGHEST)
                sel = []
                for j in range(own):
                    gj = gate[j:j + 1, :]
                    ahead = jnp.where(blk < j, jnp.where(gate >= gj, 1.0, 0.0), jnp.where(gate > gj, 1.0, 0.0))
                    sel.append(jnp.sum(jnp.where(blk < own, ahead, 0.0), axis=0, keepdims=True) < MOBA_TOPK)
            block_bias = lambda j: slopes[p] * float((j - own) * MOBA_BLOCK)
            maxima = []
            for j in range(own + 1):
                t = _dot(k_ref[j * MOBA_BLOCK:(j + 1) * MOBA_BLOCK, cols], qs_b) + bias_ref[p]
                if j == own:
                    t = jnp.where(causal, t, -jnp.inf)
                t_ref[p, j] = t
                m = jnp.max(t, axis=0, keepdims=True) + block_bias(j)
                maxima.append(m if sel is None or j == own else jnp.where(sel[j], m, -jnp.inf))
            m_all = functools.reduce(jnp.maximum, maxima)
            acc, den = None, None
            for j in range(own + 1):
                shift = m_all - block_bias(j)
                if sel is not None and j < own:
                    shift = jnp.where(sel[j], shift, jnp.inf)
                e = jnp.exp2(t_ref[p, j] - shift)
                part_den = jnp.sum(e, axis=0, keepdims=True)
                part = _dot(vt_ref[j, cols, :], e.astype(BF16))
                acc, den = (part, part_den) if acc is None else (acc + part, den + part_den)
            out_t = acc / den
            o_ref[:, cols] = jnp.where(dim < HEAD_DIM, out_t[:, :tq], out_t[:, tq:]).T

    for own in range(n_blocks):
        pl.when(qi == own)(functools.partial(tile, own))


def _moba_prompt(q, k, vt):
    n, t, c = q.shape
    tq = MOBA_BLOCK
    n_blocks = t // MOBA_BLOCK
    gate_rows = -(-n_blocks // SUBLANES) * SUBLANES
    rows = HEADS_PER_TILE * tq
    assert t % MOBA_BLOCK == 0 and c == ATT_DIM and vt.shape == (n, c, t)
    tile = pl.BlockSpec((None, tq, c), lambda i, j: (i, j, 0))
    return pl.pallas_call(
        functools.partial(_moba_prompt_kernel, n_blocks=n_blocks),
        grid=(n, n_blocks),
        in_specs=[tile, pl.BlockSpec((None, t, c), lambda i, j: (i, 0, 0)), pl.BlockSpec((None, c, t), lambda i, j: (i, 0, 0))],
        out_specs=tile,
        out_shape=jax.ShapeDtypeStruct((n, t, c), F32),
        scratch_shapes=[pltpu.VMEM((gate_rows, c), F32),
                        pltpu.VMEM((n_blocks, c, MOBA_BLOCK), BF16),
                        pltpu.VMEM((ATT_PAIRS, MOBA_BLOCK, rows), F32),
                        pltpu.VMEM((ATT_PAIRS, n_blocks, MOBA_BLOCK, rows), F32)],
        compiler_params=_params("parallel", "arbitrary"),
        name="moba_prompt",
    )(q, k, vt)


def _head_rows(x):
    t = x.shape[0]
    lane_head = lax.broadcasted_iota(jnp.int32, (1, ATT_DIM), 1) // HEAD_DIM
    row_head = lax.broadcasted_iota(jnp.int32, (ATT_HEADS * t, 1), 0) // t
    return jnp.where(lane_head == row_head, jnp.concatenate([x] * ATT_HEADS, axis=0), 0.0)


def _own_head_columns(y, t):
    lane_head = lax.broadcasted_iota(jnp.int32, (1, ATT_DIM), 1) // HEAD_DIM
    out = jnp.zeros((t, ATT_DIM), F32)
    for h in range(ATT_HEADS):
        out = jnp.where(lane_head == h, y[h * t:(h + 1) * t], out)
    return out


def _moba_sample_kernel(pt_ref, q_ref, kn_ref, vn_ref, *refs, n_past_blocks):
    del pt_ref
    k_pages = refs[:PAGES_PER_STEP]
    v_pages = refs[PAGES_PER_STEP:2 * PAGES_PER_STEP]
    o_ref, m_ref, l_ref, g_ref, acc_ref = refs[2 * PAGES_PER_STEP:]
    c = pl.program_id(1)
    t = q_ref.shape[0]
    rows = ATT_HEADS * t
    pages_per_block = MOBA_BLOCK // PAGE_SIZE
    blocks_per_step = PAGES_PER_STEP // pages_per_block
    past_len = n_past_blocks * MOBA_BLOCK

    row_head = lax.broadcasted_iota(jnp.int32, (rows, 1), 0) // t
    slope = jnp.zeros((rows, 1), F32)
    for h in range(ATT_HEADS):
        slope = jnp.where(row_head == h, _alibi_slope(h), slope)
    q_f32 = _head_rows(q_ref[...] * HEAD_DIM ** -0.5)
    q_all = q_f32.astype(BF16)
    q_split = jnp.concatenate([q_all, (q_f32 - q_all.astype(F32)).astype(BF16)], axis=0)
    kcol = lax.broadcasted_iota(jnp.int32, (1, MOBA_BLOCK), 1)
    blk = lax.broadcasted_iota(jnp.int32, (1, LANES), 1)

    @pl.when(c == 0)
    def _():
        g_ref[...] = jnp.zeros(g_ref.shape, F32)
        m_ref[...] = jnp.full(m_ref.shape, -jnp.inf, F32)
        l_ref[...] = jnp.zeros(l_ref.shape, F32)

    raw = [_dot(q_split, k_ref[...].astype(BF16)) for k_ref in k_pages]
    raw = [r[:rows] + r[rows:] for r in raw]
    for b in range(blocks_per_step):
        j = c * blocks_per_step + b
        pages = range(b * pages_per_block, (b + 1) * pages_per_block)
        s = jnp.concatenate([raw[i] for i in pages], axis=1)
        g_ref[...] = jnp.where(blk == j, jnp.sum(s, axis=1, keepdims=True), g_ref[...])
        s = s + slope * (kcol + (j * MOBA_BLOCK - past_len)).astype(F32)
        m = jnp.max(s, axis=1, keepdims=True)
        e = jnp.exp(s - m)
        m_ref[...] = jnp.where(blk == j, m, m_ref[...])
        l_ref[...] = jnp.where(blk == j, jnp.sum(e, axis=1, keepdims=True), l_ref[...])
        e = e.astype(BF16)
        acc = None
        for n, i in enumerate(pages):
            part = _dot(e[:, n * PAGE_SIZE:(n + 1) * PAGE_SIZE], v_pages[i][...].astype(BF16), NT_DIMS)
            acc = part if acc is None else acc + part
        acc_ref[j] = acc

    @pl.when(c == pl.num_programs(1) - 1)
    def _():
        pad = jnp.zeros((PAGE_SIZE - t, ATT_DIM), F32)
        k_new = jnp.concatenate([kn_ref[...], pad], axis=0).astype(BF16)
        v_new = jnp.concatenate([vn_ref[...], pad], axis=0).astype(BF16)
        ncol = lax.broadcasted_iota(jnp.int32, (1, PAGE_SIZE), 1)
        qrow = lax.rem(lax.broadcasted_iota(jnp.int32, (rows, 1), 0), t)
        s = _dot(q_all, k_new, NT_DIMS) + slope * ncol.astype(F32)
        s = jnp.where(ncol <= qrow, s, -jnp.inf)
        m = jnp.max(s, axis=1, keepdims=True)
        e = jnp.exp(s - m)
        blk_f = blk.astype(F32)
        gate = jnp.where(blk < n_past_blocks, g_ref[...], -jnp.inf)
        picked = jnp.zeros(gate.shape, F32)
        for _ in range(MOBA_TOPK):
            top = jnp.max(gate, axis=1, keepdims=True)
            first = jnp.min(jnp.where(gate == top, blk_f, float(LANES)), axis=1, keepdims=True)
            picked = jnp.where(blk_f == first, 1.0, picked)
            gate = jnp.where(blk_f == first, -jnp.inf, gate)
        m_all = jnp.maximum(m, jnp.max(jnp.where(picked > 0.0, m_ref[...], -jnp.inf), axis=1, keepdims=True))
        w = jnp.where(picked > 0.0, jnp.exp(m_ref[...] - m_all), 0.0)
        w_new = jnp.exp(m - m_all)
        den = w_new * jnp.sum(e, axis=1, keepdims=True) + jnp.sum(w * l_ref[...], axis=1, keepdims=True)
        parts = [w_new * _dot(e.astype(BF16), v_new)] + [w[:, j:j + 1] * acc_ref[j] for j in range(n_past_blocks)]
        while len(parts) > 1:
            parts = [a + b for a, b in zip(parts[::2], parts[1::2])] + parts[len(parts) - len(parts) % 2:]
        o_ref[...] = _own_head_columns(parts[0] / den, t)


def _moba_sample(q, k_new, v_new, cache_kt, cache_vt, page_ids):
    n, t, c = q.shape
    pages_per_seq = page_ids.shape[0] // n
    past_len = pages_per_seq * PAGE_SIZE
    n_past_blocks = past_len // MOBA_BLOCK
    assert past_len % MOBA_BLOCK == 0 and t <= PAGE_SIZE and t % SUBLANES == 0 and c == ATT_DIM
    assert pages_per_seq % PAGES_PER_STEP == 0 and MOBA_TOPK <= n_past_blocks <= LANES
    new = pl.BlockSpec((None, t, c), lambda i, j, pt: (i, 0, 0))

    def page(slot):
        return pl.BlockSpec((None, c, PAGE_SIZE),
                            lambda i, j, pt: (pt[i * pages_per_seq + j * PAGES_PER_STEP + slot], 0, 0))

    pages = [page(s) for s in range(PAGES_PER_STEP)]
    rows = ATT_HEADS * t
    stat = pltpu.VMEM((rows, LANES), F32)
    return pl.pallas_call(
        functools.partial(_moba_sample_kernel, n_past_blocks=n_past_blocks),
        grid_spec=pltpu.PrefetchScalarGridSpec(
            num_scalar_prefetch=1,
            grid=(n, pages_per_seq // PAGES_PER_STEP),
            in_specs=[new, new, new] + pages + pages,
            out_specs=new,
            scratch_shapes=[stat, stat, stat, pltpu.VMEM((n_past_blocks, rows, c), F32)],
        ),
        out_shape=jax.ShapeDtypeStruct((n, t, c), F32),
        compiler_params=_params("parallel", "arbitrary"),
        name="moba_sample",
    )(page_ids, q, k_new, v_new, *([cache_kt] * PAGES_PER_STEP), *([cache_vt] * PAGES_PER_STEP))


def _retention_constants(c):
    lg = np.log1p(-np.exp2(-5.0 - np.arange(RET_HEADS, dtype=np.float64)))
    idx = np.arange(c, dtype=np.float64)
    diff = idx[:, None] - idx[None, :]
    d_in = np.where(diff >= 0, np.exp(lg[:, None, None] * np.maximum(diff, 0.0)), 0.0)
    q_dec = np.exp(lg[:, None] * (idx + 1.0))
    k_dec = np.exp(lg[:, None] * (c - 1.0 - idx))
    c_dec = np.exp(lg * c)
    per_lane = lambda a: np.repeat(a.reshape(RET_PAIRS, HEADS_PER_TILE, -1).transpose(0, 2, 1), HEAD_DIM, axis=2)
    block = np.kron(np.eye(HEADS_PER_TILE), np.ones((HEAD_DIM, HEAD_DIM)))
    state_dec = block[None] * np.repeat(c_dec.reshape(RET_PAIRS, HEADS_PER_TILE), HEAD_DIM, axis=1)[:, :, None]
    as_f32 = lambda a: jnp.asarray(a, F32)
    return (as_f32(d_in.reshape(RET_PAIRS, HEADS_PER_TILE * c, c)), as_f32(per_lane(q_dec)), as_f32(per_lane(k_dec)),
            as_f32(state_dec), as_f32(block))


def _retention_kernel(*refs, has_state):
    if has_state:
        q_ref, k_ref, v_ref, g_ref, s0_ref, din_ref, qd_ref, kd_ref, sd_ref, blk_ref, y_ref, sout_ref, s_ref = refs
    else:
        q_ref, k_ref, v_ref, g_ref, din_ref, qd_ref, kd_ref, sd_ref, blk_ref, y_ref, sout_ref, s_ref = refs
    ci = pl.program_id(1)
    n_seqs, c, _ = q_ref.shape

    @pl.when(ci == 0)
    def _():
        if not has_state:
            s_ref[...] = jnp.zeros(s_ref.shape, F32)
            return
        zero = jnp.zeros((HEAD_DIM, HEAD_DIM), F32)
        for b in range(n_seqs):
            for p in range(RET_PAIRS):
                even, odd = s0_ref[b, 2 * p].astype(F32), s0_ref[b, 2 * p + 1].astype(F32)
                s_ref[b, p] = jnp.concatenate([jnp.concatenate([even, zero], axis=1),
                                               jnp.concatenate([zero, odd], axis=1)], axis=0)

    lane = lax.broadcasted_iota(jnp.int32, (c, LANES), 1)
    for b in range(n_seqs):
        for p in range(RET_PAIRS):
            cols = slice(p * LANES, (p + 1) * LANES)
            q = q_ref[b, :, cols]
            k = k_ref[b, :, cols] * HEAD_DIM ** -0.5
            v = v_ref[b, :, cols].astype(BF16)
            state = s_ref[b, p]
            att = _dot(_stack_pair(q, lane).astype(BF16), k.astype(BF16), NT_DIMS) * din_ref[p]
            o = _unstack_pair(_dot(att.astype(BF16), v), lane)
            o = o + _dot((q * qd_ref[p]).astype(BF16), state.astype(BF16))
            s_ref[b, p] = state * sd_ref[p] + _dot((k * kd_ref[p]).astype(BF16), v, TN_DIMS) * blk_ref[...]
            sq = o * o
            even = jnp.sum(jnp.where(lane < HEAD_DIM, sq, 0.0), axis=1, keepdims=True)
            odd = jnp.sum(jnp.where(lane >= HEAD_DIM, sq, 0.0), axis=1, keepdims=True)
            ms = jnp.where(lane < HEAD_DIM, even, odd) * (1.0 / HEAD_DIM)
            y_ref[b, :, cols] = _silu(g_ref[b, :, cols]) * (o * lax.rsqrt(ms + EPS))

    @pl.when(ci == pl.num_programs(1) - 1)
    def _():
        for b in range(n_seqs):
            for p in range(RET_PAIRS):
                state = s_ref[b, p]
                sout_ref[b, 2 * p] = state[:HEAD_DIM, :HEAD_DIM]
                sout_ref[b, 2 * p + 1] = state[HEAD_DIM:, HEAD_DIM:]


def _retention(q, k, v, g, state):
    n, t, d = q.shape
    c = math.gcd(t, RET_CHUNK)
    nb = math.gcd(n, RET_SEQS)
    consts = _retention_constants(c)
    chunk = pl.BlockSpec((nb, c, d), lambda i, j: (i, j, 0))
    st = pl.BlockSpec((nb, RET_HEADS, HEAD_DIM, HEAD_DIM), lambda i, j: (i, 0, 0, 0))
    in_specs, args = [chunk] * 4, [q, k, v, g]
    if state is not None:
        in_specs.append(st)
        args.append(state)
    in_specs += [_resident(a) for a in consts]
    args += list(consts)
    return pl.pallas_call(
        functools.partial(_retention_kernel, has_state=state is not None),
        grid=(n // nb, t // c),
        in_specs=in_specs,
        out_specs=[chunk, st],
        out_shape=[jax.ShapeDtypeStruct((n, t, d), F32), jax.ShapeDtypeStruct((n, RET_HEADS, HEAD_DIM, HEAD_DIM), F32)],
        scratch_shapes=[pltpu.VMEM((nb, RET_PAIRS, LANES, LANES), F32)],
        compiler_params=_params("parallel", "arbitrary"),
        name="retention",
    )(*args)


def _layer(x, n, pool_prefix, ret_state, moba, w, final_norm, final):
    t = x.shape[0] // n
    z = _dense_in(x, w["g1"], w["f1i"], w["f1o"], w["gm"], w["win"], seq_len=t if moba is None else None)
    seq = lambda a: a.reshape(n, t, a.shape[-1])
    u = seq(z["u"])
    pos0 = 0 if pool_prefix is None else moba["past_len"]
    if pool_prefix is None:
        y_pool, pool_new = _pool(u, None, w["pool_w"], w["pool_scale"], pos0), u[:, t - POOL_PAST:]
    else:
        halo = jnp.pad(pool_prefix.astype(F32), ((0, 0), (POOL_HALO - POOL_PAST, 0), (0, 0)))
        y_pool = _pool(u, halo, w["pool_w"], w["pool_scale"], pos0)
        pool_new = jnp.concatenate([pool_prefix.astype(F32), u], axis=1)[:, -POOL_PAST:]
    if moba is None:
        y_att = _moba_prompt(seq(z["qa"]), seq(z["ka"]), z["vt"])
        heads = lambda a: jnp.transpose(a.reshape(n, ATT_HEADS, HEAD_DIM, t), (0, 3, 1, 2))
        k_new, v_new = heads(z["kt"]), heads(z["vt"])
    else:
        y_att = _moba_sample(seq(z["qa"]), seq(z["ka"]), seq(z["va"]), moba["cache_k"], moba["cache_v"], moba["page_ids"])
        heads = lambda a: a.reshape(n, t, ATT_HEADS, HEAD_DIM)
        k_new, v_new = heads(z["ka"]), heads(z["va"])
    y_ret, ret_new = _retention(seq(z["qr"]), seq(z["kr"]), seq(z["vr"]), seq(z["gr"]), ret_state)
    flat = lambda a: a.reshape(n * t, a.shape[-1])
    x2 = _dense_out(z["x1"], flat(y_pool), flat(y_att), flat(y_ret), w["wo"], w["g2"], w["f2i"], w["f2o"], final_norm,
                    final)
    return x2, k_new, v_new, pool_new, ret_new


def kernel(x_prompt, x_sample, cache_k, cache_v, state_pool, state_ret, page_table, norm_ffn1, ffn1_w_in, ffn1_w_out,
           norm_mix, w_in, pool_w, pool_scale, w_out, norm_ffn2, ffn2_w_in, ffn2_w_out, norm_final):
    depth = w_in.shape[0]
    n_p, t_p, d = x_prompt.shape
    n_s, t_s, _ = x_sample.shape
    n_pool = cache_k.shape[1]
    pages_per_seq = page_table.shape[1]
    assert cache_k.shape[2] == PAGE_SIZE and d % LANES == 0
    pages_t = lambda a: jnp.transpose(a, (0, 1, 3, 4, 2)).reshape(depth * n_pool, ATT_DIM, PAGE_SIZE)
    cache_k, cache_v = pages_t(cache_k), pages_t(cache_v)
    row = lambda a: a.reshape(1, -1).astype(F32)
    final_norm = row(norm_final)
    hp, hs = x_prompt.reshape(n_p * t_p, d), x_sample.reshape(n_s * t_s, d)
    outs = [[] for _ in range(8)]
    for l in range(depth):
        w = dict(g1=row(norm_ffn1[l]), f1i=ffn1_w_in[l].astype(BF16), f1o=ffn1_w_out[l].astype(BF16),
                 gm=row(norm_mix[l]), win=w_in[l].astype(BF16),
                 pool_w=jax.scipy.linalg.block_diag(*pool_w[l]).astype(BF16), pool_scale=row(pool_scale[l]),
                 wo=w_out[l].astype(BF16), g2=row(norm_ffn2[l]), f2i=ffn2_w_in[l].astype(BF16),
                 f2o=ffn2_w_out[l].astype(BF16))
        final = l == depth - 1
        hp, kp, vp, pp, rp = _layer(hp, n_p, None, None, None, w, final_norm, final)
        moba = dict(cache_k=cache_k, cache_v=cache_v, past_len=pages_per_seq * PAGE_SIZE,
                    page_ids=(page_table.astype(jnp.int32) + l * n_pool).reshape(-1))
        hs, ks, vs, ps, rs = _layer(hs, n_s, state_pool[l], state_ret[l], moba, w, final_norm, final)
        for lst, a in zip(outs, (kp, vp, pp, rp, ks, vs, ps, rs)):
            lst.append(a)
    return (hp.reshape(n_p, t_p, d), hs.reshape(n_s, t_s, d)) + tuple(jnp.stack(lst) for lst in outs)
```

```python
import functools
import math

import numpy as np
import jax
import jax.numpy as jnp
from jax import lax
from jax.experimental import pallas as pl
from jax.experimental.pallas import tpu as pltpu

F32 = jnp.float32
BF16 = jnp.bfloat16

HEAD_DIM = 64
POOL_WINDOWS = (2, 4, 8, 16)
POOL_GRP = HEAD_DIM
POOL_DIM = len(POOL_WINDOWS) * POOL_GRP
POOL_PAST = max(POOL_WINDOWS) - 1
ATT_HEADS = 6
ATT_DIM = ATT_HEADS * HEAD_DIM
RET_HEADS = 6
RET_DIM = RET_HEADS * HEAD_DIM
MOBA_BLOCK = 256
MOBA_TOPK = 3
PAGE_SIZE = 128
RET_CHUNK = 128
EPS = 1e-6

LANES = 128
SUBLANES = 8
BF16_SUBLANES = 16
SLOPE_PARTS = 3
HEADS_PER_TILE = LANES // HEAD_DIM
ATT_PAIRS = ATT_DIM // LANES
RET_PAIRS = RET_DIM // LANES
VMEM_LIMIT = 56 * 1024 * 1024

TOKEN_TILE = 512
FF_CHUNK = 256
POOL_TILE = 512
POOL_HALO = 16
LOG2_E = math.log2(math.e)
PAGES_PER_STEP = 16
RET_SEQS = 4
NT_DIMS = (((1,), (1,)), ((), ()))
TN_DIMS = (((0,), (0,)), ((), ()))


def _params(*semantics):
    return pltpu.CompilerParams(dimension_semantics=semantics, vmem_limit_bytes=VMEM_LIMIT)


def _resident(a):
    return pl.BlockSpec(a.shape, lambda *_: (0,) * a.ndim, pipeline_mode=pl.Buffered(1))


def _rmsnorm(x, g):
    return x * lax.rsqrt(jnp.mean(x * x, axis=-1, keepdims=True) + EPS) * g


def _silu(x):
    return x * (1.0 / (1.0 + jnp.exp(-x)))


def _dot(a, b, dims=None, **kw):
    if dims is None:
        return jnp.dot(a, b, preferred_element_type=F32, **kw)
    return lax.dot_general(a, b, dims, preferred_element_type=F32, **kw)


def _stack_pair(x, lane):
    return jnp.concatenate([jnp.where(lane < HEAD_DIM, x, 0.0), jnp.where(lane >= HEAD_DIM, x, 0.0)], axis=0)


def _unstack_pair(y, lane):
    r = y.shape[0] // 2
    return jnp.where(lane < HEAD_DIM, y[:r], y[r:])


def _alibi_slope(h):
    return 2.0 ** (-8.0 * (h + 1) / ATT_HEADS)


def _swiglu_half_step(x, g_ref, wi_ref, wo_ref):
    d_ff = wo_ref.shape[0]
    h = _rmsnorm(x, g_ref[...]).astype(BF16)
    acc = jnp.zeros(x.shape, F32)
    for c in range(d_ff // FF_CHUNK):
        lo = c * FF_CHUNK
        gate = _dot(h, wi_ref[:, lo:lo + FF_CHUNK])
        up = _dot(h, wi_ref[:, d_ff + lo:d_ff + lo + FF_CHUNK])
        acc = acc + _dot((_silu(gate) * up).astype(BF16), wo_ref[lo:lo + FF_CHUNK, :])
    return x + 0.5 * acc


def _pool_tile(u, carry, pos, w_blockdiag, scale):
    tt = u.shape[0]
    ext = jnp.concatenate([carry, u], axis=0)
    sums, s = [], ext
    for w in POOL_WINDOWS:
        s = s + pltpu.roll(s, w // 2, axis=0)
        sums.append(s[POOL_HALO:])
    lane = lax.broadcasted_iota(jnp.int32, (1, POOL_DIM), 1)
    pooled = None
    for g, w in reversed(list(enumerate(POOL_WINDOWS))):
        mean = sums[g] * (1.0 / jnp.minimum(w, pos + 1).astype(F32))
        pooled = mean if pooled is None else jnp.where(lane < (g + 1) * POOL_GRP, mean, pooled)
    d = (pooled - u).astype(BF16)
    return _dot(d, w_blockdiag) * scale, ext[tt:]


def _dense_in_kernel(x_ref, g1_ref, f1i_ref, f1o_ref, gm_ref, win_ref, *refs, col_offsets, seq_tiles):
    if seq_tiles:
        (wt_ref, pw_ref, ps_ref), refs = refs[:3], refs[3:]
        (kt_ref, vt_ref, ypool_ref, tail_ref, carry_ref), refs = refs[-5:], refs[:-5]
    x1_ref, z_refs = refs[0], refs[1:]
    x1 = _swiglu_half_step(x_ref[...], g1_ref, f1i_ref, f1o_ref)
    x1_ref[...] = x1
    h = _rmsnorm(x1, gm_ref[...]).astype(BF16)
    for z_ref, off in zip(z_refs, col_offsets):
        z_ref[...] = _dot(h, win_ref[:, off:off + z_ref.shape[-1]]).astype(z_ref.dtype)
    if seq_tiles:
        zt = _dot(wt_ref[...], h, NT_DIMS)
        kt_ref[...] = zt[:ATT_DIM]
        vt_ref[...] = zt[ATT_DIM:]
        tile = lax.rem(pl.program_id(0), seq_tiles)
        tm = x_ref.shape[0]

        @pl.when(tile == 0)
        def _():
            carry_ref[...] = jnp.zeros(carry_ref.shape, F32)

        u = _dot(h, win_ref[:, Z_COLUMNS["u"][0]:sum(Z_COLUMNS["u"])])
        pos = tile * tm + lax.broadcasted_iota(jnp.int32, (tm, 1), 0)
        y, carry = _pool_tile(u, carry_ref[...], pos, pw_ref[...], ps_ref[...])
        ypool_ref[...] = y
        carry_ref[...] = carry
        tail_ref[...] = carry


Z_COLUMNS = dict(zip(("u", "qa", "ka", "va", "qr", "kr", "vr", "gr"),
                     ((0, POOL_DIM), (POOL_DIM, ATT_DIM), (POOL_DIM + ATT_DIM, ATT_DIM), (POOL_DIM + 2 * ATT_DIM, ATT_DIM))
                     + tuple((POOL_DIM + 3 * ATT_DIM + i * RET_DIM, RET_DIM) for i in range(4))))


def _dense_in(x, g1, f1i, f1o, gm, win, prompt=None):
    t, d = x.shape
    tm = min(TOKEN_TILE, t)
    assert t % tm == 0 and win.shape[1] == sum(w for _, w in Z_COLUMNS.values()) and f1o.shape[0] % FF_CHUNK == 0
    rows = lambda w: pl.BlockSpec((tm, w), lambda i: (i, 0))
    names = [k for k in Z_COLUMNS if not (prompt and k in ("u", "va"))]
    dtypes = [BF16 if (prompt and k == "ka") else F32 for k in names]
    in_specs = [rows(d)] + [_resident(a) for a in (g1, f1i, f1o, gm, win)]
    args = [x, g1, f1i, f1o, gm, win]
    out_specs = [rows(d)] + [rows(Z_COLUMNS[k][1]) for k in names]
    out_shape = [jax.ShapeDtypeStruct((t, d), F32)] + [jax.ShapeDtypeStruct((t, Z_COLUMNS[k][1]), dt)
                                                      for k, dt in zip(names, dtypes)]
    col_offsets = tuple(Z_COLUMNS[k][0] for k in names)
    scratch, tiles = [], 0
    if prompt:
        seq_len, pool_w, pool_scale = prompt
        assert seq_len % tm == 0 and t % seq_len == 0 and Z_COLUMNS["va"][0] == sum(Z_COLUMNS["ka"])
        tiles = seq_len // tm
        kv_t = win[:, Z_COLUMNS["ka"][0]:sum(Z_COLUMNS["va"])].T
        in_specs += [_resident(a) for a in (kv_t, pool_w, pool_scale)]
        args += [kv_t, pool_w, pool_scale]
        out_specs += [pl.BlockSpec((None, ATT_DIM, tm), lambda i: (i // tiles, 0, i % tiles))] * 2
        out_specs += [rows(POOL_DIM), pl.BlockSpec((None, POOL_HALO, POOL_DIM), lambda i: (i // tiles, 0, 0))]
        out_shape += [jax.ShapeDtypeStruct((t // seq_len, ATT_DIM, seq_len), F32)] * 2
        out_shape += [jax.ShapeDtypeStruct((t, POOL_DIM), F32), jax.ShapeDtypeStruct((t // seq_len, POOL_HALO, POOL_DIM), F32)]
        names += ["kt", "vt", "y_pool", "pool_tail"]
        scratch = [pltpu.VMEM((POOL_HALO, POOL_DIM), F32)]
    outs = pl.pallas_call(
        functools.partial(_dense_in_kernel, col_offsets=col_offsets, seq_tiles=tiles),
        grid=(t // tm,),
        in_specs=in_specs,
        out_specs=out_specs,
        out_shape=out_shape,
        scratch_shapes=scratch,
        compiler_params=_params("arbitrary" if prompt else "parallel"),
        name="dense_in",
    )(*args)
    return dict(zip(["x1"] + names, outs))


def _dense_out_kernel(x_ref, yp_ref, ya_ref, yr_ref, wo_ref, g2_ref, f2i_ref, f2o_ref, gf_ref, o_ref, *, final):
    off, mix = 0, None
    for y_ref in (yp_ref, ya_ref, yr_ref):
        w = y_ref.shape[-1]
        part = _dot(y_ref[...].astype(BF16), wo_ref[off:off + w, :])
        mix = part if mix is None else mix + part
        off += w
    x = _swiglu_half_step(x_ref[...] + mix, g2_ref, f2i_ref, f2o_ref)
    if final:
        x = _rmsnorm(x, gf_ref[...])
    o_ref[...] = x


def _dense_out(x, yp, ya, yr, wo, g2, f2i, f2o, gf, final):
    t, d = x.shape
    tm = min(TOKEN_TILE, t)
    assert t % tm == 0
    rows = lambda w: pl.BlockSpec((tm, w), lambda i: (i, 0))
    return pl.pallas_call(
        functools.partial(_dense_out_kernel, final=final),
        grid=(t // tm,),
        in_specs=[rows(d), rows(yp.shape[1]), rows(ya.shape[1]), rows(yr.shape[1])]
        + [_resident(a) for a in (wo, g2, f2i, f2o, gf)],
        out_specs=rows(d),
        out_shape=jax.ShapeDtypeStruct((t, d), F32),
        compiler_params=_params("parallel"),
        name="dense_out",
    )(x, yp, ya, yr, wo, g2, f2i, f2o, gf)


def _pool_kernel(u_ref, pre_ref, w_ref, scale_ref, y_ref, carry_ref, *, tt, pos0):
    t = pl.program_id(1)

    @pl.when(t == 0)
    def _():
        carry_ref[...] = pre_ref[...]

    pos = pos0 + t * tt + lax.broadcasted_iota(jnp.int32, (tt, 1), 0)
    for b in range(u_ref.shape[0]):
        y_ref[b], carry_ref[b] = _pool_tile(u_ref[b], carry_ref[b], pos, w_ref[...], scale_ref[...])


def _pool(u, prefix, w_blockdiag, scale, pos0):
    n, t, c = u.shape
    tt = min(POOL_TILE, t)
    nb = math.gcd(n, POOL_TILE // tt)
    assert t % tt == 0 and tt % SUBLANES == 0 and POOL_WINDOWS == (2, 4, 8, 16)
    seq = pl.BlockSpec((nb, tt, c), lambda i, j: (i, j, 0))
    return pl.pallas_call(
        functools.partial(_pool_kernel, tt=tt, pos0=pos0),
        grid=(n // nb, t // tt),
        in_specs=[seq, pl.BlockSpec((nb, POOL_HALO, c), lambda i, j: (i, 0, 0)), _resident(w_blockdiag), _resident(scale)],
        out_specs=seq,
        out_shape=jax.ShapeDtypeStruct((n, t, c), F32),
        scratch_shapes=[pltpu.VMEM((nb, POOL_HALO, c), F32)],
        compiler_params=_params("parallel", "arbitrary"),
        name="pool_mixer",
    )(u, prefix, w_blockdiag, scale)


def _moba_prompt_kernel(q_ref, k_ref, vt_in_ref, o_ref, kmean_ref, vt_ref, kpos_ref, t_ref, *, n_blocks):
    qi = pl.program_id(1)
    tq = q_ref.shape[0]
    rows = HEADS_PER_TILE * tq
    dim = lax.broadcasted_iota(jnp.int32, (LANES, 1), 0)
    key = lax.broadcasted_iota(jnp.int32, (MOBA_BLOCK, rows), 0)
    col = lax.broadcasted_iota(jnp.int32, (1, rows), 1)
    blk = lax.broadcasted_iota(jnp.int32, (kmean_ref.shape[0], 1), 0)
    slopes = [jnp.where(col < tq, _alibi_slope(2 * p) * LOG2_E, _alibi_slope(2 * p + 1) * LOG2_E).astype(F32)
              for p in range(ATT_PAIRS)]

    @pl.when(qi == 0)
    def _():
        kmean_ref[...] = jnp.zeros(kmean_ref.shape, F32)
        ones = jnp.ones((BF16_SUBLANES, MOBA_BLOCK), BF16)
        for j in range(n_blocks):
            keys = slice(j * MOBA_BLOCK, (j + 1) * MOBA_BLOCK)
            kmean_ref[j:j + 1, :] = jnp.mean(k_ref[keys, :].astype(F32), axis=0, keepdims=True)
            for h in range(ATT_HEADS):
                vt_ref[j, h] = jnp.concatenate([vt_in_ref[h * HEAD_DIM:(h + 1) * HEAD_DIM, keys].astype(BF16), ones], axis=0)
        kpos_ref[...] = jnp.where(lax.broadcasted_iota(jnp.int32, kpos_ref.shape, 1) < SLOPE_PARTS,
                                  lax.broadcasted_iota(jnp.int32, kpos_ref.shape, 0), 0).astype(F32).astype(BF16)

    def tile(own):
        causal = key <= lax.rem(col, tq)
        for p in range(ATT_PAIRS):
            cols = slice(p * LANES, (p + 1) * LANES)
            q_t = (q_ref[:, cols] * (HEAD_DIM ** -0.5 * LOG2_E)).T
            qs_t = jnp.concatenate([jnp.where(dim < HEAD_DIM, q_t, 0.0), jnp.where(dim >= HEAD_DIM, q_t, 0.0)], axis=1)
            slope_rows, rest = jnp.zeros((LANES, rows), F32), slopes[p]
            for i in range(SLOPE_PARTS):
                piece = rest.astype(BF16).astype(F32)
                slope_rows, rest = jnp.where(dim == i, piece, slope_rows), rest - piece
            qs_b = jnp.concatenate([qs_t, slope_rows], axis=0).astype(BF16)
            sel = None
            if own > MOBA_TOPK:
                gate = _dot(kmean_ref[:, cols], qs_t, precision=lax.Precision.HIGHEST)
                sel = []
                for j in range(own):
                    gj = gate[j:j + 1, :]
                    ahead = jnp.where(blk < j, jnp.where(gate >= gj, 1.0, 0.0), jnp.where(gate > gj, 1.0, 0.0))
                    sel.append(jnp.sum(jnp.where(blk < own, ahead, 0.0), axis=0, keepdims=True) < MOBA_TOPK)
            block_bias = lambda j: slopes[p] * float((j - own) * MOBA_BLOCK)
            maxima = []
            for j in range(own + 1):
                k_blk = jnp.concatenate([k_ref[j * MOBA_BLOCK:(j + 1) * MOBA_BLOCK, cols], kpos_ref[...]], axis=1)
                t = _dot(k_blk, qs_b)
                if j == own:
                    t = jnp.where(causal, t, -jnp.inf)
                t_ref[p, j] = t
                m = jnp.max(t, axis=0, keepdims=True) + block_bias(j)
                maxima.append(m if sel is None or j == own else jnp.where(sel[j], m, -jnp.inf))
            m_all = functools.reduce(jnp.maximum, maxima)
            acc = [None] * HEADS_PER_TILE
            for j in range(own + 1):
                shift = m_all - block_bias(j)
                if sel is not None and j < own:
                    shift = jnp.where(sel[j], shift, jnp.inf)
                e = jnp.exp2(t_ref[p, j] - shift).astype(BF16)
                for h in range(HEADS_PER_TILE):
                    part = _dot(vt_ref[j, HEADS_PER_TILE * p + h], e[:, h * tq:(h + 1) * tq])
                    acc[h] = part if acc[h] is None else acc[h] + part
            out_t = jnp.concatenate([a[:HEAD_DIM] / a[HEAD_DIM:HEAD_DIM + 1] for a in acc], axis=0)
            o_ref[:, cols] = out_t.T

    for own in range(n_blocks):
        pl.when(qi == own)(functools.partial(tile, own))


def _moba_prompt(q, k, vt):
    n, t, c = q.shape
    tq = MOBA_BLOCK
    n_blocks = t // MOBA_BLOCK
    gate_rows = -(-n_blocks // SUBLANES) * SUBLANES
    rows = HEADS_PER_TILE * tq
    assert t % MOBA_BLOCK == 0 and c == ATT_DIM and vt.shape == (n, c, t)
    tile = pl.BlockSpec((None, tq, c), lambda i, j: (i, j, 0))
    return pl.pallas_call(
        functools.partial(_moba_prompt_kernel, n_blocks=n_blocks),
        grid=(n, n_blocks),
        in_specs=[tile, pl.BlockSpec((None, t, c), lambda i, j: (i, 0, 0)), pl.BlockSpec((None, c, t), lambda i, j: (i, 0, 0))],
        out_specs=tile,
        out_shape=jax.ShapeDtypeStruct((n, t, c), F32),
        scratch_shapes=[pltpu.VMEM((gate_rows, c), F32),
                        pltpu.VMEM((n_blocks, ATT_HEADS, HEAD_DIM + BF16_SUBLANES, MOBA_BLOCK), BF16),
                        pltpu.VMEM((MOBA_BLOCK, LANES), BF16),
                        pltpu.VMEM((ATT_PAIRS, n_blocks, MOBA_BLOCK, rows), F32)],
        compiler_params=_params("parallel", "arbitrary"),
        name="moba_prompt",
    )(q, k, vt)


def _head_rows(x):
    t = x.shape[0]
    lane_head = lax.broadcasted_iota(jnp.int32, (1, ATT_DIM), 1) // HEAD_DIM
    row_head = lax.broadcasted_iota(jnp.int32, (ATT_HEADS * t, 1), 0) // t
    return jnp.where(lane_head == row_head, jnp.concatenate([x] * ATT_HEADS, axis=0), 0.0)


def _own_head_columns(y, t):
    lane_head = lax.broadcasted_iota(jnp.int32, (1, ATT_DIM), 1) // HEAD_DIM
    out = jnp.zeros((t, ATT_DIM), F32)
    for h in range(ATT_HEADS):
        out = jnp.where(lane_head == h, y[h * t:(h + 1) * t], out)
    return out


def _moba_sample_kernel(pt_ref, q_ref, kn_ref, vn_ref, *refs, n_past_blocks):
    del pt_ref
    k_pages = refs[:PAGES_PER_STEP]
    v_pages = refs[PAGES_PER_STEP:2 * PAGES_PER_STEP]
    o_ref, m_ref, l_ref, g_ref, acc_ref = refs[2 * PAGES_PER_STEP:]
    c = pl.program_id(1)
    t = q_ref.shape[0]
    rows = ATT_HEADS * t
    pages_per_block = MOBA_BLOCK // PAGE_SIZE
    blocks_per_step = PAGES_PER_STEP // pages_per_block
    past_len = n_past_blocks * MOBA_BLOCK

    row_head = lax.broadcasted_iota(jnp.int32, (rows, 1), 0) // t
    slope = jnp.zeros((rows, 1), F32)
    for h in range(ATT_HEADS):
        slope = jnp.where(row_head == h, _alibi_slope(h), slope)
    q_f32 = _head_rows(q_ref[...] * HEAD_DIM ** -0.5)
    q_all = q_f32.astype(BF16)
    q_split = jnp.concatenate([q_all, (q_f32 - q_all.astype(F32)).astype(BF16)], axis=0)
    kcol = lax.broadcasted_iota(jnp.int32, (1, MOBA_BLOCK), 1)
    blk = lax.broadcasted_iota(jnp.int32, (1, LANES), 1)

    @pl.when(c == 0)
    def _():
        g_ref[...] = jnp.zeros(g_ref.shape, F32)
        m_ref[...] = jnp.full(m_ref.shape, -jnp.inf, F32)
        l_ref[...] = jnp.zeros(l_ref.shape, F32)

    raw = [_dot(q_split, k_ref[...].astype(BF16)) for k_ref in k_pages]
    raw = [r[:rows] + r[rows:] for r in raw]
    for b in range(blocks_per_step):
        j = c * blocks_per_step + b
        pages = range(b * pages_per_block, (b + 1) * pages_per_block)
        s = jnp.concatenate([raw[i] for i in pages], axis=1)
        g_ref[...] = jnp.where(blk == j, jnp.sum(s, axis=1, keepdims=True), g_ref[...])
        s = s + slope * (kcol + (j * MOBA_BLOCK - past_len)).astype(F32)
        m = jnp.max(s, axis=1, keepdims=True)
        e = jnp.exp(s - m)
        m_ref[...] = jnp.where(blk == j, m, m_ref[...])
        l_ref[...] = jnp.where(blk == j, jnp.sum(e, axis=1, keepdims=True), l_ref[...])
        e = e.astype(BF16)
        acc = None
        for n, i in enumerate(pages):
            part = _dot(e[:, n * PAGE_SIZE:(n + 1) * PAGE_SIZE], v_pages[i][...].astype(BF16), NT_DIMS)
            acc = part if acc is None else acc + part
        acc_ref[j] = acc

    @pl.when(c == pl.num_programs(1) - 1)
    def _():
        pad = jnp.zeros((PAGE_SIZE - t, ATT_DIM), F32)
        k_new = jnp.concatenate([kn_ref[...], pad], axis=0).astype(BF16)
        v_new = jnp.concatenate([vn_ref[...], pad], axis=0).astype(BF16)
        ncol = lax.broadcasted_iota(jnp.int32, (1, PAGE_SIZE), 1)
        qrow = lax.rem(lax.broadcasted_iota(jnp.int32, (rows, 1), 0), t)
        s = _dot(q_all, k_new, NT_DIMS) + slope * ncol.astype(F32)
        s = jnp.where(ncol <= qrow, s, -jnp.inf)
        m = jnp.max(s, axis=1, keepdims=True)
        e = jnp.exp(s - m)
        blk_f = blk.astype(F32)
        gate = jnp.where(blk < n_past_blocks, g_ref[...], -jnp.inf)
        picked = jnp.zeros(gate.shape, F32)
        for _ in range(MOBA_TOPK):
            top = jnp.max(gate, axis=1, keepdims=True)
            first = jnp.min(jnp.where(gate == top, blk_f, float(LANES)), axis=1, keepdims=True)
            picked = jnp.where(blk_f == first, 1.0, picked)
            gate = jnp.where(blk_f == first, -jnp.inf, gate)
        m_all = jnp.maximum(m, jnp.max(jnp.where(picked > 0.0, m_ref[...], -jnp.inf), axis=1, keepdims=True))
        w = jnp.where(picked > 0.0, jnp.exp(m_ref[...] - m_all), 0.0)
        w_new = jnp.exp(m - m_all)
        den = w_new * jnp.sum(e, axis=1, keepdims=True) + jnp.sum(w * l_ref[...], axis=1, keepdims=True)
        parts = [w_new * _dot(e.astype(BF16), v_new)] + [w[:, j:j + 1] * acc_ref[j] for j in range(n_past_blocks)]
        while len(parts) > 1:
            parts = [a + b for a, b in zip(parts[::2], parts[1::2])] + parts[len(parts) - len(parts) % 2:]
        o_ref[...] = _own_head_columns(parts[0] / den, t)


def _moba_sample(q, k_new, v_new, cache_kt, cache_vt, page_ids):
    n, t, c = q.shape
    pages_per_seq = page_ids.shape[0] // n
    past_len = pages_per_seq * PAGE_SIZE
    n_past_blocks = past_len // MOBA_BLOCK
    assert past_len % MOBA_BLOCK == 0 and t <= PAGE_SIZE and t % SUBLANES == 0 and c == ATT_DIM
    assert pages_per_seq % PAGES_PER_STEP == 0 and MOBA_TOPK <= n_past_blocks <= LANES
    new = pl.BlockSpec((None, t, c), lambda i, j, pt: (i, 0, 0))

    def page(slot):
        return pl.BlockSpec((None, c, PAGE_SIZE),
                            lambda i, j, pt: (pt[i * pages_per_seq + j * PAGES_PER_STEP + slot], 0, 0))

    pages = [page(s) for s in range(PAGES_PER_STEP)]
    rows = ATT_HEADS * t
    stat = pltpu.VMEM((rows, LANES), F32)
    return pl.pallas_call(
        functools.partial(_moba_sample_kernel, n_past_blocks=n_past_blocks),
        grid_spec=pltpu.PrefetchScalarGridSpec(
            num_scalar_prefetch=1,
            grid=(n, pages_per_seq // PAGES_PER_STEP),
            in_specs=[new, new, new] + pages + pages,
            out_specs=new,
            scratch_shapes=[stat, stat, stat, pltpu.VMEM((n_past_blocks, rows, c), F32)],
        ),
        out_shape=jax.ShapeDtypeStruct((n, t, c), F32),
        compiler_params=_params("parallel", "arbitrary"),
        name="moba_sample",
    )(page_ids, q, k_new, v_new, *([cache_kt] * PAGES_PER_STEP), *([cache_vt] * PAGES_PER_STEP))


def _retention_constants(c):
    lg = np.log1p(-np.exp2(-5.0 - np.arange(RET_HEADS, dtype=np.float64)))
    idx = np.arange(c, dtype=np.float64)
    diff = idx[:, None] - idx[None, :]
    d_in = np.where(diff >= 0, np.exp(lg[:, None, None] * np.maximum(diff, 0.0)), 0.0)
    q_dec = np.exp(lg[:, None] * (idx + 1.0))
    k_dec = np.exp(lg[:, None] * (c - 1.0 - idx))
    c_dec = np.exp(lg * c)
    per_lane = lambda a: np.repeat(a.reshape(RET_PAIRS, HEADS_PER_TILE, -1).transpose(0, 2, 1), HEAD_DIM, axis=2)
    block = np.kron(np.eye(HEADS_PER_TILE), np.ones((HEAD_DIM, HEAD_DIM)))
    state_dec = block[None] * np.repeat(c_dec.reshape(RET_PAIRS, HEADS_PER_TILE), HEAD_DIM, axis=1)[:, :, None]
    as_f32 = lambda a: jnp.asarray(a, F32)
    return (as_f32(d_in.reshape(RET_PAIRS, HEADS_PER_TILE * c, c)), as_f32(per_lane(q_dec)), as_f32(per_lane(k_dec)),
            as_f32(state_dec), as_f32(block))


def _retention_kernel(*refs, has_state):
    if has_state:
        q_ref, k_ref, v_ref, g_ref, s0_ref, din_ref, qd_ref, kd_ref, sd_ref, blk_ref, y_ref, sout_ref, s_ref = refs
    else:
        q_ref, k_ref, v_ref, g_ref, din_ref, qd_ref, kd_ref, sd_ref, blk_ref, y_ref, sout_ref, s_ref = refs
    ci = pl.program_id(1)
    n_seqs, c, _ = q_ref.shape

    @pl.when(ci == 0)
    def _():
        if not has_state:
            s_ref[...] = jnp.zeros(s_ref.shape, F32)
            return
        zero = jnp.zeros((HEAD_DIM, HEAD_DIM), F32)
        for b in range(n_seqs):
            for p in range(RET_PAIRS):
                even, odd = s0_ref[b, 2 * p].astype(F32), s0_ref[b, 2 * p + 1].astype(F32)
                s_ref[b, p] = jnp.concatenate([jnp.concatenate([even, zero], axis=1),
                                               jnp.concatenate([zero, odd], axis=1)], axis=0)

    lane = lax.broadcasted_iota(jnp.int32, (c, LANES), 1)
    for b in range(n_seqs):
        for p in range(RET_PAIRS):
            cols = slice(p * LANES, (p + 1) * LANES)
            q = q_ref[b, :, cols]
            k = k_ref[b, :, cols] * HEAD_DIM ** -0.5
            v = v_ref[b, :, cols].astype(BF16)
            state = s_ref[b, p]
            att = _dot(_stack_pair(q, lane).astype(BF16), k.astype(BF16), NT_DIMS) * din_ref[p]
            o = _unstack_pair(_dot(att.astype(BF16), v), lane)
            o = o + _dot((q * qd_ref[p]).astype(BF16), state.astype(BF16))
            s_ref[b, p] = state * sd_ref[p] + _dot((k * kd_ref[p]).astype(BF16), v, TN_DIMS) * blk_ref[...]
            sq = o * o
            even = jnp.sum(jnp.where(lane < HEAD_DIM, sq, 0.0), axis=1, keepdims=True)
            odd = jnp.sum(jnp.where(lane >= HEAD_DIM, sq, 0.0), axis=1, keepdims=True)
            ms = jnp.where(lane < HEAD_DIM, even, odd) * (1.0 / HEAD_DIM)
            y_ref[b, :, cols] = _silu(g_ref[b, :, cols]) * (o * lax.rsqrt(ms + EPS))

    @pl.when(ci == pl.num_programs(1) - 1)
    def _():
        for b in range(n_seqs):
            for p in range(RET_PAIRS):
                state = s_ref[b, p]
                sout_ref[b, 2 * p] = state[:HEAD_DIM, :HEAD_DIM]
                sout_ref[b, 2 * p + 1] = state[HEAD_DIM:, HEAD_DIM:]


def _retention(q, k, v, g, state):
    n, t, d = q.shape
    c = math.gcd(t, RET_CHUNK)
    nb = math.gcd(n, RET_SEQS)
    consts = _retention_constants(c)
    chunk = pl.BlockSpec((nb, c, d), lambda i, j: (i, j, 0))
    st = pl.BlockSpec((nb, RET_HEADS, HEAD_DIM, HEAD_DIM), lambda i, j: (i, 0, 0, 0))
    in_specs, args = [chunk] * 4, [q, k, v, g]
    if state is not None:
        in_specs.append(st)
        args.append(state)
    in_specs += [_resident(a) for a in consts]
    args += list(consts)
    return pl.pallas_call(
        functools.partial(_retention_kernel, has_state=state is not None),
        grid=(n // nb, t // c),
        in_specs=in_specs,
        out_specs=[chunk, st],
        out_shape=[jax.ShapeDtypeStruct((n, t, d), F32), jax.ShapeDtypeStruct((n, RET_HEADS, HEAD_DIM, HEAD_DIM), F32)],
        scratch_shapes=[pltpu.VMEM((nb, RET_PAIRS, LANES, LANES), F32)],
        compiler_params=_params("parallel", "arbitrary"),
        name="retention",
    )(*args)


def _layer(x, n, pool_prefix, ret_state, moba, w, final_norm, final):
    t = x.shape[0] // n
    z = _dense_in(x, w["g1"], w["f1i"], w["f1o"], w["gm"], w["win"],
                  prompt=(t, w["pool_w"], w["pool_scale"]) if moba is None else None)
    seq = lambda a: a.reshape(n, t, a.shape[-1])
    if moba is None:
        assert t >= POOL_PAST
        y_pool, pool_new = z["y_pool"], z["pool_tail"][:, POOL_HALO - POOL_PAST:]
    else:
        u = seq(z["u"])
        halo = jnp.pad(pool_prefix.astype(F32), ((0, 0), (POOL_HALO - POOL_PAST, 0), (0, 0)))
        y_pool = _pool(u, halo, w["pool_w"], w["pool_scale"], moba["past_len"])
        pool_new = jnp.concatenate([pool_prefix.astype(F32), u], axis=1)[:, -POOL_PAST:]
    if moba is None:
        y_att = _moba_prompt(seq(z["qa"]), seq(z["ka"]), z["vt"])
        heads = lambda a: jnp.transpose(a.reshape(n, ATT_HEADS, HEAD_DIM, t), (0, 3, 1, 2))
        k_new, v_new = heads(z["kt"]), heads(z["vt"])
    else:
        y_att = _moba_sample(seq(z["qa"]), seq(z["ka"]), seq(z["va"]), moba["cache_k"], moba["cache_v"], moba["page_ids"])
        heads = lambda a: a.reshape(n, t, ATT_HEADS, HEAD_DIM)
        k_new, v_new = heads(z["ka"]), heads(z["va"])
    y_ret, ret_new = _retention(seq(z["qr"]), seq(z["kr"]), seq(z["vr"]), seq(z["gr"]), ret_state)
    flat = lambda a: a.reshape(n * t, a.shape[-1])
    x2 = _dense_out(z["x1"], flat(y_pool), flat(y_att), flat(y_ret), w["wo"], w["g2"], w["f2i"], w["f2o"], final_norm,
                    final)
    return x2, k_new, v_new, pool_new, ret_new


def kernel(x_prompt, x_sample, cache_k, cache_v, state_pool, state_ret, page_table, norm_ffn1, ffn1_w_in, ffn1_w_out,
           norm_mix, w_in, pool_w, pool_scale, w_out, norm_ffn2, ffn2_w_in, ffn2_w_out, norm_final):
    depth = w_in.shape[0]
    n_p, t_p, d = x_prompt.shape
    n_s, t_s, _ = x_sample.shape
    n_pool = cache_k.shape[1]
    pages_per_seq = page_table.shape[1]
    assert cache_k.shape[2] == PAGE_SIZE and d % LANES == 0
    pages_t = lambda a: jnp.transpose(a, (0, 1, 3, 4, 2)).reshape(depth * n_pool, ATT_DIM, PAGE_SIZE)
    cache_k, cache_v = pages_t(cache_k), pages_t(cache_v)
    row = lambda a: a.reshape(1, -1).astype(F32)
    final_norm = row(norm_final)
    hp, hs = x_prompt.reshape(n_p * t_p, d), x_sample.reshape(n_s * t_s, d)
    outs = [[] for _ in range(8)]
    for l in range(depth):
        w = dict(g1=row(norm_ffn1[l]), f1i=ffn1_w_in[l].astype(BF16), f1o=ffn1_w_out[l].astype(BF16),
                 gm=row(norm_mix[l]), win=w_in[l].astype(BF16),
                 pool_w=jax.scipy.linalg.block_diag(*pool_w[l]).astype(BF16), pool_scale=row(pool_scale[l]),
                 wo=w_out[l].astype(BF16), g2=row(norm_ffn2[l]), f2i=ffn2_w_in[l].astype(BF16),
                 f2o=ffn2_w_out[l].astype(BF16))
        final = l == depth - 1
        hp, kp, vp, pp, rp = _layer(hp, n_p, None, None, None, w, final_norm, final)
        moba = dict(cache_k=cache_k, cache_v=cache_v, past_len=pages_per_seq * PAGE_SIZE,
                    page_ids=(page_table.astype(jnp.int32) + l * n_pool).reshape(-1))
        hs, ks, vs, ps, rs = _layer(hs, n_s, state_pool[l], state_ret[l], moba, w, final_norm, final)
        for lst, a in zip(outs, (kp, vp, pp, rp, ks, vs, ps, rs)):
            lst.append(a)
    return (hp.reshape(n_p, t_p, d), hs.reshape(n_s, t_s, d)) + tuple(jnp.stack(lst) for lst in outs)
```

```python
import functools
import math
from typing import NamedTuple

import numpy as np
import jax
import jax.numpy as jnp
from jax import lax
from jax.experimental import pallas as pl
from jax.experimental.pallas import tpu as pltpu

F32 = jnp.float32
BF16 = jnp.bfloat16

HEAD_DIM = 64
POOL_WINDOWS = (2, 4, 8, 16)
POOL_GRP = HEAD_DIM
POOL_DIM = len(POOL_WINDOWS) * POOL_GRP
POOL_PAST = max(POOL_WINDOWS) - 1
ATT_HEADS = 6
ATT_DIM = ATT_HEADS * HEAD_DIM
RET_HEADS = 6
RET_DIM = RET_HEADS * HEAD_DIM
MOBA_BLOCK = 256
MOBA_TOPK = 3
PAGE_SIZE = 128
RET_CHUNK = 128
EPS = 1e-6

LANES = 128
SUBLANES = 8
BF16_SUBLANES = 16
SLOPE_PARTS = 3
HEADS_PER_TILE = LANES // HEAD_DIM
ATT_PAIRS = ATT_DIM // LANES
RET_PAIRS = RET_DIM // LANES
VMEM_LIMIT = 56 * 1024 * 1024

TOKEN_TILE = 512
FF_CHUNK = 256
POOL_TILE = 512
POOL_HALO = 16
LOG2_E = math.log2(math.e)
PAGES_PER_STEP = 16
RET_SEQS = 8
NT_DIMS = (((1,), (1,)), ((), ()))
TN_DIMS = (((0,), (0,)), ((), ()))


def _params(*semantics):
    return pltpu.CompilerParams(dimension_semantics=semantics, vmem_limit_bytes=VMEM_LIMIT)


class _Stacked(NamedTuple):
    stack: jax.Array
    layer: int

    @property
    def shape(self):
        return self.stack.shape[1:]

    def columns(self, lo, hi):
        return self.stack[self.layer, :, lo:hi]


def _resident(a):
    if isinstance(a, _Stacked):
        return pl.BlockSpec((None,) + a.shape, lambda *_: (a.layer,) + (0,) * len(a.shape), pipeline_mode=pl.Buffered(1))
    return pl.BlockSpec(a.shape, lambda *_: (0,) * a.ndim, pipeline_mode=pl.Buffered(1))


def _operand(a):
    return a.stack if isinstance(a, _Stacked) else a


def _rmsnorm(x, g):
    return x * lax.rsqrt(jnp.mean(x * x, axis=-1, keepdims=True) + EPS) * g


def _silu(x):
    return x * (1.0 / (1.0 + jnp.exp(-x)))


def _dot(a, b, dims=None, **kw):
    if dims is None:
        return jnp.dot(a, b, preferred_element_type=F32, **kw)
    return lax.dot_general(a, b, dims, preferred_element_type=F32, **kw)


def _stack_pair(x, lane):
    return jnp.concatenate([jnp.where(lane < HEAD_DIM, x, 0.0), jnp.where(lane >= HEAD_DIM, x, 0.0)], axis=0)


def _unstack_pair(y, lane):
    r = y.shape[0] // 2
    return jnp.where(lane < HEAD_DIM, y[:r], y[r:])


def _alibi_slope(h):
    return 2.0 ** (-8.0 * (h + 1) / ATT_HEADS)


def _swiglu_half_step(x, g_ref, wi_ref, wo_ref):
    d_ff = wo_ref.shape[0]
    h = _rmsnorm(x, g_ref[...]).astype(BF16)
    acc = jnp.zeros(x.shape, F32)
    for c in range(d_ff // FF_CHUNK):
        lo = c * FF_CHUNK
        gate = _dot(h, wi_ref[:, lo:lo + FF_CHUNK])
        up = _dot(h, wi_ref[:, d_ff + lo:d_ff + lo + FF_CHUNK])
        acc = acc + _dot((_silu(gate) * up).astype(BF16), wo_ref[lo:lo + FF_CHUNK, :])
    return x + 0.5 * acc


def _pool_tile(u, carry, pos, w_blockdiag, scale):
    tt = u.shape[0]
    ext = jnp.concatenate([carry, u], axis=0)
    sums, s = [], ext
    for w in POOL_WINDOWS:
        s = s + pltpu.roll(s, w // 2, axis=0)
        sums.append(s[POOL_HALO:])
    lane = lax.broadcasted_iota(jnp.int32, (1, POOL_DIM), 1)
    pooled = None
    for g, w in reversed(list(enumerate(POOL_WINDOWS))):
        mean = sums[g] * (1.0 / jnp.minimum(w, pos + 1).astype(F32))
        pooled = mean if pooled is None else jnp.where(lane < (g + 1) * POOL_GRP, mean, pooled)
    d = (pooled - u).astype(BF16)
    return _dot(d, w_blockdiag) * scale, ext[tt:]


def _dense_in_kernel(x_ref, g1_ref, f1i_ref, f1o_ref, gm_ref, win_ref, *refs, col_offsets, seq_tiles):
    if seq_tiles:
        (wt_ref, pw_ref, ps_ref), refs = refs[:3], refs[3:]
        (kt_ref, vt_ref, ypool_ref, tail_ref, carry_ref), refs = refs[-5:], refs[:-5]
    x1_ref, z_refs = refs[0], refs[1:]
    x1 = _swiglu_half_step(x_ref[...], g1_ref, f1i_ref, f1o_ref)
    x1_ref[...] = x1
    h = _rmsnorm(x1, gm_ref[...]).astype(BF16)
    for z_ref, off in zip(z_refs, col_offsets):
        z_ref[...] = _dot(h, win_ref[:, off:off + z_ref.shape[-1]]).astype(z_ref.dtype)
    if seq_tiles:
        zt = _dot(wt_ref[...], h, NT_DIMS)
        kt_ref[...] = zt[:ATT_DIM]
        vt_ref[...] = zt[ATT_DIM:]
        tile = lax.rem(pl.program_id(0), seq_tiles)
        tm = x_ref.shape[0]

        @pl.when(tile == 0)
        def _():
            carry_ref[...] = jnp.zeros(carry_ref.shape, F32)

        u = _dot(h, win_ref[:, Z_COLUMNS["u"][0]:sum(Z_COLUMNS["u"])])
        pos = tile * tm + lax.broadcasted_iota(jnp.int32, (tm, 1), 0)
        y, carry = _pool_tile(u, carry_ref[...], pos, pw_ref[...], ps_ref[...])
        ypool_ref[...] = y
        carry_ref[...] = carry
        tail_ref[...] = carry


Z_COLUMNS = dict(zip(("u", "qa", "ka", "va", "qr", "kr", "vr", "gr"),
                     ((0, POOL_DIM), (POOL_DIM, ATT_DIM), (POOL_DIM + ATT_DIM, ATT_DIM), (POOL_DIM + 2 * ATT_DIM, ATT_DIM))
                     + tuple((POOL_DIM + 3 * ATT_DIM + i * RET_DIM, RET_DIM) for i in range(4))))


def _dense_in(x, g1, f1i, f1o, gm, win, prompt=None):
    t, d = x.shape
    tm = min(TOKEN_TILE, t)
    assert t % tm == 0 and win.shape[1] == sum(w for _, w in Z_COLUMNS.values()) and f1o.shape[0] % FF_CHUNK == 0
    rows = lambda w: pl.BlockSpec((tm, w), lambda i: (i, 0))
    names = [k for k in Z_COLUMNS if not (prompt and k in ("u", "va"))]
    dtypes = [BF16 if (prompt and k == "ka") else F32 for k in names]
    in_specs = [rows(d)] + [_resident(a) for a in (g1, f1i, f1o, gm, win)]
    args = [x, g1, f1i, f1o, gm, win]
    out_specs = [rows(d)] + [rows(Z_COLUMNS[k][1]) for k in names]
    out_shape = [jax.ShapeDtypeStruct((t, d), F32)] + [jax.ShapeDtypeStruct((t, Z_COLUMNS[k][1]), dt)
                                                      for k, dt in zip(names, dtypes)]
    col_offsets = tuple(Z_COLUMNS[k][0] for k in names)
    scratch, tiles = [], 0
    if prompt:
        seq_len, pool_w, pool_scale = prompt
        assert seq_len % tm == 0 and t % seq_len == 0 and Z_COLUMNS["va"][0] == sum(Z_COLUMNS["ka"])
        tiles = seq_len // tm
        kv_t = win.columns(Z_COLUMNS["ka"][0], sum(Z_COLUMNS["va"])).T
        in_specs += [_resident(a) for a in (kv_t, pool_w, pool_scale)]
        args += [kv_t, pool_w, pool_scale]
        out_specs += [pl.BlockSpec((None, ATT_DIM, tm), lambda i: (i // tiles, 0, i % tiles))] * 2
        out_specs += [rows(POOL_DIM), pl.BlockSpec((None, POOL_HALO, POOL_DIM), lambda i: (i // tiles, 0, 0))]
        out_shape += [jax.ShapeDtypeStruct((t // seq_len, ATT_DIM, seq_len), F32)] * 2
        out_shape += [jax.ShapeDtypeStruct((t, POOL_DIM), F32), jax.ShapeDtypeStruct((t // seq_len, POOL_HALO, POOL_DIM), F32)]
        names += ["kt", "vt", "y_pool", "pool_tail"]
        scratch = [pltpu.VMEM((POOL_HALO, POOL_DIM), F32)]
    outs = pl.pallas_call(
        functools.partial(_dense_in_kernel, col_offsets=col_offsets, seq_tiles=tiles),
        grid=(t // tm,),
        in_specs=in_specs,
        out_specs=out_specs,
        out_shape=out_shape,
        scratch_shapes=scratch,
        compiler_params=_params("arbitrary" if prompt else "parallel"),
        name="dense_in",
    )(*[_operand(a) for a in args])
    return dict(zip(["x1"] + names, outs))


def _dense_out_kernel(x_ref, yp_ref, ya_ref, yr_ref, wo_ref, g2_ref, f2i_ref, f2o_ref, gf_ref, *refs, final, n_gather):
    gather_refs, o_ref, gathered_refs = refs[:n_gather], refs[n_gather], refs[n_gather + 1:]
    off, mix = 0, None
    for y_ref in (yp_ref, ya_ref, yr_ref):
        w = y_ref.shape[-1]
        part = _dot(y_ref[...].astype(BF16), wo_ref[off:off + w, :])
        mix = part if mix is None else mix + part
        off += w
    x = _swiglu_half_step(x_ref[...] + mix, g2_ref, f2i_ref, f2o_ref)
    if final:
        x = _rmsnorm(x, gf_ref[...])
    o_ref[...] = x
    per_out = n_gather // max(len(gathered_refs), 1)
    for k, out_ref in enumerate(gathered_refs):
        for l in range(per_out):
            out_ref[l] = gather_refs[k * per_out + l][...]


def _dense_out(x, yp, ya, yr, wo, g2, f2i, f2o, gf, final, stack_layers=()):
    t, d = x.shape
    tm = min(TOKEN_TILE, t)
    assert t % tm == 0
    rows = lambda w: pl.BlockSpec((tm, w), lambda i: (i, 0))
    in_specs = [rows(d), rows(yp.shape[1]), rows(ya.shape[1]), rows(yr.shape[1])] + [_resident(a) for a in (wo, g2, f2i, f2o, gf)]
    args = [x, yp, ya, yr, wo, g2, f2i, f2o, gf]
    out_specs, out_shape = [rows(d)], [jax.ShapeDtypeStruct((t, d), F32)]
    for group in stack_layers:
        n, c, seq_len = group[0].shape
        tiles = seq_len // tm
        assert seq_len % tm == 0 and n * seq_len == t and all(a.shape == group[0].shape for a in group)
        in_specs += [pl.BlockSpec((None, c, tm), lambda i: (i // tiles, 0, i % tiles))] * len(group)
        args += list(group)
        out_specs.append(pl.BlockSpec((len(group), None, c, tm), lambda i: (0, i // tiles, 0, i % tiles)))
        out_shape.append(jax.ShapeDtypeStruct((len(group), n, c, seq_len), group[0].dtype))
    outs = pl.pallas_call(
        functools.partial(_dense_out_kernel, final=final, n_gather=sum(len(g) for g in stack_layers)),
        grid=(t // tm,),
        in_specs=in_specs,
        out_specs=out_specs,
        out_shape=out_shape,
        compiler_params=_params("parallel"),
        name="dense_out",
    )(*[_operand(a) for a in args])
    return outs[0], tuple(outs[1:])


def _pool_kernel(u_ref, pre_ref, w_ref, scale_ref, y_ref, carry_ref, *, tt, pos0):
    t = pl.program_id(1)

    @pl.when(t == 0)
    def _():
        carry_ref[...] = pre_ref[...]

    pos = pos0 + t * tt + lax.broadcasted_iota(jnp.int32, (tt, 1), 0)
    for b in range(u_ref.shape[0]):
        y_ref[b], carry_ref[b] = _pool_tile(u_ref[b], carry_ref[b], pos, w_ref[...], scale_ref[...])


def _pool(u, prefix, w_blockdiag, scale, pos0):
    n, t, c = u.shape
    tt = min(POOL_TILE, t)
    nb = math.gcd(n, POOL_TILE // tt)
    assert t % tt == 0 and tt % SUBLANES == 0 and POOL_WINDOWS == (2, 4, 8, 16)
    seq = pl.BlockSpec((nb, tt, c), lambda i, j: (i, j, 0))
    return pl.pallas_call(
        functools.partial(_pool_kernel, tt=tt, pos0=pos0),
        grid=(n // nb, t // tt),
        in_specs=[seq, pl.BlockSpec((nb, POOL_HALO, c), lambda i, j: (i, 0, 0)), _resident(w_blockdiag), _resident(scale)],
        out_specs=seq,
        out_shape=jax.ShapeDtypeStruct((n, t, c), F32),
        scratch_shapes=[pltpu.VMEM((nb, POOL_HALO, c), F32)],
        compiler_params=_params("parallel", "arbitrary"),
        name="pool_mixer",
    )(u, prefix, w_blockdiag, scale)


def _moba_prompt_kernel(q_ref, k_ref, vt_in_ref, o_ref, kmean_ref, vt_ref, kpos_ref, t_ref, *, n_blocks):
    qi = pl.program_id(1)
    tq = q_ref.shape[0]
    rows = HEADS_PER_TILE * tq
    dim = lax.broadcasted_iota(jnp.int32, (LANES, 1), 0)
    key = lax.broadcasted_iota(jnp.int32, (MOBA_BLOCK, rows), 0)
    col = lax.broadcasted_iota(jnp.int32, (1, rows), 1)
    blk = lax.broadcasted_iota(jnp.int32, (kmean_ref.shape[0], 1), 0)
    slopes = [jnp.where(col < tq, _alibi_slope(2 * p) * LOG2_E, _alibi_slope(2 * p + 1) * LOG2_E).astype(F32)
              for p in range(ATT_PAIRS)]

    @pl.when(qi == 0)
    def _():
        kmean_ref[...] = jnp.zeros(kmean_ref.shape, F32)
        ones = jnp.ones((BF16_SUBLANES, MOBA_BLOCK), BF16)
        for j in range(n_blocks):
            keys = slice(j * MOBA_BLOCK, (j + 1) * MOBA_BLOCK)
            kmean_ref[j:j + 1, :] = jnp.mean(k_ref[keys, :].astype(F32), axis=0, keepdims=True)
            for h in range(ATT_HEADS):
                vt_ref[j, h] = jnp.concatenate([vt_in_ref[h * HEAD_DIM:(h + 1) * HEAD_DIM, keys].astype(BF16), ones], axis=0)
        kpos_ref[...] = jnp.where(lax.broadcasted_iota(jnp.int32, kpos_ref.shape, 1) < SLOPE_PARTS,
                                  lax.broadcasted_iota(jnp.int32, kpos_ref.shape, 0), 0).astype(F32).astype(BF16)

    def tile(own):
        causal = key <= lax.rem(col, tq)
        for p in range(ATT_PAIRS):
            cols = slice(p * LANES, (p + 1) * LANES)
            q_t = (q_ref[:, cols] * (HEAD_DIM ** -0.5 * LOG2_E)).T
            qs_t = jnp.concatenate([jnp.where(dim < HEAD_DIM, q_t, 0.0), jnp.where(dim >= HEAD_DIM, q_t, 0.0)], axis=1)
            slope_rows, rest = jnp.zeros((LANES, rows), F32), slopes[p]
            for i in range(SLOPE_PARTS):
                piece = rest.astype(BF16).astype(F32)
                slope_rows, rest = jnp.where(dim == i, piece, slope_rows), rest - piece
            qs_b = jnp.concatenate([qs_t, slope_rows], axis=0).astype(BF16)
            sel = None
            if own > MOBA_TOPK:
                gate = _dot(kmean_ref[:, cols], qs_t, precision=lax.Precision.HIGHEST)
                sel = []
                for j in range(own):
                    gj = gate[j:j + 1, :]
                    ahead = jnp.where(blk < j, jnp.where(gate >= gj, 1.0, 0.0), jnp.where(gate > gj, 1.0, 0.0))
                    sel.append(jnp.sum(jnp.where(blk < own, ahead, 0.0), axis=0, keepdims=True) < MOBA_TOPK)
            block_bias = lambda j: slopes[p] * float((j - own) * MOBA_BLOCK)
            maxima = []
            for j in range(own + 1):
                k_blk = jnp.concatenate([k_ref[j * MOBA_BLOCK:(j + 1) * MOBA_BLOCK, cols], kpos_ref[...]], axis=1)
                t = _dot(k_blk, qs_b)
                if j == own:
                    t = jnp.where(causal, t, -jnp.inf)
                t_ref[p, j] = t
                m = jnp.max(t, axis=0, keepdims=True) + block_bias(j)
                maxima.append(m if sel is None or j == own else jnp.where(sel[j], m, -jnp.inf))
            m_all = functools.reduce(jnp.maximum, maxima)
            acc = [None] * HEADS_PER_TILE
            for j in range(own + 1):
                shift = m_all - block_bias(j)
                if sel is not None and j < own:
                    shift = jnp.where(sel[j], shift, jnp.inf)
                e = jnp.exp2(t_ref[p, j] - shift).astype(BF16)
                for h in range(HEADS_PER_TILE):
                    part = _dot(vt_ref[j, HEADS_PER_TILE * p + h], e[:, h * tq:(h + 1) * tq])
                    acc[h] = part if acc[h] is None else acc[h] + part
            out_t = jnp.concatenate([a[:HEAD_DIM] / a[HEAD_DIM:HEAD_DIM + 1] for a in acc], axis=0)
            o_ref[:, cols] = out_t.T

    for own in range(n_blocks):
        pl.when(qi == own)(functools.partial(tile, own))


def _moba_prompt(q, k, vt):
    n, t, c = q.shape
    tq = MOBA_BLOCK
    n_blocks = t // MOBA_BLOCK
    gate_rows = -(-n_blocks // SUBLANES) * SUBLANES
    rows = HEADS_PER_TILE * tq
    assert t % MOBA_BLOCK == 0 and c == ATT_DIM and vt.shape == (n, c, t)
    tile = pl.BlockSpec((None, tq, c), lambda i, j: (i, j, 0))
    return pl.pallas_call(
        functools.partial(_moba_prompt_kernel, n_blocks=n_blocks),
        grid=(n, n_blocks),
        in_specs=[tile, pl.BlockSpec((None, t, c), lambda i, j: (i, 0, 0)), pl.BlockSpec((None, c, t), lambda i, j: (i, 0, 0))],
        out_specs=tile,
        out_shape=jax.ShapeDtypeStruct((n, t, c), F32),
        scratch_shapes=[pltpu.VMEM((gate_rows, c), F32),
                        pltpu.VMEM((n_blocks, ATT_HEADS, HEAD_DIM + BF16_SUBLANES, MOBA_BLOCK), BF16),
                        pltpu.VMEM((MOBA_BLOCK, LANES), BF16),
                        pltpu.VMEM((ATT_PAIRS, n_blocks, MOBA_BLOCK, rows), F32)],
        compiler_params=_params("parallel", "arbitrary"),
        name="moba_prompt",
    )(q, k, vt)


def _head_rows(x):
    t = x.shape[0]
    lane_head = lax.broadcasted_iota(jnp.int32, (1, ATT_DIM), 1) // HEAD_DIM
    row_head = lax.broadcasted_iota(jnp.int32, (ATT_HEADS * t, 1), 0) // t
    return jnp.where(lane_head == row_head, jnp.concatenate([x] * ATT_HEADS, axis=0), 0.0)


def _own_head_columns(y, t):
    lane_head = lax.broadcasted_iota(jnp.int32, (1, ATT_DIM), 1) // HEAD_DIM
    out = jnp.zeros((t, ATT_DIM), F32)
    for h in range(ATT_HEADS):
        out = jnp.where(lane_head == h, y[h * t:(h + 1) * t], out)
    return out


def _moba_sample_kernel(pt_ref, q_ref, kn_ref, vn_ref, *refs, n_past_blocks):
    del pt_ref
    k_pages = refs[:PAGES_PER_STEP]
    v_pages = refs[PAGES_PER_STEP:2 * PAGES_PER_STEP]
    o_ref, m_ref, l_ref, g_ref, acc_ref = refs[2 * PAGES_PER_STEP:]
    c = pl.program_id(1)
    t = q_ref.shape[0]
    rows = ATT_HEADS * t
    pages_per_block = MOBA_BLOCK // PAGE_SIZE
    blocks_per_step = PAGES_PER_STEP // pages_per_block
    past_len = n_past_blocks * MOBA_BLOCK

    row_head = lax.broadcasted_iota(jnp.int32, (rows, 1), 0) // t
    slope = jnp.zeros((rows, 1), F32)
    for h in range(ATT_HEADS):
        slope = jnp.where(row_head == h, _alibi_slope(h), slope)
    q_f32 = _head_rows(q_ref[...] * HEAD_DIM ** -0.5)
    q_all = q_f32.astype(BF16)
    q_split = jnp.concatenate([q_all, (q_f32 - q_all.astype(F32)).astype(BF16)], axis=0)
    kcol = lax.broadcasted_iota(jnp.int32, (1, MOBA_BLOCK), 1)
    blk = lax.broadcasted_iota(jnp.int32, (1, LANES), 1)

    @pl.when(c == 0)
    def _():
        g_ref[...] = jnp.zeros(g_ref.shape, F32)
        m_ref[...] = jnp.full(m_ref.shape, -jnp.inf, F32)
        l_ref[...] = jnp.zeros(l_ref.shape, F32)

    raw = [_dot(q_split, k_ref[...].astype(BF16)) for k_ref in k_pages]
    raw = [r[:rows] + r[rows:] for r in raw]
    for b in range(blocks_per_step):
        j = c * blocks_per_step + b
        pages = range(b * pages_per_block, (b + 1) * pages_per_block)
        s = jnp.concatenate([raw[i] for i in pages], axis=1)
        g_ref[...] = jnp.where(blk == j, jnp.sum(s, axis=1, keepdims=True), g_ref[...])
        s = s + slope * (kcol + (j * MOBA_BLOCK - past_len)).astype(F32)
        m = jnp.max(s, axis=1, keepdims=True)
        e = jnp.exp(s - m)
        m_ref[...] = jnp.where(blk == j, m, m_ref[...])
        l_ref[...] = jnp.where(blk == j, jnp.sum(e, axis=1, keepdims=True), l_ref[...])
        e = e.astype(BF16)
        acc = None
        for n, i in enumerate(pages):
            part = _dot(e[:, n * PAGE_SIZE:(n + 1) * PAGE_SIZE], v_pages[i][...].astype(BF16), NT_DIMS)
            acc = part if acc is None else acc + part
        acc_ref[j] = acc

    @pl.when(c == pl.num_programs(1) - 1)
    def _():
        pad = jnp.zeros((PAGE_SIZE - t, ATT_DIM), F32)
        k_new = jnp.concatenate([kn_ref[...], pad], axis=0).astype(BF16)
        v_new = jnp.concatenate([vn_ref[...], pad], axis=0).astype(BF16)
        ncol = lax.broadcasted_iota(jnp.int32, (1, PAGE_SIZE), 1)
        qrow = lax.rem(lax.broadcasted_iota(jnp.int32, (rows, 1), 0), t)
        s = _dot(q_all, k_new, NT_DIMS) + slope * ncol.astype(F32)
        s = jnp.where(ncol <= qrow, s, -jnp.inf)
        m = jnp.max(s, axis=1, keepdims=True)
        e = jnp.exp(s - m)
        blk_f = blk.astype(F32)
        gate = jnp.where(blk < n_past_blocks, g_ref[...], -jnp.inf)
        picked = jnp.zeros(gate.shape, F32)
        for _ in range(MOBA_TOPK):
            top = jnp.max(gate, axis=1, keepdims=True)
            first = jnp.min(jnp.where(gate == top, blk_f, float(LANES)), axis=1, keepdims=True)
            picked = jnp.where(blk_f == first, 1.0, picked)
            gate = jnp.where(blk_f == first, -jnp.inf, gate)
        m_all = jnp.maximum(m, jnp.max(jnp.where(picked > 0.0, m_ref[...], -jnp.inf), axis=1, keepdims=True))
        w = jnp.where(picked > 0.0, jnp.exp(m_ref[...] - m_all), 0.0)
        w_new = jnp.exp(m - m_all)
        den = w_new * jnp.sum(e, axis=1, keepdims=True) + jnp.sum(w * l_ref[...], axis=1, keepdims=True)
        parts = [w_new * _dot(e.astype(BF16), v_new)] + [w[:, j:j + 1] * acc_ref[j] for j in range(n_past_blocks)]
        while len(parts) > 1:
            parts = [a + b for a, b in zip(parts[::2], parts[1::2])] + parts[len(parts) - len(parts) % 2:]
        o_ref[...] = _own_head_columns(parts[0] / den, t)


def _moba_sample(q, k_new, v_new, cache_kt, cache_vt, page_ids):
    n, t, c = q.shape
    pages_per_seq = page_ids.shape[0] // n
    past_len = pages_per_seq * PAGE_SIZE
    n_past_blocks = past_len // MOBA_BLOCK
    assert past_len % MOBA_BLOCK == 0 and t <= PAGE_SIZE and t % SUBLANES == 0 and c == ATT_DIM
    assert pages_per_seq % PAGES_PER_STEP == 0 and MOBA_TOPK <= n_past_blocks <= LANES
    new = pl.BlockSpec((None, t, c), lambda i, j, pt: (i, 0, 0))

    def page(slot):
        return pl.BlockSpec((None, c, PAGE_SIZE),
                            lambda i, j, pt: (pt[i * pages_per_seq + j * PAGES_PER_STEP + slot], 0, 0))

    pages = [page(s) for s in range(PAGES_PER_STEP)]
    rows = ATT_HEADS * t
    stat = pltpu.VMEM((rows, LANES), F32)
    return pl.pallas_call(
        functools.partial(_moba_sample_kernel, n_past_blocks=n_past_blocks),
        grid_spec=pltpu.PrefetchScalarGridSpec(
            num_scalar_prefetch=1,
            grid=(n, pages_per_seq // PAGES_PER_STEP),
            in_specs=[new, new, new] + pages + pages,
            out_specs=new,
            scratch_shapes=[stat, stat, stat, pltpu.VMEM((n_past_blocks, rows, c), F32)],
        ),
        out_shape=jax.ShapeDtypeStruct((n, t, c), F32),
        compiler_params=_params("parallel", "arbitrary"),
        name="moba_sample",
    )(page_ids, q, k_new, v_new, *([cache_kt] * PAGES_PER_STEP), *([cache_vt] * PAGES_PER_STEP))


def _retention_constants(c):
    lg = np.log1p(-np.exp2(-5.0 - np.arange(RET_HEADS, dtype=np.float64)))
    idx = np.arange(c, dtype=np.float64)
    diff = idx[:, None] - idx[None, :]
    d_in = np.where(diff >= 0, np.exp(lg[:, None, None] * np.maximum(diff, 0.0)), 0.0)
    q_dec = np.exp(lg[:, None] * (idx + 1.0))
    k_dec = np.exp(lg[:, None] * (c - 1.0 - idx))
    c_dec = np.exp(lg * c)
    per_lane = lambda a: np.repeat(a.reshape(RET_PAIRS, HEADS_PER_TILE, -1).transpose(0, 2, 1), HEAD_DIM, axis=2)
    block = np.kron(np.eye(HEADS_PER_TILE), np.ones((HEAD_DIM, HEAD_DIM)))
    state_dec = block[None] * np.repeat(c_dec.reshape(RET_PAIRS, HEADS_PER_TILE), HEAD_DIM, axis=1)[:, :, None]
    as_f32 = lambda a: jnp.asarray(a, F32)
    return (as_f32(d_in.reshape(RET_PAIRS, HEADS_PER_TILE * c, c)), as_f32(per_lane(q_dec)), as_f32(per_lane(k_dec)),
            as_f32(state_dec), as_f32(block))


def _retention_kernel(*refs, has_state):
    if has_state:
        q_ref, k_ref, v_ref, g_ref, s0_ref, din_ref, qd_ref, kd_ref, sd_ref, blk_ref, y_ref, sout_ref, s_ref = refs
    else:
        q_ref, k_ref, v_ref, g_ref, din_ref, qd_ref, kd_ref, sd_ref, blk_ref, y_ref, sout_ref, s_ref = refs
    ci = pl.program_id(1)
    n_seqs, c, _ = q_ref.shape

    @pl.when(ci == 0)
    def _():
        if not has_state:
            s_ref[...] = jnp.zeros(s_ref.shape, F32)
            return
        zero = jnp.zeros((HEAD_DIM, HEAD_DIM), F32)
        for b in range(n_seqs):
            for p in range(RET_PAIRS):
                even, odd = s0_ref[b, 2 * p].astype(F32), s0_ref[b, 2 * p + 1].astype(F32)
                s_ref[b, p] = jnp.concatenate([jnp.concatenate([even, zero], axis=1),
                                               jnp.concatenate([zero, odd], axis=1)], axis=0)

    lane = lax.broadcasted_iota(jnp.int32, (c, LANES), 1)
    for b in range(n_seqs):
        for p in range(RET_PAIRS):
            cols = slice(p * LANES, (p + 1) * LANES)
            q = q_ref[b, :, cols]
            k = k_ref[b, :, cols] * HEAD_DIM ** -0.5
            v = v_ref[b, :, cols].astype(BF16)
            state = s_ref[b, p]
            att = _dot(_stack_pair(q, lane).astype(BF16), k.astype(BF16), NT_DIMS) * din_ref[p]
            o = _unstack_pair(_dot(att.astype(BF16), v), lane)
            o = o + _dot((q * qd_ref[p]).astype(BF16), state.astype(BF16))
            s_ref[b, p] = state * sd_ref[p] + _dot((k * kd_ref[p]).astype(BF16), v, TN_DIMS) * blk_ref[...]
            sq = o * o
            even = jnp.sum(jnp.where(lane < HEAD_DIM, sq, 0.0), axis=1, keepdims=True)
            odd = jnp.sum(jnp.where(lane >= HEAD_DIM, sq, 0.0), axis=1, keepdims=True)
            ms = jnp.where(lane < HEAD_DIM, even, odd) * (1.0 / HEAD_DIM)
            y_ref[b, :, cols] = _silu(g_ref[b, :, cols]) * (o * lax.rsqrt(ms + EPS))

    @pl.when(ci == pl.num_programs(1) - 1)
    def _():
        for b in range(n_seqs):
            for p in range(RET_PAIRS):
                state = s_ref[b, p]
                sout_ref[b, 2 * p] = state[:HEAD_DIM, :HEAD_DIM]
                sout_ref[b, 2 * p + 1] = state[HEAD_DIM:, HEAD_DIM:]


def _retention(q, k, v, g, state):
    n, t, d = q.shape
    c = math.gcd(t, RET_CHUNK)
    nb = math.gcd(n, RET_SEQS)
    consts = _retention_constants(c)
    chunk = pl.BlockSpec((nb, c, d), lambda i, j: (i, j, 0))
    st = pl.BlockSpec((nb, RET_HEADS, HEAD_DIM, HEAD_DIM), lambda i, j: (i, 0, 0, 0))
    in_specs, args = [chunk] * 4, [q, k, v, g]
    if state is not None:
        in_specs.append(st)
        args.append(state)
    in_specs += [_resident(a) for a in consts]
    args += list(consts)
    return pl.pallas_call(
        functools.partial(_retention_kernel, has_state=state is not None),
        grid=(n // nb, t // c),
        in_specs=in_specs,
        out_specs=[chunk, st],
        out_shape=[jax.ShapeDtypeStruct((n, t, d), F32), jax.ShapeDtypeStruct((n, RET_HEADS, HEAD_DIM, HEAD_DIM), F32)],
        scratch_shapes=[pltpu.VMEM((nb, RET_PAIRS, LANES, LANES), F32)],
        compiler_params=_params("parallel", "arbitrary"),
        name="retention",
    )(*args)


def _layer(x, n, pool_prefix, ret_state, moba, w, final_norm, final, kv_before=()):
    t = x.shape[0] // n
    z = _dense_in(x, w["g1"], w["f1i"], w["f1o"], w["gm"], w["win"],
                  prompt=(t, w["pool_w"], w["pool_scale"]) if moba is None else None)
    seq = lambda a: a.reshape(n, t, a.shape[-1])
    if moba is None:
        assert t >= POOL_PAST
        y_pool, pool_new = z["y_pool"], z["pool_tail"][:, POOL_HALO - POOL_PAST:]
    else:
        u = seq(z["u"])
        halo = jnp.pad(pool_prefix.astype(F32), ((0, 0), (POOL_HALO - POOL_PAST, 0), (0, 0)))
        y_pool = _pool(u, halo, w["pool_w"], w["pool_scale"], moba["past_len"])
        pool_new = jnp.concatenate([pool_prefix.astype(F32), u], axis=1)[:, -POOL_PAST:]
    if moba is None:
        y_att = _moba_prompt(seq(z["qa"]), seq(z["ka"]), z["vt"])
        k_new, v_new = z["kt"], z["vt"]
    else:
        y_att = _moba_sample(seq(z["qa"]), seq(z["ka"]), seq(z["va"]), moba["cache_k"], moba["cache_v"], moba["page_ids"])
        heads = lambda a: a.reshape(n, t, ATT_HEADS, HEAD_DIM)
        k_new, v_new = heads(z["ka"]), heads(z["va"])
    y_ret, ret_new = _retention(seq(z["qr"]), seq(z["kr"]), seq(z["vr"]), seq(z["gr"]), ret_state)
    flat = lambda a: a.reshape(n * t, a.shape[-1])
    stack_layers = ()
    if moba is None and final:
        stack_layers = tuple([before[i] for before in kv_before] + [new] for i, new in enumerate((k_new, v_new)))
    x2, stacked = _dense_out(z["x1"], flat(y_pool), flat(y_att), flat(y_ret), w["wo"], w["g2"], w["f2i"], w["f2o"],
                             final_norm, final, stack_layers)
    if stacked:
        k_new, v_new = stacked
    return x2, k_new, v_new, pool_new, ret_new


def kernel(x_prompt, x_sample, cache_k, cache_v, state_pool, state_ret, page_table, norm_ffn1, ffn1_w_in, ffn1_w_out,
           norm_mix, w_in, pool_w, pool_scale, w_out, norm_ffn2, ffn2_w_in, ffn2_w_out, norm_final):
    depth = w_in.shape[0]
    n_p, t_p, d = x_prompt.shape
    n_s, t_s, _ = x_sample.shape
    n_pool = cache_k.shape[1]
    pages_per_seq = page_table.shape[1]
    assert cache_k.shape[2] == PAGE_SIZE and d % LANES == 0
    pages_t = lambda a: jnp.transpose(a, (0, 1, 3, 4, 2)).reshape(depth * n_pool, ATT_DIM, PAGE_SIZE)
    cache_k, cache_v = pages_t(cache_k), pages_t(cache_v)
    row = lambda a: a.reshape(1, -1).astype(F32)
    final_norm = row(norm_final)
    hp, hs = x_prompt.reshape(n_p * t_p, d), x_sample.reshape(n_s * t_s, d)
    big = {k: a.astype(BF16) for k, a in dict(f1i=ffn1_w_in, f1o=ffn1_w_out, win=w_in, wo=w_out, f2i=ffn2_w_in,
                                              f2o=ffn2_w_out).items()}
    kv_prompt, outs = [], [[] for _ in range(6)]
    for l in range(depth):
        w = dict(g1=row(norm_ffn1[l]), gm=row(norm_mix[l]), g2=row(norm_ffn2[l]),
                 pool_w=jax.scipy.linalg.block_diag(*pool_w[l]).astype(BF16), pool_scale=row(pool_scale[l]),
                 **{k: _Stacked(a, l) for k, a in big.items()})
        final = l == depth - 1
        hp, kp, vp, pp, rp = _layer(hp, n_p, None, None, None, w, final_norm, final, kv_prompt)
        kv_prompt.append((kp, vp))
        moba = dict(cache_k=cache_k, cache_v=cache_v, past_len=pages_per_seq * PAGE_SIZE,
                    page_ids=(page_table.astype(jnp.int32) + l * n_pool).reshape(-1))
        hs, ks, vs, ps, rs = _layer(hs, n_s, state_pool[l], state_ret[l], moba, w, final_norm, final)
        for lst, a in zip(outs, (pp, rp, ks, vs, ps, rs)):
            lst.append(a)
    heads = lambda a: jnp.transpose(a.reshape(depth, n_p, ATT_HEADS, HEAD_DIM, t_p), (0, 1, 4, 2, 3))
    pool_p, ret_p, k_s, v_s, pool_s, ret_s = (jnp.stack(lst) for lst in outs)
    return (hp.reshape(n_p, t_p, d), hs.reshape(n_s, t_s, d), heads(kp), heads(vp), pool_p, ret_p, k_s, v_s, pool_s, ret_s)
```

```python
import functools
import math
from typing import NamedTuple

import numpy as np
import jax
import jax.numpy as jnp
from jax import lax
from jax.experimental import pallas as pl
from jax.experimental.pallas import tpu as pltpu

F32 = jnp.float32
BF16 = jnp.bfloat16

HEAD_DIM = 64
POOL_WINDOWS = (2, 4, 8, 16)
POOL_GRP = HEAD_DIM
POOL_DIM = len(POOL_WINDOWS) * POOL_GRP
POOL_PAST = max(POOL_WINDOWS) - 1
ATT_HEADS = 6
ATT_DIM = ATT_HEADS * HEAD_DIM
RET_HEADS = 6
RET_DIM = RET_HEADS * HEAD_DIM
MOBA_BLOCK = 256
MOBA_TOPK = 3
PAGE_SIZE = 128
RET_CHUNK = 128
EPS = 1e-6

LANES = 128
SUBLANES = 8
BF16_SUBLANES = 16
SLOPE_PARTS = 3
HEADS_PER_TILE = LANES // HEAD_DIM
ATT_PAIRS = ATT_DIM // LANES
RET_PAIRS = RET_DIM // LANES
VMEM_LIMIT = 56 * 1024 * 1024

TOKEN_TILE = 512
FF_CHUNK = 256
POOL_TILE = 512
POOL_HALO = 16
LOG2_E = math.log2(math.e)
PAGES_PER_STEP = 16
PAGE_SLOTS = 3
RET_SEQS = 8
NT_DIMS = (((1,), (1,)), ((), ()))
TN_DIMS = (((0,), (0,)), ((), ()))


def _params(*semantics):
    return pltpu.CompilerParams(dimension_semantics=semantics, vmem_limit_bytes=VMEM_LIMIT)


class _Stacked(NamedTuple):
    stack: jax.Array
    layer: int

    @property
    def shape(self):
        return self.stack.shape[1:]

    def columns(self, lo, hi):
        return self.stack[self.layer, :, lo:hi]


def _resident(a):
    if isinstance(a, _Stacked):
        return pl.BlockSpec((None,) + a.shape, lambda *_: (a.layer,) + (0,) * len(a.shape), pipeline_mode=pl.Buffered(1))
    return pl.BlockSpec(a.shape, lambda *_: (0,) * a.ndim, pipeline_mode=pl.Buffered(1))


def _operand(a):
    return a.stack if isinstance(a, _Stacked) else a


def _rmsnorm(x, g):
    return x * lax.rsqrt(jnp.mean(x * x, axis=-1, keepdims=True) + EPS) * g


def _silu(x):
    return x * (1.0 / (1.0 + jnp.exp(-x)))


def _dot(a, b, dims=None, **kw):
    if dims is None:
        return jnp.dot(a, b, preferred_element_type=F32, **kw)
    return lax.dot_general(a, b, dims, preferred_element_type=F32, **kw)


def _stack_pair(x, lane):
    return jnp.concatenate([jnp.where(lane < HEAD_DIM, x, 0.0), jnp.where(lane >= HEAD_DIM, x, 0.0)], axis=0)


def _unstack_pair(y, lane):
    r = y.shape[0] // 2
    return jnp.where(lane < HEAD_DIM, y[:r], y[r:])


def _alibi_slope(h):
    return 2.0 ** (-8.0 * (h + 1) / ATT_HEADS)


def _swiglu_half_step(x, g_ref, wi_ref, wo_ref):
    d_ff = wo_ref.shape[0]
    h = _rmsnorm(x, g_ref[...]).astype(BF16)
    acc = jnp.zeros(x.shape, F32)
    for c in range(d_ff // FF_CHUNK):
        lo = c * FF_CHUNK
        gate = _dot(h, wi_ref[:, lo:lo + FF_CHUNK])
        up = _dot(h, wi_ref[:, d_ff + lo:d_ff + lo + FF_CHUNK])
        acc = acc + _dot((_silu(gate) * up).astype(BF16), wo_ref[lo:lo + FF_CHUNK, :])
    return x + 0.5 * acc


def _pool_tile(u, carry, pos, w_blockdiag, scale):
    tt = u.shape[0]
    ext = jnp.concatenate([carry, u], axis=0)
    sums, s = [], ext
    for w in POOL_WINDOWS:
        s = s + pltpu.roll(s, w // 2, axis=0)
        sums.append(s[POOL_HALO:])
    lane = lax.broadcasted_iota(jnp.int32, (1, POOL_DIM), 1)
    pooled = None
    for g, w in reversed(list(enumerate(POOL_WINDOWS))):
        mean = sums[g] * (1.0 / jnp.minimum(w, pos + 1).astype(F32))
        pooled = mean if pooled is None else jnp.where(lane < (g + 1) * POOL_GRP, mean, pooled)
    d = (pooled - u).astype(BF16)
    return _dot(d, w_blockdiag) * scale, ext[tt:]


def _dense_in_kernel(x_ref, g1_ref, f1i_ref, f1o_ref, gm_ref, win_ref, *refs, col_offsets, seq_tiles):
    if seq_tiles:
        (wt_ref, pw_ref, ps_ref), refs = refs[:3], refs[3:]
        (kt_ref, vt_ref, ypool_ref, tail_ref, carry_ref), refs = refs[-5:], refs[:-5]
    x1_ref, z_refs = refs[0], refs[1:]
    x1 = _swiglu_half_step(x_ref[...], g1_ref, f1i_ref, f1o_ref)
    x1_ref[...] = x1
    h = _rmsnorm(x1, gm_ref[...]).astype(BF16)
    for z_ref, off in zip(z_refs, col_offsets):
        z_ref[...] = _dot(h, win_ref[:, off:off + z_ref.shape[-1]]).astype(z_ref.dtype)
    if seq_tiles:
        zt = _dot(wt_ref[...], h, NT_DIMS)
        kt_ref[...] = zt[:ATT_DIM]
        vt_ref[...] = zt[ATT_DIM:]
        tile = lax.rem(pl.program_id(0), seq_tiles)
        tm = x_ref.shape[0]

        @pl.when(tile == 0)
        def _():
            carry_ref[...] = jnp.zeros(carry_ref.shape, F32)

        u = _dot(h, win_ref[:, Z_COLUMNS["u"][0]:sum(Z_COLUMNS["u"])])
        pos = tile * tm + lax.broadcasted_iota(jnp.int32, (tm, 1), 0)
        y, carry = _pool_tile(u, carry_ref[...], pos, pw_ref[...], ps_ref[...])
        ypool_ref[...] = y
        carry_ref[...] = carry
        tail_ref[...] = carry


Z_COLUMNS = dict(zip(("u", "qa", "ka", "va", "qr", "kr", "vr", "gr"),
                     ((0, POOL_DIM), (POOL_DIM, ATT_DIM), (POOL_DIM + ATT_DIM, ATT_DIM), (POOL_DIM + 2 * ATT_DIM, ATT_DIM))
                     + tuple((POOL_DIM + 3 * ATT_DIM + i * RET_DIM, RET_DIM) for i in range(4))))


def _dense_in(x, g1, f1i, f1o, gm, win, prompt=None):
    t, d = x.shape
    tm = min(TOKEN_TILE, t)
    assert t % tm == 0 and win.shape[1] == sum(w for _, w in Z_COLUMNS.values()) and f1o.shape[0] % FF_CHUNK == 0
    rows = lambda w: pl.BlockSpec((tm, w), lambda i: (i, 0))
    names = [k for k in Z_COLUMNS if not (prompt and k in ("u", "va"))]
    dtypes = [BF16 if (prompt and k == "ka") else F32 for k in names]
    in_specs = [rows(d)] + [_resident(a) for a in (g1, f1i, f1o, gm, win)]
    args = [x, g1, f1i, f1o, gm, win]
    out_specs = [rows(d)] + [rows(Z_COLUMNS[k][1]) for k in names]
    out_shape = [jax.ShapeDtypeStruct((t, d), F32)] + [jax.ShapeDtypeStruct((t, Z_COLUMNS[k][1]), dt)
                                                      for k, dt in zip(names, dtypes)]
    col_offsets = tuple(Z_COLUMNS[k][0] for k in names)
    scratch, tiles = [], 0
    if prompt:
        seq_len, pool_w, pool_scale = prompt
        assert seq_len % tm == 0 and t % seq_len == 0 and Z_COLUMNS["va"][0] == sum(Z_COLUMNS["ka"])
        tiles = seq_len // tm
        kv_t = win.columns(Z_COLUMNS["ka"][0], sum(Z_COLUMNS["va"])).T
        in_specs += [_resident(a) for a in (kv_t, pool_w, pool_scale)]
        args += [kv_t, pool_w, pool_scale]
        out_specs += [pl.BlockSpec((None, ATT_DIM, tm), lambda i: (i // tiles, 0, i % tiles))] * 2
        out_specs += [rows(POOL_DIM), pl.BlockSpec((None, POOL_HALO, POOL_DIM), lambda i: (i // tiles, 0, 0))]
        out_shape += [jax.ShapeDtypeStruct((t // seq_len, ATT_DIM, seq_len), F32)] * 2
        out_shape += [jax.ShapeDtypeStruct((t, POOL_DIM), F32), jax.ShapeDtypeStruct((t // seq_len, POOL_HALO, POOL_DIM), F32)]
        names += ["kt", "vt", "y_pool", "pool_tail"]
        scratch = [pltpu.VMEM((POOL_HALO, POOL_DIM), F32)]
    outs = pl.pallas_call(
        functools.partial(_dense_in_kernel, col_offsets=col_offsets, seq_tiles=tiles),
        grid=(t // tm,),
        in_specs=in_specs,
        out_specs=out_specs,
        out_shape=out_shape,
        scratch_shapes=scratch,
        compiler_params=_params("arbitrary" if prompt else "parallel"),
        name="dense_in",
    )(*[_operand(a) for a in args])
    return dict(zip(["x1"] + names, outs))


def _dense_out_kernel(x_ref, yp_ref, ya_ref, yr_ref, wo_ref, g2_ref, f2i_ref, f2o_ref, gf_ref, *refs, final, n_gather):
    gather_refs, o_ref, gathered_refs = refs[:n_gather], refs[n_gather], refs[n_gather + 1:]
    off, mix = 0, None
    for y_ref in (yp_ref, ya_ref, yr_ref):
        w = y_ref.shape[-1]
        part = _dot(y_ref[...].astype(BF16), wo_ref[off:off + w, :])
        mix = part if mix is None else mix + part
        off += w
    x = _swiglu_half_step(x_ref[...] + mix, g2_ref, f2i_ref, f2o_ref)
    if final:
        x = _rmsnorm(x, gf_ref[...])
    o_ref[...] = x
    per_out = n_gather // max(len(gathered_refs), 1)
    for k, out_ref in enumerate(gathered_refs):
        for l in range(per_out):
            out_ref[l] = gather_refs[k * per_out + l][...]


def _dense_out(x, yp, ya, yr, wo, g2, f2i, f2o, gf, final, stack_layers=()):
    t, d = x.shape
    tm = min(TOKEN_TILE, t)
    assert t % tm == 0
    rows = lambda w: pl.BlockSpec((tm, w), lambda i: (i, 0))
    in_specs = [rows(d), rows(yp.shape[1]), rows(ya.shape[1]), rows(yr.shape[1])] + [_resident(a) for a in (wo, g2, f2i, f2o, gf)]
    args = [x, yp, ya, yr, wo, g2, f2i, f2o, gf]
    out_specs, out_shape = [rows(d)], [jax.ShapeDtypeStruct((t, d), F32)]
    for group in stack_layers:
        n, c, seq_len = group[0].shape
        tiles = seq_len // tm
        assert seq_len % tm == 0 and n * seq_len == t and all(a.shape == group[0].shape for a in group)
        in_specs += [pl.BlockSpec((None, c, tm), lambda i: (i // tiles, 0, i % tiles))] * len(group)
        args += list(group)
        out_specs.append(pl.BlockSpec((len(group), None, c, tm), lambda i: (0, i // tiles, 0, i % tiles)))
        out_shape.append(jax.ShapeDtypeStruct((len(group), n, c, seq_len), group[0].dtype))
    outs = pl.pallas_call(
        functools.partial(_dense_out_kernel, final=final, n_gather=sum(len(g) for g in stack_layers)),
        grid=(t // tm,),
        in_specs=in_specs,
        out_specs=out_specs,
        out_shape=out_shape,
        compiler_params=_params("parallel"),
        name="dense_out",
    )(*[_operand(a) for a in args])
    return outs[0], tuple(outs[1:])


def _pool_kernel(u_ref, pre_ref, w_ref, scale_ref, y_ref, carry_ref, *, tt, pos0):
    t = pl.program_id(1)

    @pl.when(t == 0)
    def _():
        carry_ref[...] = pre_ref[...]

    pos = pos0 + t * tt + lax.broadcasted_iota(jnp.int32, (tt, 1), 0)
    for b in range(u_ref.shape[0]):
        y_ref[b], carry_ref[b] = _pool_tile(u_ref[b], carry_ref[b], pos, w_ref[...], scale_ref[...])


def _pool(u, prefix, w_blockdiag, scale, pos0):
    n, t, c = u.shape
    tt = min(POOL_TILE, t)
    nb = math.gcd(n, POOL_TILE // tt)
    assert t % tt == 0 and tt % SUBLANES == 0 and POOL_WINDOWS == (2, 4, 8, 16)
    seq = pl.BlockSpec((nb, tt, c), lambda i, j: (i, j, 0))
    return pl.pallas_call(
        functools.partial(_pool_kernel, tt=tt, pos0=pos0),
        grid=(n // nb, t // tt),
        in_specs=[seq, pl.BlockSpec((nb, POOL_HALO, c), lambda i, j: (i, 0, 0)), _resident(w_blockdiag), _resident(scale)],
        out_specs=seq,
        out_shape=jax.ShapeDtypeStruct((n, t, c), F32),
        scratch_shapes=[pltpu.VMEM((nb, POOL_HALO, c), F32)],
        compiler_params=_params("parallel", "arbitrary"),
        name="pool_mixer",
    )(u, prefix, w_blockdiag, scale)


def _moba_prompt_kernel(q_ref, k_ref, vt_in_ref, o_ref, kmean_ref, vt_ref, kpos_ref, t_ref, *, n_blocks):
    qi = pl.program_id(1)
    tq = q_ref.shape[0]
    rows = HEADS_PER_TILE * tq
    dim = lax.broadcasted_iota(jnp.int32, (LANES, 1), 0)
    key = lax.broadcasted_iota(jnp.int32, (MOBA_BLOCK, rows), 0)
    col = lax.broadcasted_iota(jnp.int32, (1, rows), 1)
    blk = lax.broadcasted_iota(jnp.int32, (kmean_ref.shape[0], 1), 0)
    slopes = [jnp.where(col < tq, _alibi_slope(2 * p) * LOG2_E, _alibi_slope(2 * p + 1) * LOG2_E).astype(F32)
              for p in range(ATT_PAIRS)]

    @pl.when(qi == 0)
    def _():
        kmean_ref[...] = jnp.zeros(kmean_ref.shape, F32)
        ones = jnp.ones((BF16_SUBLANES, MOBA_BLOCK), BF16)
        for j in range(n_blocks):
            keys = slice(j * MOBA_BLOCK, (j + 1) * MOBA_BLOCK)
            kmean_ref[j:j + 1, :] = jnp.mean(k_ref[keys, :].astype(F32), axis=0, keepdims=True)
            for h in range(ATT_HEADS):
                vt_ref[j, h] = jnp.concatenate([vt_in_ref[h * HEAD_DIM:(h + 1) * HEAD_DIM, keys].astype(BF16), ones], axis=0)
        kpos_ref[...] = jnp.where(lax.broadcasted_iota(jnp.int32, kpos_ref.shape, 1) < SLOPE_PARTS,
                                  lax.broadcasted_iota(jnp.int32, kpos_ref.shape, 0), 0).astype(F32).astype(BF16)

    def tile(own):
        causal = key <= lax.rem(col, tq)
        for p in range(ATT_PAIRS):
            cols = slice(p * LANES, (p + 1) * LANES)
            q_t = (q_ref[:, cols] * (HEAD_DIM ** -0.5 * LOG2_E)).T
            qs_t = jnp.concatenate([jnp.where(dim < HEAD_DIM, q_t, 0.0), jnp.where(dim >= HEAD_DIM, q_t, 0.0)], axis=1)
            slope_rows, rest = jnp.zeros((LANES, rows), F32), slopes[p]
            for i in range(SLOPE_PARTS):
                piece = rest.astype(BF16).astype(F32)
                slope_rows, rest = jnp.where(dim == i, piece, slope_rows), rest - piece
            qs_b = jnp.concatenate([qs_t, slope_rows], axis=0).astype(BF16)
            sel = None
            if own > MOBA_TOPK:
                gate = _dot(kmean_ref[:, cols], qs_t, precision=lax.Precision.HIGHEST)
                sel = []
                for j in range(own):
                    gj = gate[j:j + 1, :]
                    ahead = jnp.where(blk < j, jnp.where(gate >= gj, 1.0, 0.0), jnp.where(gate > gj, 1.0, 0.0))
                    sel.append(jnp.sum(jnp.where(blk < own, ahead, 0.0), axis=0, keepdims=True) < MOBA_TOPK)
            block_bias = lambda j: slopes[p] * float((j - own) * MOBA_BLOCK)
            maxima = []
            for j in range(own + 1):
                k_blk = jnp.concatenate([k_ref[j * MOBA_BLOCK:(j + 1) * MOBA_BLOCK, cols], kpos_ref[...]], axis=1)
                t = _dot(k_blk, qs_b)
                if j == own:
                    t = jnp.where(causal, t, -jnp.inf)
                t_ref[p, j] = t
                m = jnp.max(t, axis=0, keepdims=True) + block_bias(j)
                maxima.append(m if sel is None or j == own else jnp.where(sel[j], m, -jnp.inf))
            m_all = functools.reduce(jnp.maximum, maxima)
            acc = [None] * HEADS_PER_TILE
            for j in range(own + 1):
                shift = m_all - block_bias(j)
                if sel is not None and j < own:
                    shift = jnp.where(sel[j], shift, jnp.inf)
                e = jnp.exp2(t_ref[p, j] - shift).astype(BF16)
                for h in range(HEADS_PER_TILE):
                    part = _dot(vt_ref[j, HEADS_PER_TILE * p + h], e[:, h * tq:(h + 1) * tq])
                    acc[h] = part if acc[h] is None else acc[h] + part
            out_t = jnp.concatenate([a[:HEAD_DIM] / a[HEAD_DIM:HEAD_DIM + 1] for a in acc], axis=0)
            o_ref[:, cols] = out_t.T

    for own in range(n_blocks):
        pl.when(qi == own)(functools.partial(tile, own))


def _moba_prompt(q, k, vt):
    n, t, c = q.shape
    tq = MOBA_BLOCK
    n_blocks = t // MOBA_BLOCK
    gate_rows = -(-n_blocks // SUBLANES) * SUBLANES
    rows = HEADS_PER_TILE * tq
    assert t % MOBA_BLOCK == 0 and c == ATT_DIM and vt.shape == (n, c, t)
    tile = pl.BlockSpec((None, tq, c), lambda i, j: (i, j, 0))
    return pl.pallas_call(
        functools.partial(_moba_prompt_kernel, n_blocks=n_blocks),
        grid=(n, n_blocks),
        in_specs=[tile, pl.BlockSpec((None, t, c), lambda i, j: (i, 0, 0)), pl.BlockSpec((None, c, t), lambda i, j: (i, 0, 0))],
        out_specs=tile,
        out_shape=jax.ShapeDtypeStruct((n, t, c), F32),
        scratch_shapes=[pltpu.VMEM((gate_rows, c), F32),
                        pltpu.VMEM((n_blocks, ATT_HEADS, HEAD_DIM + BF16_SUBLANES, MOBA_BLOCK), BF16),
                        pltpu.VMEM((MOBA_BLOCK, LANES), BF16),
                        pltpu.VMEM((ATT_PAIRS, n_blocks, MOBA_BLOCK, rows), F32)],
        compiler_params=_params("parallel", "arbitrary"),
        name="moba_prompt",
    )(q, k, vt)


def _head_rows(x):
    t = x.shape[0]
    lane_head = lax.broadcasted_iota(jnp.int32, (1, ATT_DIM), 1) // HEAD_DIM
    row_head = lax.broadcasted_iota(jnp.int32, (ATT_HEADS * t, 1), 0) // t
    return jnp.where(lane_head == row_head, jnp.concatenate([x] * ATT_HEADS, axis=0), 0.0)


def _own_head_columns(y, t):
    lane_head = lax.broadcasted_iota(jnp.int32, (1, ATT_DIM), 1) // HEAD_DIM
    out = jnp.zeros((t, ATT_DIM), F32)
    for h in range(ATT_HEADS):
        out = jnp.where(lane_head == h, y[h * t:(h + 1) * t], out)
    return out


def _moba_sample_kernel(pt_ref, q_ref, kn_ref, vn_ref, kc_hbm, vc_hbm, o_ref, k_buf, v_buf, sem, m_ref, l_ref, g_ref,
                        acc_ref, *, n_past_blocks, steps_per_seq):
    step = pl.program_id(0)
    n_steps = pl.num_programs(0)
    c = lax.rem(step, steps_per_seq)
    slot = lax.rem(step, PAGE_SLOTS)

    def page_copies(of_step):
        to = lax.rem(of_step, PAGE_SLOTS)
        for i in range(PAGES_PER_STEP):
            page = pt_ref[of_step * PAGES_PER_STEP + i]
            yield pltpu.make_async_copy(kc_hbm.at[page], k_buf.at[to, i], sem.at[0, to])
            yield pltpu.make_async_copy(vc_hbm.at[page], v_buf.at[to, i], sem.at[1, to])

    @pl.when(step == 0)
    def _():
        for first in range(PAGE_SLOTS - 1):
            for copy in page_copies(first):
                copy.start()

    @pl.when(step + (PAGE_SLOTS - 1) < n_steps)
    def _():
        for copy in page_copies(step + (PAGE_SLOTS - 1)):
            copy.start()

    for copy in page_copies(step):
        copy.wait()

    k_pages = [k_buf.at[slot, i] for i in range(PAGES_PER_STEP)]
    v_pages = [v_buf.at[slot, i] for i in range(PAGES_PER_STEP)]
    t = q_ref.shape[0]
    rows = ATT_HEADS * t
    pages_per_block = MOBA_BLOCK // PAGE_SIZE
    blocks_per_step = PAGES_PER_STEP // pages_per_block
    past_len = n_past_blocks * MOBA_BLOCK

    row_head = lax.broadcasted_iota(jnp.int32, (rows, 1), 0) // t
    slope = jnp.zeros((rows, 1), F32)
    for h in range(ATT_HEADS):
        slope = jnp.where(row_head == h, _alibi_slope(h), slope)
    q_f32 = _head_rows(q_ref[...] * HEAD_DIM ** -0.5)
    q_all = q_f32.astype(BF16)
    q_split = jnp.concatenate([q_all, (q_f32 - q_all.astype(F32)).astype(BF16)], axis=0)
    kcol = lax.broadcasted_iota(jnp.int32, (1, MOBA_BLOCK), 1)
    blk = lax.broadcasted_iota(jnp.int32, (1, LANES), 1)

    @pl.when(c == 0)
    def _():
        g_ref[...] = jnp.zeros(g_ref.shape, F32)
        m_ref[...] = jnp.full(m_ref.shape, -jnp.inf, F32)
        l_ref[...] = jnp.zeros(l_ref.shape, F32)

    raw = [_dot(q_split, k_ref[...].astype(BF16)) for k_ref in k_pages]
    raw = [r[:rows] + r[rows:] for r in raw]
    for b in range(blocks_per_step):
        j = c * blocks_per_step + b
        pages = range(b * pages_per_block, (b + 1) * pages_per_block)
        s = jnp.concatenate([raw[i] for i in pages], axis=1)
        g_ref[...] = jnp.where(blk == j, jnp.sum(s, axis=1, keepdims=True), g_ref[...])
        s = s + slope * (kcol + (j * MOBA_BLOCK - past_len)).astype(F32)
        m = jnp.max(s, axis=1, keepdims=True)
        e = jnp.exp(s - m)
        m_ref[...] = jnp.where(blk == j, m, m_ref[...])
        l_ref[...] = jnp.where(blk == j, jnp.sum(e, axis=1, keepdims=True), l_ref[...])
        e = e.astype(BF16)
        acc = None
        for n, i in enumerate(pages):
            part = _dot(e[:, n * PAGE_SIZE:(n + 1) * PAGE_SIZE], v_pages[i][...].astype(BF16), NT_DIMS)
            acc = part if acc is None else acc + part
        acc_ref[j] = acc

    @pl.when(c == steps_per_seq - 1)
    def _():
        pad = jnp.zeros((PAGE_SIZE - t, ATT_DIM), F32)
        k_new = jnp.concatenate([kn_ref[...], pad], axis=0).astype(BF16)
        v_new = jnp.concatenate([vn_ref[...], pad], axis=0).astype(BF16)
        ncol = lax.broadcasted_iota(jnp.int32, (1, PAGE_SIZE), 1)
        qrow = lax.rem(lax.broadcasted_iota(jnp.int32, (rows, 1), 0), t)
        s = _dot(q_all, k_new, NT_DIMS) + slope * ncol.astype(F32)
        s = jnp.where(ncol <= qrow, s, -jnp.inf)
        m = jnp.max(s, axis=1, keepdims=True)
        e = jnp.exp(s - m)
        blk_f = blk.astype(F32)
        gate = jnp.where(blk < n_past_blocks, g_ref[...], -jnp.inf)
        picked = jnp.zeros(gate.shape, F32)
        for _ in range(MOBA_TOPK):
            top = jnp.max(gate, axis=1, keepdims=True)
            first = jnp.min(jnp.where(gate == top, blk_f, float(LANES)), axis=1, keepdims=True)
            picked = jnp.where(blk_f == first, 1.0, picked)
            gate = jnp.where(blk_f == first, -jnp.inf, gate)
        m_all = jnp.maximum(m, jnp.max(jnp.where(picked > 0.0, m_ref[...], -jnp.inf), axis=1, keepdims=True))
        w = jnp.where(picked > 0.0, jnp.exp(m_ref[...] - m_all), 0.0)
        w_new = jnp.exp(m - m_all)
        den = w_new * jnp.sum(e, axis=1, keepdims=True) + jnp.sum(w * l_ref[...], axis=1, keepdims=True)
        parts = [w_new * _dot(e.astype(BF16), v_new)] + [w[:, j:j + 1] * acc_ref[j] for j in range(n_past_blocks)]
        while len(parts) > 1:
            parts = [a + b for a, b in zip(parts[::2], parts[1::2])] + parts[len(parts) - len(parts) % 2:]
        o_ref[...] = _own_head_columns(parts[0] / den, t)


def _moba_sample(q, k_new, v_new, cache_kt, cache_vt, page_ids):
    n, t, c = q.shape
    pages_per_seq = page_ids.shape[0] // n
    past_len = pages_per_seq * PAGE_SIZE
    n_past_blocks = past_len // MOBA_BLOCK
    assert past_len % MOBA_BLOCK == 0 and t <= PAGE_SIZE and t % SUBLANES == 0 and c == ATT_DIM
    assert pages_per_seq % PAGES_PER_STEP == 0 and MOBA_TOPK <= n_past_blocks <= LANES
    steps_per_seq = pages_per_seq // PAGES_PER_STEP
    assert n * steps_per_seq >= PAGE_SLOTS - 1
    new = pl.BlockSpec((None, t, c), lambda i, pt: (i // steps_per_seq, 0, 0))
    hbm = pl.BlockSpec(memory_space=pl.ANY)
    rows = ATT_HEADS * t
    stat = pltpu.VMEM((rows, LANES), F32)
    ring = pltpu.VMEM((PAGE_SLOTS, PAGES_PER_STEP, c, PAGE_SIZE), cache_kt.dtype)
    return pl.pallas_call(
        functools.partial(_moba_sample_kernel, n_past_blocks=n_past_blocks, steps_per_seq=steps_per_seq),
        grid_spec=pltpu.PrefetchScalarGridSpec(
            num_scalar_prefetch=1,
            grid=(n * steps_per_seq,),
            in_specs=[new, new, new, hbm, hbm],
            out_specs=new,
            scratch_shapes=[ring, ring, pltpu.SemaphoreType.DMA((2, PAGE_SLOTS)), stat, stat, stat,
                            pltpu.VMEM((n_past_blocks, rows, c), F32)],
        ),
        out_shape=jax.ShapeDtypeStruct((n, t, c), F32),
        compiler_params=_params("arbitrary"),
        name="moba_sample",
    )(page_ids, q, k_new, v_new, cache_kt, cache_vt)


def _retention_constants(c):
    lg = np.log1p(-np.exp2(-5.0 - np.arange(RET_HEADS, dtype=np.float64)))
    idx = np.arange(c, dtype=np.float64)
    diff = idx[:, None] - idx[None, :]
    d_in = np.where(diff >= 0, np.exp(lg[:, None, None] * np.maximum(diff, 0.0)), 0.0)
    q_dec = np.exp(lg[:, None] * (idx + 1.0))
    k_dec = np.exp(lg[:, None] * (c - 1.0 - idx))
    c_dec = np.exp(lg * c)
    per_lane = lambda a: np.repeat(a.reshape(RET_PAIRS, HEADS_PER_TILE, -1).transpose(0, 2, 1), HEAD_DIM, axis=2)
    block = np.kron(np.eye(HEADS_PER_TILE), np.ones((HEAD_DIM, HEAD_DIM)))
    state_dec = block[None] * np.repeat(c_dec.reshape(RET_PAIRS, HEADS_PER_TILE), HEAD_DIM, axis=1)[:, :, None]
    as_f32 = lambda a: jnp.asarray(a, F32)
    return (as_f32(d_in.reshape(RET_PAIRS, HEADS_PER_TILE * c, c)), as_f32(per_lane(q_dec)), as_f32(per_lane(k_dec)),
            as_f32(state_dec), as_f32(block))


def _retention_kernel(*refs, has_state):
    if has_state:
        q_ref, k_ref, v_ref, g_ref, s0_ref, din_ref, qd_ref, kd_ref, sd_ref, blk_ref, y_ref, sout_ref, s_ref = refs
    else:
        q_ref, k_ref, v_ref, g_ref, din_ref, qd_ref, kd_ref, sd_ref, blk_ref, y_ref, sout_ref, s_ref = refs
    ci = pl.program_id(1)
    n_seqs, c, _ = q_ref.shape

    @pl.when(ci == 0)
    def _():
        if not has_state:
            s_ref[...] = jnp.zeros(s_ref.shape, F32)
            return
        zero = jnp.zeros((HEAD_DIM, HEAD_DIM), F32)
        for b in range(n_seqs):
            for p in range(RET_PAIRS):
                even, odd = s0_ref[b, 2 * p].astype(F32), s0_ref[b, 2 * p + 1].astype(F32)
                s_ref[b, p] = jnp.concatenate([jnp.concatenate([even, zero], axis=1),
                                               jnp.concatenate([zero, odd], axis=1)], axis=0)

    lane = lax.broadcasted_iota(jnp.int32, (c, LANES), 1)
    for b in range(n_seqs):
        for p in range(RET_PAIRS):
            cols = slice(p * LANES, (p + 1) * LANES)
            q = q_ref[b, :, cols]
            k = k_ref[b, :, cols] * HEAD_DIM ** -0.5
            v = v_ref[b, :, cols].astype(BF16)
            state = s_ref[b, p]
            att = _dot(_stack_pair(q, lane).astype(BF16), k.astype(BF16), NT_DIMS) * din_ref[p]
            o = _unstack_pair(_dot(att.astype(BF16), v), lane)
            o = o + _dot((q * qd_ref[p]).astype(BF16), state.astype(BF16))
            s_ref[b, p] = state * sd_ref[p] + _dot((k * kd_ref[p]).astype(BF16), v, TN_DIMS) * blk_ref[...]
            sq = o * o
            even = jnp.sum(jnp.where(lane < HEAD_DIM, sq, 0.0), axis=1, keepdims=True)
            odd = jnp.sum(jnp.where(lane >= HEAD_DIM, sq, 0.0), axis=1, keepdims=True)
            ms = jnp.where(lane < HEAD_DIM, even, odd) * (1.0 / HEAD_DIM)
            y_ref[b, :, cols] = _silu(g_ref[b, :, cols]) * (o * lax.rsqrt(ms + EPS))

    @pl.when(ci == pl.num_programs(1) - 1)
    def _():
        for b in range(n_seqs):
            for p in range(RET_PAIRS):
                state = s_ref[b, p]
                sout_ref[b, 2 * p] = state[:HEAD_DIM, :HEAD_DIM]
                sout_ref[b, 2 * p + 1] = state[HEAD_DIM:, HEAD_DIM:]


def _retention(q, k, v, g, state):
    n, t, d = q.shape
    c = math.gcd(t, RET_CHUNK)
    nb = math.gcd(n, RET_SEQS)
    consts = _retention_constants(c)
    chunk = pl.BlockSpec((nb, c, d), lambda i, j: (i, j, 0))
    st = pl.BlockSpec((nb, RET_HEADS, HEAD_DIM, HEAD_DIM), lambda i, j: (i, 0, 0, 0))
    in_specs, args = [chunk] * 4, [q, k, v, g]
    if state is not None:
        in_specs.append(st)
        args.append(state)
    in_specs += [_resident(a) for a in consts]
    args += list(consts)
    return pl.pallas_call(
        functools.partial(_retention_kernel, has_state=state is not None),
        grid=(n // nb, t // c),
        in_specs=in_specs,
        out_specs=[chunk, st],
        out_shape=[jax.ShapeDtypeStruct((n, t, d), F32), jax.ShapeDtypeStruct((n, RET_HEADS, HEAD_DIM, HEAD_DIM), F32)],
        scratch_shapes=[pltpu.VMEM((nb, RET_PAIRS, LANES, LANES), F32)],
        compiler_params=_params("parallel", "arbitrary"),
        name="retention",
    )(*args)


def _layer(x, n, pool_prefix, ret_state, moba, w, final_norm, final, kv_before=()):
    t = x.shape[0] // n
    z = _dense_in(x, w["g1"], w["f1i"], w["f1o"], w["gm"], w["win"],
                  prompt=(t, w["pool_w"], w["pool_scale"]) if moba is None else None)
    seq = lambda a: a.reshape(n, t, a.shape[-1])
    if moba is None:
        assert t >= POOL_PAST
        y_pool, pool_new = z["y_pool"], z["pool_tail"][:, POOL_HALO - POOL_PAST:]
    else:
        u = seq(z["u"])
        halo = jnp.pad(pool_prefix.astype(F32), ((0, 0), (POOL_HALO - POOL_PAST, 0), (0, 0)))
        y_pool = _pool(u, halo, w["pool_w"], w["pool_scale"], moba["past_len"])
        pool_new = jnp.concatenate([pool_prefix.astype(F32), u], axis=1)[:, -POOL_PAST:]
    if moba is None:
        y_att = _moba_prompt(seq(z["qa"]), seq(z["ka"]), z["vt"])
        k_new, v_new = z["kt"], z["vt"]
    else:
        y_att = _moba_sample(seq(z["qa"]), seq(z["ka"]), seq(z["va"]), moba["cache_k"], moba["cache_v"], moba["page_ids"])
        heads = lambda a: a.reshape(n, t, ATT_HEADS, HEAD_DIM)
        k_new, v_new = heads(z["ka"]), heads(z["va"])
    y_ret, ret_new = _retention(seq(z["qr"]), seq(z["kr"]), seq(z["vr"]), seq(z["gr"]), ret_state)
    flat = lambda a: a.reshape(n * t, a.shape[-1])
    stack_layers = ()
    if moba is None and final:
        stack_layers = tuple([before[i] for before in kv_before] + [new] for i, new in enumerate((k_new, v_new)))
    x2, stacked = _dense_out(z["x1"], flat(y_pool), flat(y_att), flat(y_ret), w["wo"], w["g2"], w["f2i"], w["f2o"],
                             final_norm, final, stack_layers)
    if stacked:
        k_new, v_new = stacked
    return x2, k_new, v_new, pool_new, ret_new


def kernel(x_prompt, x_sample, cache_k, cache_v, state_pool, state_ret, page_table, norm_ffn1, ffn1_w_in, ffn1_w_out,
           norm_mix, w_in, pool_w, pool_scale, w_out, norm_ffn2, ffn2_w_in, ffn2_w_out, norm_final):
    depth = w_in.shape[0]
    n_p, t_p, d = x_prompt.shape
    n_s, t_s, _ = x_sample.shape
    n_pool = cache_k.shape[1]
    pages_per_seq = page_table.shape[1]
    assert cache_k.shape[2] == PAGE_SIZE and d % LANES == 0
    pages_t = lambda a: jnp.transpose(a, (0, 1, 3, 4, 2)).reshape(depth * n_pool, ATT_DIM, PAGE_SIZE)
    cache_k, cache_v = pages_t(cache_k), pages_t(cache_v)
    row = lambda a: a.reshape(1, -1).astype(F32)
    final_norm = row(norm_final)
    hp, hs = x_prompt.reshape(n_p * t_p, d), x_sample.reshape(n_s * t_s, d)
    big = {k: a.astype(BF16) for k, a in dict(f1i=ffn1_w_in, f1o=ffn1_w_out, win=w_in, wo=w_out, f2i=ffn2_w_in,
                                              f2o=ffn2_w_out).items()}
    kv_prompt, outs = [], [[] for _ in range(6)]
    for l in range(depth):
        w = dict(g1=row(norm_ffn1[l]), gm=row(norm_mix[l]), g2=row(norm_ffn2[l]),
                 pool_w=jax.scipy.linalg.block_diag(*pool_w[l]).astype(BF16), pool_scale=row(pool_scale[l]),
                 **{k: _Stacked(a, l) for k, a in big.items()})
        final = l == depth - 1
        hp, kp, vp, pp, rp = _layer(hp, n_p, None, None, None, w, final_norm, final, kv_prompt)
        kv_prompt.append((kp, vp))
        moba = dict(cache_k=cache_k, cache_v=cache_v, past_len=pages_per_seq * PAGE_SIZE,
                    page_ids=(page_table.astype(jnp.int32) + l * n_pool).reshape(-1))
        hs, ks, vs, ps, rs = _layer(hs, n_s, state_pool[l], state_ret[l], moba, w, final_norm, final)
        for lst, a in zip(outs, (pp, rp, ks, vs, ps, rs)):
            lst.append(a)
    heads = lambda a: jnp.transpose(a.reshape(depth, n_p, ATT_HEADS, HEAD_DIM, t_p), (0, 1, 4, 2, 3))
    pool_p, ret_p, k_s, v_s, pool_s, ret_s = (jnp.stack(lst) for lst in outs)
    return (hp.reshape(n_p, t_p, d), hs.reshape(n_s, t_s, d), heads(kp), heads(vp), pool_p, ret_p, k_s, v_s, pool_s, ret_s)
```

```python
import functools
import math
from typing import NamedTuple

import numpy as np
import jax
import jax.numpy as jnp
from jax import lax
from jax.experimental import pallas as pl
from jax.experimental.pallas import tpu as pltpu

F32 = jnp.float32
BF16 = jnp.bfloat16

HEAD_DIM = 64
POOL_WINDOWS = (2, 4, 8, 16)
POOL_GRP = HEAD_DIM
POOL_DIM = len(POOL_WINDOWS) * POOL_GRP
POOL_PAST = max(POOL_WINDOWS) - 1
ATT_HEADS = 6
ATT_DIM = ATT_HEADS * HEAD_DIM
RET_HEADS = 6
RET_DIM = RET_HEADS * HEAD_DIM
MOBA_BLOCK = 256
MOBA_TOPK = 3
PAGE_SIZE = 128
RET_CHUNK = 128
EPS = 1e-6

LANES = 128
SUBLANES = 8
BF16_SUBLANES = 16
SLOPE_PARTS = 3
HEADS_PER_TILE = LANES // HEAD_DIM
ATT_PAIRS = ATT_DIM // LANES
RET_PAIRS = RET_DIM // LANES
VMEM_LIMIT = 56 * 1024 * 1024

TOKEN_TILE = 512
FF_CHUNK = 256
POOL_TILE = 512
POOL_HALO = 16
LOG2_E = math.log2(math.e)
PAGES_PER_STEP = 16
PAGE_SLOTS = 4
RET_SEQS = 8
NT_DIMS = (((1,), (1,)), ((), ()))
TN_DIMS = (((0,), (0,)), ((), ()))


def _params(*semantics):
    return pltpu.CompilerParams(dimension_semantics=semantics, vmem_limit_bytes=VMEM_LIMIT)


class _Stacked(NamedTuple):
    stack: jax.Array
    layer: int

    @property
    def shape(self):
        return self.stack.shape[1:]

    def columns(self, lo, hi):
        return self.stack[self.layer, :, lo:hi]


def _resident(a):
    if isinstance(a, _Stacked):
        return pl.BlockSpec((None,) + a.shape, lambda *_: (a.layer,) + (0,) * len(a.shape), pipeline_mode=pl.Buffered(1))
    return pl.BlockSpec(a.shape, lambda *_: (0,) * a.ndim, pipeline_mode=pl.Buffered(1))


def _operand(a):
    return a.stack if isinstance(a, _Stacked) else a


def _rmsnorm(x, g):
    return x * lax.rsqrt(jnp.mean(x * x, axis=-1, keepdims=True) + EPS) * g


def _silu(x):
    return x * (1.0 / (1.0 + jnp.exp(-x)))


def _dot(a, b, dims=None, **kw):
    if dims is None:
        return jnp.dot(a, b, preferred_element_type=F32, **kw)
    return lax.dot_general(a, b, dims, preferred_element_type=F32, **kw)


def _stack_pair(x, lane):
    return jnp.concatenate([jnp.where(lane < HEAD_DIM, x, 0.0), jnp.where(lane >= HEAD_DIM, x, 0.0)], axis=0)


def _unstack_pair(y, lane):
    r = y.shape[0] // 2
    return jnp.where(lane < HEAD_DIM, y[:r], y[r:])


def _alibi_slope(h):
    return 2.0 ** (-8.0 * (h + 1) / ATT_HEADS)


def _swiglu_half_step(x, g_ref, wi_ref, wo_ref):
    d_ff = wo_ref.shape[0]
    h = _rmsnorm(x, g_ref[...]).astype(BF16)
    acc = jnp.zeros(x.shape, F32)
    for c in range(d_ff // FF_CHUNK):
        lo = c * FF_CHUNK
        gate = _dot(h, wi_ref[:, lo:lo + FF_CHUNK])
        up = _dot(h, wi_ref[:, d_ff + lo:d_ff + lo + FF_CHUNK])
        acc = acc + _dot((_silu(gate) * up).astype(BF16), wo_ref[lo:lo + FF_CHUNK, :])
    return x + 0.5 * acc


def _pool_tile(u, carry, pos, w_blockdiag, scale):
    tt = u.shape[0]
    ext = jnp.concatenate([carry, u], axis=0)
    sums, s = [], ext
    for w in POOL_WINDOWS:
        s = s + pltpu.roll(s, w // 2, axis=0)
        sums.append(s[POOL_HALO:])
    lane = lax.broadcasted_iota(jnp.int32, (1, POOL_DIM), 1)
    pooled = None
    for g, w in reversed(list(enumerate(POOL_WINDOWS))):
        mean = sums[g] * (1.0 / jnp.minimum(w, pos + 1).astype(F32))
        pooled = mean if pooled is None else jnp.where(lane < (g + 1) * POOL_GRP, mean, pooled)
    d = (pooled - u).astype(BF16)
    return _dot(d, w_blockdiag) * scale, ext[tt:]


def _dense_in_kernel(x_ref, g1_ref, f1i_ref, f1o_ref, gm_ref, win_ref, *refs, col_offsets, seq_tiles):
    if seq_tiles:
        (wt_ref, pw_ref, ps_ref), refs = refs[:3], refs[3:]
        (kt_ref, vt_ref, ypool_ref, tail_ref, carry_ref), refs = refs[-5:], refs[:-5]
    x1_ref, z_refs = refs[0], refs[1:]
    x1 = _swiglu_half_step(x_ref[...], g1_ref, f1i_ref, f1o_ref)
    x1_ref[...] = x1
    h = _rmsnorm(x1, gm_ref[...]).astype(BF16)
    for z_ref, off in zip(z_refs, col_offsets):
        z_ref[...] = _dot(h, win_ref[:, off:off + z_ref.shape[-1]]).astype(z_ref.dtype)
    if seq_tiles:
        zt = _dot(wt_ref[...], h, NT_DIMS)
        kt_ref[...] = zt[:ATT_DIM]
        vt_ref[...] = zt[ATT_DIM:]
        tile = lax.rem(pl.program_id(0), seq_tiles)
        tm = x_ref.shape[0]

        @pl.when(tile == 0)
        def _():
            carry_ref[...] = jnp.zeros(carry_ref.shape, F32)

        u = _dot(h, win_ref[:, Z_COLUMNS["u"][0]:sum(Z_COLUMNS["u"])])
        pos = tile * tm + lax.broadcasted_iota(jnp.int32, (tm, 1), 0)
        y, carry = _pool_tile(u, carry_ref[...], pos, pw_ref[...], ps_ref[...])
        ypool_ref[...] = y
        carry_ref[...] = carry
        tail_ref[...] = carry


Z_COLUMNS = dict(zip(("u", "qa", "ka", "va", "qr", "kr", "vr", "gr"),
                     ((0, POOL_DIM), (POOL_DIM, ATT_DIM), (POOL_DIM + ATT_DIM, ATT_DIM), (POOL_DIM + 2 * ATT_DIM, ATT_DIM))
                     + tuple((POOL_DIM + 3 * ATT_DIM + i * RET_DIM, RET_DIM) for i in range(4))))


def _dense_in(x, g1, f1i, f1o, gm, win, prompt=None):
    t, d = x.shape
    tm = min(TOKEN_TILE, t)
    assert t % tm == 0 and win.shape[1] == sum(w for _, w in Z_COLUMNS.values()) and f1o.shape[0] % FF_CHUNK == 0
    rows = lambda w: pl.BlockSpec((tm, w), lambda i: (i, 0))
    names = [k for k in Z_COLUMNS if not (prompt and k in ("u", "va"))]
    dtypes = [BF16 if (prompt and k == "ka") else F32 for k in names]
    in_specs = [rows(d)] + [_resident(a) for a in (g1, f1i, f1o, gm, win)]
    args = [x, g1, f1i, f1o, gm, win]
    out_specs = [rows(d)] + [rows(Z_COLUMNS[k][1]) for k in names]
    out_shape = [jax.ShapeDtypeStruct((t, d), F32)] + [jax.ShapeDtypeStruct((t, Z_COLUMNS[k][1]), dt)
                                                      for k, dt in zip(names, dtypes)]
    col_offsets = tuple(Z_COLUMNS[k][0] for k in names)
    scratch, tiles = [], 0
    if prompt:
        seq_len, pool_w, pool_scale = prompt
        assert seq_len % tm == 0 and t % seq_len == 0 and Z_COLUMNS["va"][0] == sum(Z_COLUMNS["ka"])
        tiles = seq_len // tm
        kv_t = win.columns(Z_COLUMNS["ka"][0], sum(Z_COLUMNS["va"])).T
        in_specs += [_resident(a) for a in (kv_t, pool_w, pool_scale)]
        args += [kv_t, pool_w, pool_scale]
        out_specs += [pl.BlockSpec((None, ATT_DIM, tm), lambda i: (i // tiles, 0, i % tiles))] * 2
        out_specs += [rows(POOL_DIM), pl.BlockSpec((None, POOL_HALO, POOL_DIM), lambda i: (i // tiles, 0, 0))]
        out_shape += [jax.ShapeDtypeStruct((t // seq_len, ATT_DIM, seq_len), F32)] * 2
        out_shape += [jax.ShapeDtypeStruct((t, POOL_DIM), F32), jax.ShapeDtypeStruct((t // seq_len, POOL_HALO, POOL_DIM), F32)]
        names += ["kt", "vt", "y_pool", "pool_tail"]
        scratch = [pltpu.VMEM((POOL_HALO, POOL_DIM), F32)]
    outs = pl.pallas_call(
        functools.partial(_dense_in_kernel, col_offsets=col_offsets, seq_tiles=tiles),
        grid=(t // tm,),
        in_specs=in_specs,
        out_specs=out_specs,
        out_shape=out_shape,
        scratch_shapes=scratch,
        compiler_params=_params("arbitrary" if prompt else "parallel"),
        name="dense_in",
    )(*[_operand(a) for a in args])
    return dict(zip(["x1"] + names, outs))


def _dense_out_kernel(x_ref, yp_ref, ya_ref, yr_ref, wo_ref, g2_ref, f2i_ref, f2o_ref, gf_ref, *refs, final, n_gather):
    gather_refs, o_ref, gathered_refs = refs[:n_gather], refs[n_gather], refs[n_gather + 1:]
    off, mix = 0, None
    for y_ref in (yp_ref, ya_ref, yr_ref):
        w = y_ref.shape[-1]
        part = _dot(y_ref[...].astype(BF16), wo_ref[off:off + w, :])
        mix = part if mix is None else mix + part
        off += w
    x = _swiglu_half_step(x_ref[...] + mix, g2_ref, f2i_ref, f2o_ref)
    if final:
        x = _rmsnorm(x, gf_ref[...])
    o_ref[...] = x
    per_out = n_gather // max(len(gathered_refs), 1)
    for k, out_ref in enumerate(gathered_refs):
        for l in range(per_out):
            out_ref[l] = gather_refs[k * per_out + l][...]


def _dense_out(x, yp, ya, yr, wo, g2, f2i, f2o, gf, final, stack_layers=()):
    t, d = x.shape
    tm = min(TOKEN_TILE, t)
    assert t % tm == 0
    rows = lambda w: pl.BlockSpec((tm, w), lambda i: (i, 0))
    in_specs = [rows(d), rows(yp.shape[1]), rows(ya.shape[1]), rows(yr.shape[1])] + [_resident(a) for a in (wo, g2, f2i, f2o, gf)]
    args = [x, yp, ya, yr, wo, g2, f2i, f2o, gf]
    out_specs, out_shape = [rows(d)], [jax.ShapeDtypeStruct((t, d), F32)]
    for group in stack_layers:
        n, c, seq_len = group[0].shape
        tiles = seq_len // tm
        assert seq_len % tm == 0 and n * seq_len == t and all(a.shape == group[0].shape for a in group)
        in_specs += [pl.BlockSpec((None, c, tm), lambda i: (i // tiles, 0, i % tiles))] * len(group)
        args += list(group)
        out_specs.append(pl.BlockSpec((len(group), None, c, tm), lambda i: (0, i // tiles, 0, i % tiles)))
        out_shape.append(jax.ShapeDtypeStruct((len(group), n, c, seq_len), group[0].dtype))
    outs = pl.pallas_call(
        functools.partial(_dense_out_kernel, final=final, n_gather=sum(len(g) for g in stack_layers)),
        grid=(t // tm,),
        in_specs=in_specs,
        out_specs=out_specs,
        out_shape=out_shape,
        compiler_params=_params("parallel"),
        name="dense_out",
    )(*[_operand(a) for a in args])
    return outs[0], tuple(outs[1:])


def _pool_kernel(u_ref, pre_ref, w_ref, scale_ref, y_ref, carry_ref, *, tt, pos0):
    t = pl.program_id(1)

    @pl.when(t == 0)
    def _():
        carry_ref[...] = pre_ref[...]

    pos = pos0 + t * tt + lax.broadcasted_iota(jnp.int32, (tt, 1), 0)
    for b in range(u_ref.shape[0]):
        y_ref[b], carry_ref[b] = _pool_tile(u_ref[b], carry_ref[b], pos, w_ref[...], scale_ref[...])


def _pool(u, prefix, w_blockdiag, scale, pos0):
    n, t, c = u.shape
    tt = min(POOL_TILE, t)
    nb = math.gcd(n, POOL_TILE // tt)
    assert t % tt == 0 and tt % SUBLANES == 0 and POOL_WINDOWS == (2, 4, 8, 16)
    seq = pl.BlockSpec((nb, tt, c), lambda i, j: (i, j, 0))
    return pl.pallas_call(
        functools.partial(_pool_kernel, tt=tt, pos0=pos0),
        grid=(n // nb, t // tt),
        in_specs=[seq, pl.BlockSpec((nb, POOL_HALO, c), lambda i, j: (i, 0, 0)), _resident(w_blockdiag), _resident(scale)],
        out_specs=seq,
        out_shape=jax.ShapeDtypeStruct((n, t, c), F32),
        scratch_shapes=[pltpu.VMEM((nb, POOL_HALO, c), F32)],
        compiler_params=_params("parallel", "arbitrary"),
        name="pool_mixer",
    )(u, prefix, w_blockdiag, scale)


def _prompt_tile(qi, q_ref, k_ref, vt_in_ref, o_ref, kmean_ref, vt_ref, kpos_ref, t_ref, n_blocks):
    tq = q_ref.shape[0]
    rows = HEADS_PER_TILE * tq
    dim = lax.broadcasted_iota(jnp.int32, (LANES, 1), 0)
    key = lax.broadcasted_iota(jnp.int32, (MOBA_BLOCK, rows), 0)
    col = lax.broadcasted_iota(jnp.int32, (1, rows), 1)
    blk = lax.broadcasted_iota(jnp.int32, (kmean_ref.shape[0], 1), 0)
    slopes = [jnp.where(col < tq, _alibi_slope(2 * p) * LOG2_E, _alibi_slope(2 * p + 1) * LOG2_E).astype(F32)
              for p in range(ATT_PAIRS)]

    @pl.when(qi == 0)
    def _():
        kmean_ref[...] = jnp.zeros(kmean_ref.shape, F32)
        ones = jnp.ones((BF16_SUBLANES, MOBA_BLOCK), BF16)
        for j in range(n_blocks):
            keys = slice(j * MOBA_BLOCK, (j + 1) * MOBA_BLOCK)
            kmean_ref[j:j + 1, :] = jnp.mean(k_ref[keys, :].astype(F32), axis=0, keepdims=True)
            for h in range(ATT_HEADS):
                vt_ref[j, h] = jnp.concatenate([vt_in_ref[h * HEAD_DIM:(h + 1) * HEAD_DIM, keys].astype(BF16), ones], axis=0)
        kpos_ref[...] = jnp.where(lax.broadcasted_iota(jnp.int32, kpos_ref.shape, 1) < SLOPE_PARTS,
                                  lax.broadcasted_iota(jnp.int32, kpos_ref.shape, 0), 0).astype(F32).astype(BF16)

    def tile(own):
        causal = key <= lax.rem(col, tq)
        for p in range(ATT_PAIRS):
            cols = slice(p * LANES, (p + 1) * LANES)
            q_t = (q_ref[:, cols] * (HEAD_DIM ** -0.5 * LOG2_E)).T
            qs_t = jnp.concatenate([jnp.where(dim < HEAD_DIM, q_t, 0.0), jnp.where(dim >= HEAD_DIM, q_t, 0.0)], axis=1)
            slope_rows, rest = jnp.zeros((LANES, rows), F32), slopes[p]
            for i in range(SLOPE_PARTS):
                piece = rest.astype(BF16).astype(F32)
                slope_rows, rest = jnp.where(dim == i, piece, slope_rows), rest - piece
            qs_b = jnp.concatenate([qs_t, slope_rows], axis=0).astype(BF16)
            sel = None
            if own > MOBA_TOPK:
                gate = _dot(kmean_ref[:, cols], qs_t, precision=lax.Precision.HIGHEST)
                sel = []
                for j in range(own):
                    gj = gate[j:j + 1, :]
                    ahead = jnp.where(blk < j, jnp.where(gate >= gj, 1.0, 0.0), jnp.where(gate > gj, 1.0, 0.0))
                    sel.append(jnp.sum(jnp.where(blk < own, ahead, 0.0), axis=0, keepdims=True) < MOBA_TOPK)
            block_bias = lambda j: slopes[p] * float((j - own) * MOBA_BLOCK)
            maxima = []
            for j in range(own + 1):
                k_blk = jnp.concatenate([k_ref[j * MOBA_BLOCK:(j + 1) * MOBA_BLOCK, cols], kpos_ref[...]], axis=1)
                t = _dot(k_blk, qs_b)
                if j == own:
                    t = jnp.where(causal, t, -jnp.inf)
                t_ref[p, j] = t
                m = jnp.max(t, axis=0, keepdims=True) + block_bias(j)
                maxima.append(m if sel is None or j == own else jnp.where(sel[j], m, -jnp.inf))
            m_all = functools.reduce(jnp.maximum, maxima)
            acc = [None] * HEADS_PER_TILE
            for j in range(own + 1):
                shift = m_all - block_bias(j)
                if sel is not None and j < own:
                    shift = jnp.where(sel[j], shift, jnp.inf)
                e = jnp.exp2(t_ref[p, j] - shift).astype(BF16)
                for h in range(HEADS_PER_TILE):
                    part = _dot(vt_ref[j, HEADS_PER_TILE * p + h], e[:, h * tq:(h + 1) * tq])
                    acc[h] = part if acc[h] is None else acc[h] + part
            out_t = jnp.concatenate([a[:HEAD_DIM] / a[HEAD_DIM:HEAD_DIM + 1] for a in acc], axis=0)
            o_ref[:, cols] = out_t.T

    for own in range(n_blocks):
        pl.when(qi == own)(functools.partial(tile, own))


def _head_rows(x):
    t = x.shape[0]
    lane_head = lax.broadcasted_iota(jnp.int32, (1, ATT_DIM), 1) // HEAD_DIM
    row_head = lax.broadcasted_iota(jnp.int32, (ATT_HEADS * t, 1), 0) // t
    return jnp.where(lane_head == row_head, jnp.concatenate([x] * ATT_HEADS, axis=0), 0.0)


def _own_head_columns(y, t):
    lane_head = lax.broadcasted_iota(jnp.int32, (1, ATT_DIM), 1) // HEAD_DIM
    out = jnp.zeros((t, ATT_DIM), F32)
    for h in range(ATT_HEADS):
        out = jnp.where(lane_head == h, y[h * t:(h + 1) * t], out)
    return out


def _sample_group(step, n_steps, pt_ref, qs_ref, kns_ref, vns_ref, kc_hbm, vc_hbm, os_ref, k_buf, v_buf, sem, m_ref, l_ref,
                  g_ref, acc_ref, n_past_blocks, steps_per_seq):
    seq = step // steps_per_seq
    c = lax.rem(step, steps_per_seq)
    slot = lax.rem(step, PAGE_SLOTS)
    q_ref, kn_ref, vn_ref, o_ref = qs_ref.at[seq], kns_ref.at[seq], vns_ref.at[seq], os_ref.at[seq]

    def page_copies(of_step):
        to = lax.rem(of_step, PAGE_SLOTS)
        for i in range(PAGES_PER_STEP):
            page = pt_ref[of_step * PAGES_PER_STEP + i]
            yield pltpu.make_async_copy(kc_hbm.at[page], k_buf.at[to, i], sem.at[0, to])
            yield pltpu.make_async_copy(vc_hbm.at[page], v_buf.at[to, i], sem.at[1, to])

    @pl.when(step == 0)
    def _():
        for first in range(PAGE_SLOTS - 1):
            for copy in page_copies(first):
                copy.start()

    @pl.when(step + (PAGE_SLOTS - 1) < n_steps)
    def _():
        for copy in page_copies(step + (PAGE_SLOTS - 1)):
            copy.start()

    for copy in page_copies(step):
        copy.wait()

    k_pages = [k_buf.at[slot, i] for i in range(PAGES_PER_STEP)]
    v_pages = [v_buf.at[slot, i] for i in range(PAGES_PER_STEP)]
    t = q_ref.shape[0]
    rows = ATT_HEADS * t
    pages_per_block = MOBA_BLOCK // PAGE_SIZE
    blocks_per_step = PAGES_PER_STEP // pages_per_block
    past_len = n_past_blocks * MOBA_BLOCK

    row_head = lax.broadcasted_iota(jnp.int32, (rows, 1), 0) // t
    slope = jnp.zeros((rows, 1), F32)
    for h in range(ATT_HEADS):
        slope = jnp.where(row_head == h, _alibi_slope(h), slope)
    q_f32 = _head_rows(q_ref[...] * HEAD_DIM ** -0.5)
    q_all = q_f32.astype(BF16)
    q_split = jnp.concatenate([q_all, (q_f32 - q_all.astype(F32)).astype(BF16)], axis=0)
    kcol = lax.broadcasted_iota(jnp.int32, (1, MOBA_BLOCK), 1)
    blk = lax.broadcasted_iota(jnp.int32, (1, LANES), 1)

    @pl.when(c == 0)
    def _():
        g_ref[...] = jnp.zeros(g_ref.shape, F32)
        m_ref[...] = jnp.full(m_ref.shape, -jnp.inf, F32)
        l_ref[...] = jnp.zeros(l_ref.shape, F32)

    raw = [_dot(q_split, k_ref[...].astype(BF16)) for k_ref in k_pages]
    raw = [r[:rows] + r[rows:] for r in raw]
    for b in range(blocks_per_step):
        j = c * blocks_per_step + b
        pages = range(b * pages_per_block, (b + 1) * pages_per_block)
        s = jnp.concatenate([raw[i] for i in pages], axis=1)
        g_ref[...] = jnp.where(blk == j, jnp.sum(s, axis=1, keepdims=True), g_ref[...])
        s = s + slope * (kcol + (j * MOBA_BLOCK - past_len)).astype(F32)
        m = jnp.max(s, axis=1, keepdims=True)
        e = jnp.exp(s - m)
        m_ref[...] = jnp.where(blk == j, m, m_ref[...])
        l_ref[...] = jnp.where(blk == j, jnp.sum(e, axis=1, keepdims=True), l_ref[...])
        e = e.astype(BF16)
        acc = None
        for n, i in enumerate(pages):
            part = _dot(e[:, n * PAGE_SIZE:(n + 1) * PAGE_SIZE], v_pages[i][...].astype(BF16), NT_DIMS)
            acc = part if acc is None else acc + part
        acc_ref[j] = acc

    @pl.when(c == steps_per_seq - 1)
    def _():
        pad = jnp.zeros((PAGE_SIZE - t, ATT_DIM), F32)
        k_new = jnp.concatenate([kn_ref[...], pad], axis=0).astype(BF16)
        v_new = jnp.concatenate([vn_ref[...], pad], axis=0).astype(BF16)
        ncol = lax.broadcasted_iota(jnp.int32, (1, PAGE_SIZE), 1)
        qrow = lax.rem(lax.broadcasted_iota(jnp.int32, (rows, 1), 0), t)
        s = _dot(q_all, k_new, NT_DIMS) + slope * ncol.astype(F32)
        s = jnp.where(ncol <= qrow, s, -jnp.inf)
        m = jnp.max(s, axis=1, keepdims=True)
        e = jnp.exp(s - m)
        blk_f = blk.astype(F32)
        gate = jnp.where(blk < n_past_blocks, g_ref[...], -jnp.inf)
        picked = jnp.zeros(gate.shape, F32)
        for _ in range(MOBA_TOPK):
            top = jnp.max(gate, axis=1, keepdims=True)
            first = jnp.min(jnp.where(gate == top, blk_f, float(LANES)), axis=1, keepdims=True)
            picked = jnp.where(blk_f == first, 1.0, picked)
            gate = jnp.where(blk_f == first, -jnp.inf, gate)
        m_all = jnp.maximum(m, jnp.max(jnp.where(picked > 0.0, m_ref[...], -jnp.inf), axis=1, keepdims=True))
        w = jnp.where(picked > 0.0, jnp.exp(m_ref[...] - m_all), 0.0)
        w_new = jnp.exp(m - m_all)
        den = w_new * jnp.sum(e, axis=1, keepdims=True) + jnp.sum(w * l_ref[...], axis=1, keepdims=True)
        parts = [w_new * _dot(e.astype(BF16), v_new)] + [w[:, j:j + 1] * acc_ref[j] for j in range(n_past_blocks)]
        while len(parts) > 1:
            parts = [a + b for a, b in zip(parts[::2], parts[1::2])] + parts[len(parts) - len(parts) % 2:]
        o_ref[...] = _own_head_columns(parts[0] / den, t)


def _moba_kernel(pt_ref, q_ref, k_ref, vt_in_ref, qs_ref, kns_ref, vns_ref, kc_hbm, vc_hbm, o_ref, os_ref,
                 kmean_ref, vt_ref, kpos_ref, t_ref, k_buf, v_buf, sem, m_ref, l_ref, g_ref, acc_ref, *,
                 n_blocks, n_past_blocks, steps_per_seq, groups_per_step):
    step = pl.program_id(0)
    n_groups = pl.num_programs(0) * groups_per_step
    for i in range(groups_per_step):
        _sample_group(step * groups_per_step + i, n_groups, pt_ref, qs_ref, kns_ref, vns_ref, kc_hbm, vc_hbm, os_ref,
                      k_buf, v_buf, sem, m_ref, l_ref, g_ref, acc_ref, n_past_blocks, steps_per_seq)
    _prompt_tile(lax.rem(step, n_blocks), q_ref, k_ref, vt_in_ref, o_ref, kmean_ref, vt_ref, kpos_ref, t_ref, n_blocks)


def _moba(q, k, vt, q_s, k_new, v_new, cache_kt, cache_vt, page_ids):
    n, t, c = q.shape
    n_blocks = t // MOBA_BLOCK
    gate_rows = -(-n_blocks // SUBLANES) * SUBLANES
    rows = HEADS_PER_TILE * MOBA_BLOCK
    assert t % MOBA_BLOCK == 0 and c == ATT_DIM and vt.shape == (n, c, t)
    ns, ts, _ = q_s.shape
    pages_per_seq = page_ids.shape[0] // ns
    past_len = pages_per_seq * PAGE_SIZE
    n_past_blocks = past_len // MOBA_BLOCK
    assert past_len % MOBA_BLOCK == 0 and ts <= PAGE_SIZE and ts % SUBLANES == 0
    assert pages_per_seq % PAGES_PER_STEP == 0 and MOBA_TOPK <= n_past_blocks <= LANES
    steps_per_seq = pages_per_seq // PAGES_PER_STEP
    n_steps, n_groups = n * n_blocks, ns * steps_per_seq
    assert n_groups % n_steps == 0 and n_groups >= PAGE_SLOTS - 1
    tile = pl.BlockSpec((None, MOBA_BLOCK, c), lambda i, pt: (i // n_blocks, i % n_blocks, 0))
    whole = lambda a: pl.BlockSpec(a.shape, lambda i, pt: (0,) * a.ndim)
    hbm = pl.BlockSpec(memory_space=pl.ANY)
    stat = pltpu.VMEM((ATT_HEADS * ts, LANES), F32)
    ring = pltpu.VMEM((PAGE_SLOTS, PAGES_PER_STEP, c, PAGE_SIZE), cache_kt.dtype)
    return pl.pallas_call(
        functools.partial(_moba_kernel, n_blocks=n_blocks, n_past_blocks=n_past_blocks, steps_per_seq=steps_per_seq,
                          groups_per_step=n_groups // n_steps),
        grid_spec=pltpu.PrefetchScalarGridSpec(
            num_scalar_prefetch=1,
            grid=(n_steps,),
            in_specs=[tile, pl.BlockSpec((None, t, c), lambda i, pt: (i // n_blocks, 0, 0)),
                      pl.BlockSpec((None, c, t), lambda i, pt: (i // n_blocks, 0, 0)),
                      whole(q_s), whole(k_new), whole(v_new), hbm, hbm],
            out_specs=[tile, whole(q_s)],
            scratch_shapes=[pltpu.VMEM((gate_rows, c), F32),
                            pltpu.VMEM((n_blocks, ATT_HEADS, HEAD_DIM + BF16_SUBLANES, MOBA_BLOCK), BF16),
                            pltpu.VMEM((MOBA_BLOCK, LANES), BF16),
                            pltpu.VMEM((ATT_PAIRS, n_blocks, MOBA_BLOCK, rows), F32),
                            ring, ring, pltpu.SemaphoreType.DMA((2, PAGE_SLOTS)), stat, stat, stat,
                            pltpu.VMEM((n_past_blocks, ATT_HEADS * ts, c), F32)],
        ),
        out_shape=[jax.ShapeDtypeStruct((n, t, c), F32), jax.ShapeDtypeStruct((ns, ts, c), F32)],
        compiler_params=_params("arbitrary"),
        name="moba",
    )(page_ids, q, k, vt, q_s, k_new, v_new, cache_kt, cache_vt)


def _retention_constants(c):
    lg = np.log1p(-np.exp2(-5.0 - np.arange(RET_HEADS, dtype=np.float64)))
    idx = np.arange(c, dtype=np.float64)
    diff = idx[:, None] - idx[None, :]
    d_in = np.where(diff >= 0, np.exp(lg[:, None, None] * np.maximum(diff, 0.0)), 0.0)
    q_dec = np.exp(lg[:, None] * (idx + 1.0))
    k_dec = np.exp(lg[:, None] * (c - 1.0 - idx))
    c_dec = np.exp(lg * c)
    per_lane = lambda a: np.repeat(a.reshape(RET_PAIRS, HEADS_PER_TILE, -1).transpose(0, 2, 1), HEAD_DIM, axis=2)
    block = np.kron(np.eye(HEADS_PER_TILE), np.ones((HEAD_DIM, HEAD_DIM)))
    state_dec = block[None] * np.repeat(c_dec.reshape(RET_PAIRS, HEADS_PER_TILE), HEAD_DIM, axis=1)[:, :, None]
    as_f32 = lambda a: jnp.asarray(a, F32)
    return (as_f32(d_in.reshape(RET_PAIRS, HEADS_PER_TILE * c, c)), as_f32(per_lane(q_dec)), as_f32(per_lane(k_dec)),
            as_f32(state_dec), as_f32(block))


def _retention_kernel(*refs, has_state):
    if has_state:
        q_ref, k_ref, v_ref, g_ref, s0_ref, din_ref, qd_ref, kd_ref, sd_ref, blk_ref, y_ref, sout_ref, s_ref = refs
    else:
        q_ref, k_ref, v_ref, g_ref, din_ref, qd_ref, kd_ref, sd_ref, blk_ref, y_ref, sout_ref, s_ref = refs
    ci = pl.program_id(1)
    n_seqs, c, _ = q_ref.shape

    @pl.when(ci == 0)
    def _():
        if not has_state:
            s_ref[...] = jnp.zeros(s_ref.shape, F32)
            return
        zero = jnp.zeros((HEAD_DIM, HEAD_DIM), F32)
        for b in range(n_seqs):
            for p in range(RET_PAIRS):
                even, odd = s0_ref[b, 2 * p].astype(F32), s0_ref[b, 2 * p + 1].astype(F32)
                s_ref[b, p] = jnp.concatenate([jnp.concatenate([even, zero], axis=1),
                                               jnp.concatenate([zero, odd], axis=1)], axis=0)

    lane = lax.broadcasted_iota(jnp.int32, (c, LANES), 1)
    for b in range(n_seqs):
        for p in range(RET_PAIRS):
            cols = slice(p * LANES, (p + 1) * LANES)
            q = q_ref[b, :, cols]
            k = k_ref[b, :, cols] * HEAD_DIM ** -0.5
            v = v_ref[b, :, cols].astype(BF16)
            state = s_ref[b, p]
            att = _dot(_stack_pair(q, lane).astype(BF16), k.astype(BF16), NT_DIMS) * din_ref[p]
            o = _unstack_pair(_dot(att.astype(BF16), v), lane)
            o = o + _dot((q * qd_ref[p]).astype(BF16), state.astype(BF16))
            s_ref[b, p] = state * sd_ref[p] + _dot((k * kd_ref[p]).astype(BF16), v, TN_DIMS) * blk_ref[...]
            sq = o * o
            even = jnp.sum(jnp.where(lane < HEAD_DIM, sq, 0.0), axis=1, keepdims=True)
            odd = jnp.sum(jnp.where(lane >= HEAD_DIM, sq, 0.0), axis=1, keepdims=True)
            ms = jnp.where(lane < HEAD_DIM, even, odd) * (1.0 / HEAD_DIM)
            y_ref[b, :, cols] = _silu(g_ref[b, :, cols]) * (o * lax.rsqrt(ms + EPS))

    @pl.when(ci == pl.num_programs(1) - 1)
    def _():
        for b in range(n_seqs):
            for p in range(RET_PAIRS):
                state = s_ref[b, p]
                sout_ref[b, 2 * p] = state[:HEAD_DIM, :HEAD_DIM]
                sout_ref[b, 2 * p + 1] = state[HEAD_DIM:, HEAD_DIM:]


def _retention(q, k, v, g, state):
    n, t, d = q.shape
    c = math.gcd(t, RET_CHUNK)
    nb = math.gcd(n, RET_SEQS)
    consts = _retention_constants(c)
    chunk = pl.BlockSpec((nb, c, d), lambda i, j: (i, j, 0))
    st = pl.BlockSpec((nb, RET_HEADS, HEAD_DIM, HEAD_DIM), lambda i, j: (i, 0, 0, 0))
    in_specs, args = [chunk] * 4, [q, k, v, g]
    if state is not None:
        in_specs.append(st)
        args.append(state)
    in_specs += [_resident(a) for a in consts]
    args += list(consts)
    return pl.pallas_call(
        functools.partial(_retention_kernel, has_state=state is not None),
        grid=(n // nb, t // c),
        in_specs=in_specs,
        out_specs=[chunk, st],
        out_shape=[jax.ShapeDtypeStruct((n, t, d), F32), jax.ShapeDtypeStruct((n, RET_HEADS, HEAD_DIM, HEAD_DIM), F32)],
        scratch_shapes=[pltpu.VMEM((nb, RET_PAIRS, LANES, LANES), F32)],
        compiler_params=_params("parallel", "arbitrary"),
        name="retention",
    )(*args)


def _layer(xp, n_p, xs, n_s, pool_prefix, ret_state, cache, w, final_norm, final, kv_before):
    t_p, t_s = xp.shape[0] // n_p, xs.shape[0] // n_s
    assert t_p >= POOL_PAST
    zp = _dense_in(xp, w["g1"], w["f1i"], w["f1o"], w["gm"], w["win"], prompt=(t_p, w["pool_w"], w["pool_scale"]))
    zs = _dense_in(xs, w["g1"], w["f1i"], w["f1o"], w["gm"], w["win"])
    seq_p = lambda a: a.reshape(n_p, t_p, a.shape[-1])
    seq_s = lambda a: a.reshape(n_s, t_s, a.shape[-1])
    att_p, att_s = _moba(seq_p(zp["qa"]), seq_p(zp["ka"]), zp["vt"], seq_s(zs["qa"]), seq_s(zs["ka"]), seq_s(zs["va"]),
                         cache["k"], cache["v"], cache["page_ids"])
    u = seq_s(zs["u"])
    halo = jnp.pad(pool_prefix.astype(F32), ((0, 0), (POOL_HALO - POOL_PAST, 0), (0, 0)))
    pool_s = _pool(u, halo, w["pool_w"], w["pool_scale"], cache["past_len"])
    pool_new_s = jnp.concatenate([pool_prefix.astype(F32), u], axis=1)[:, -POOL_PAST:]
    pool_new_p = zp["pool_tail"][:, POOL_HALO - POOL_PAST:]
    ret_p, state_p = _retention(seq_p(zp["qr"]), seq_p(zp["kr"]), seq_p(zp["vr"]), seq_p(zp["gr"]), None)
    ret_s, state_s = _retention(seq_s(zs["qr"]), seq_s(zs["kr"]), seq_s(zs["vr"]), seq_s(zs["gr"]), ret_state)
    flat = lambda a: a.reshape(-1, a.shape[-1])
    k_p, v_p = zp["kt"], zp["vt"]
    stack_layers = tuple([before[i] for before in kv_before] + [new] for i, new in enumerate((k_p, v_p))) if final else ()
    xp2, stacked = _dense_out(zp["x1"], zp["y_pool"], flat(att_p), flat(ret_p), w["wo"], w["g2"], w["f2i"], w["f2o"],
                              final_norm, final, stack_layers)
    if stacked:
        k_p, v_p = stacked
    xs2, _ = _dense_out(zs["x1"], flat(pool_s), flat(att_s), flat(ret_s), w["wo"], w["g2"], w["f2i"], w["f2o"],
                        final_norm, final)
    heads = lambda a: a.reshape(n_s, t_s, ATT_HEADS, HEAD_DIM)
    return xp2, xs2, (k_p, v_p, pool_new_p, state_p), (heads(zs["ka"]), heads(zs["va"]), pool_new_s, state_s)


def kernel(x_prompt, x_sample, cache_k, cache_v, state_pool, state_ret, page_table, norm_ffn1, ffn1_w_in, ffn1_w_out,
           norm_mix, w_in, pool_w, pool_scale, w_out, norm_ffn2, ffn2_w_in, ffn2_w_out, norm_final):
    depth = w_in.shape[0]
    n_p, t_p, d = x_prompt.shape
    n_s, t_s, _ = x_sample.shape
    n_pool = cache_k.shape[1]
    pages_per_seq = page_table.shape[1]
    assert cache_k.shape[2] == PAGE_SIZE and d % LANES == 0
    pages_t = lambda a: jnp.transpose(a, (0, 1, 3, 4, 2)).reshape(depth * n_pool, ATT_DIM, PAGE_SIZE)
    cache_k, cache_v = pages_t(cache_k), pages_t(cache_v)
    row = lambda a: a.reshape(1, -1).astype(F32)
    final_norm = row(norm_final)
    hp, hs = x_prompt.reshape(n_p * t_p, d), x_sample.reshape(n_s * t_s, d)
    big = {k: a.astype(BF16) for k, a in dict(f1i=ffn1_w_in, f1o=ffn1_w_out, win=w_in, wo=w_out, f2i=ffn2_w_in,
                                              f2o=ffn2_w_out).items()}
    kv_prompt, outs = [], [[] for _ in range(6)]
    for l in range(depth):
        w = dict(g1=row(norm_ffn1[l]), gm=row(norm_mix[l]), g2=row(norm_ffn2[l]),
                 pool_w=jax.scipy.linalg.block_diag(*pool_w[l]).astype(BF16), pool_scale=row(pool_scale[l]),
                 **{k: _Stacked(a, l) for k, a in big.items()})
        cache = dict(k=cache_k, v=cache_v, past_len=pages_per_seq * PAGE_SIZE,
                     page_ids=(page_table.astype(jnp.int32) + l * n_pool).reshape(-1))
        hp, hs, (kp, vp, pp, rp), (ks, vs, ps, rs) = _layer(hp, n_p, hs, n_s, state_pool[l], state_ret[l], cache, w,
                                                            final_norm, l == depth - 1, kv_prompt)
        kv_prompt.append((kp, vp))
        for lst, a in zip(outs, (pp, rp, ks, vs, ps, rs)):
            lst.append(a)
    heads = lambda a: jnp.transpose(a.reshape(depth, n_p, ATT_HEADS, HEAD_DIM, t_p), (0, 1, 4, 2, 3))
    pool_p, ret_p, k_s, v_s, pool_s, ret_s = (jnp.stack(lst) for lst in outs)
    return (hp.reshape(n_p, t_p, d), hs.reshape(n_s, t_s, d), heads(kp), heads(vp), pool_p, ret_p, k_s, v_s, pool_s, ret_s)
```

```python
import functools
import math
from typing import NamedTuple

import numpy as np
import jax
import jax.numpy as jnp
from jax import lax
from jax.experimental import pallas as pl
from jax.experimental.pallas import tpu as pltpu

F32 = jnp.float32
BF16 = jnp.bfloat16

HEAD_DIM = 64
POOL_WINDOWS = (2, 4, 8, 16)
POOL_GRP = HEAD_DIM
POOL_DIM = len(POOL_WINDOWS) * POOL_GRP
POOL_PAST = max(POOL_WINDOWS) - 1
ATT_HEADS = 6
ATT_DIM = ATT_HEADS * HEAD_DIM
RET_HEADS = 6
RET_DIM = RET_HEADS * HEAD_DIM
MOBA_BLOCK = 256
MOBA_TOPK = 3
PAGE_SIZE = 128
RET_CHUNK = 128
EPS = 1e-6

LANES = 128
SUBLANES = 8
BF16_SUBLANES = 16
SLOPE_PARTS = 3
HEADS_PER_TILE = LANES // HEAD_DIM
ATT_PAIRS = ATT_DIM // LANES
RET_PAIRS = RET_DIM // LANES
VMEM_LIMIT = 56 * 1024 * 1024

TOKEN_TILE = 512
FF_CHUNK = 256
POOL_TILE = 512
POOL_HALO = 16
LOG2_E = math.log2(math.e)
PAGES_PER_STEP = 32
PAGE_SLOTS = 2
CAST_ROWS = 256
RET_SEQS = 8
NT_DIMS = (((1,), (1,)), ((), ()))
TN_DIMS = (((0,), (0,)), ((), ()))


def _params(*semantics):
    return pltpu.CompilerParams(dimension_semantics=semantics, vmem_limit_bytes=VMEM_LIMIT)


class _Stacked(NamedTuple):
    stack: jax.Array
    layer: int

    @property
    def shape(self):
        return self.stack.shape[1:]

    def columns(self, lo, hi):
        return self.stack[self.layer, :, lo:hi]


def _resident(a):
    if isinstance(a, _Stacked):
        return pl.BlockSpec((None,) + a.shape, lambda *_: (a.layer,) + (0,) * len(a.shape), pipeline_mode=pl.Buffered(1))
    return pl.BlockSpec(a.shape, lambda *_: (0,) * a.ndim, pipeline_mode=pl.Buffered(1))


def _operand(a):
    return a.stack if isinstance(a, _Stacked) else a


def _cast_kernel(x_ref, o_ref):
    o_ref[...] = x_ref[...].astype(o_ref.dtype)


def _to_bf16(a):
    layers, r, c = a.shape
    rows = math.gcd(r, CAST_ROWS)
    slab = pl.BlockSpec((None, rows, c), lambda l, i: (l, i, 0))
    return pl.pallas_call(_cast_kernel, grid=(layers, r // rows), in_specs=[slab], out_specs=slab,
                          out_shape=jax.ShapeDtypeStruct(a.shape, BF16), compiler_params=_params("parallel", "parallel"),
                          name="to_bf16")(a)


def _rmsnorm(x, g):
    return x * lax.rsqrt(jnp.mean(x * x, axis=-1, keepdims=True) + EPS) * g


def _silu(x):
    return x * (1.0 / (1.0 + jnp.exp(-x)))


def _dot(a, b, dims=None, **kw):
    if dims is None:
        return jnp.dot(a, b, preferred_element_type=F32, **kw)
    return lax.dot_general(a, b, dims, preferred_element_type=F32, **kw)


def _stack_pair(x, lane):
    return jnp.concatenate([jnp.where(lane < HEAD_DIM, x, 0.0), jnp.where(lane >= HEAD_DIM, x, 0.0)], axis=0)


def _unstack_pair(y, lane):
    r = y.shape[0] // 2
    return jnp.where(lane < HEAD_DIM, y[:r], y[r:])


def _alibi_slope(h):
    return 2.0 ** (-8.0 * (h + 1) / ATT_HEADS)


def _swiglu_half_step(x, g_ref, wi_ref, wo_ref):
    d_ff = wo_ref.shape[0]
    h = _rmsnorm(x, g_ref[...]).astype(BF16)
    acc = jnp.zeros(x.shape, F32)
    for c in range(d_ff // FF_CHUNK):
        lo = c * FF_CHUNK
        gate = _dot(h, wi_ref[:, lo:lo + FF_CHUNK])
        up = _dot(h, wi_ref[:, d_ff + lo:d_ff + lo + FF_CHUNK])
        acc = acc + _dot((_silu(gate) * up).astype(BF16), wo_ref[lo:lo + FF_CHUNK, :])
    return x + 0.5 * acc


def _pool_tile(u, carry, pos, w_blockdiag, scale):
    tt = u.shape[0]
    ext = jnp.concatenate([carry, u], axis=0)
    sums, s = [], ext
    for w in POOL_WINDOWS:
        s = s + pltpu.roll(s, w // 2, axis=0)
        sums.append(s[POOL_HALO:])
    lane = lax.broadcasted_iota(jnp.int32, (1, POOL_DIM), 1)
    pooled = None
    for g, w in reversed(list(enumerate(POOL_WINDOWS))):
        mean = sums[g] * (1.0 / jnp.minimum(w, pos + 1).astype(F32))
        pooled = mean if pooled is None else jnp.where(lane < (g + 1) * POOL_GRP, mean, pooled)
    d = (pooled - u).astype(BF16)
    return _dot(d, w_blockdiag) * scale, ext[tt:]


def _dense_in_kernel(x_ref, g1_ref, f1i_ref, f1o_ref, gm_ref, win_ref, *refs, col_offsets, seq_tiles):
    if seq_tiles:
        (wt_ref, pw_ref, ps_ref), refs = refs[:3], refs[3:]
        (kt_ref, vt_ref, ypool_ref, tail_ref, carry_ref), refs = refs[-5:], refs[:-5]
    x1_ref, z_refs = refs[0], refs[1:]
    x1 = _swiglu_half_step(x_ref[...], g1_ref, f1i_ref, f1o_ref)
    x1_ref[...] = x1
    h = _rmsnorm(x1, gm_ref[...]).astype(BF16)
    for z_ref, off in zip(z_refs, col_offsets):
        z_ref[...] = _dot(h, win_ref[:, off:off + z_ref.shape[-1]]).astype(z_ref.dtype)
    if seq_tiles:
        zt = _dot(wt_ref[...], h, NT_DIMS)
        kt_ref[...] = zt[:ATT_DIM]
        vt_ref[...] = zt[ATT_DIM:]
        tile = lax.rem(pl.program_id(0), seq_tiles)
        tm = x_ref.shape[0]

        @pl.when(tile == 0)
        def _():
            carry_ref[...] = jnp.zeros(carry_ref.shape, F32)

        u = _dot(h, win_ref[:, Z_COLUMNS["u"][0]:sum(Z_COLUMNS["u"])])
        pos = tile * tm + lax.broadcasted_iota(jnp.int32, (tm, 1), 0)
        y, carry = _pool_tile(u, carry_ref[...], pos, pw_ref[...], ps_ref[...])
        ypool_ref[...] = y
        carry_ref[...] = carry
        tail_ref[...] = carry


Z_COLUMNS = dict(zip(("u", "qa", "ka", "va", "qr", "kr", "vr", "gr"),
                     ((0, POOL_DIM), (POOL_DIM, ATT_DIM), (POOL_DIM + ATT_DIM, ATT_DIM), (POOL_DIM + 2 * ATT_DIM, ATT_DIM))
                     + tuple((POOL_DIM + 3 * ATT_DIM + i * RET_DIM, RET_DIM) for i in range(4))))


def _dense_in(x, g1, f1i, f1o, gm, win, prompt=None):
    t, d = x.shape
    tm = min(TOKEN_TILE, t)
    assert t % tm == 0 and win.shape[1] == sum(w for _, w in Z_COLUMNS.values()) and f1o.shape[0] % FF_CHUNK == 0
    rows = lambda w: pl.BlockSpec((tm, w), lambda i: (i, 0))
    names = [k for k in Z_COLUMNS if not (prompt and k in ("u", "va"))]
    dtypes = [BF16 if (prompt and k == "ka") else F32 for k in names]
    in_specs = [rows(d)] + [_resident(a) for a in (g1, f1i, f1o, gm, win)]
    args = [x, g1, f1i, f1o, gm, win]
    out_specs = [rows(d)] + [rows(Z_COLUMNS[k][1]) for k in names]
    out_shape = [jax.ShapeDtypeStruct((t, d), F32)] + [jax.ShapeDtypeStruct((t, Z_COLUMNS[k][1]), dt)
                                                      for k, dt in zip(names, dtypes)]
    col_offsets = tuple(Z_COLUMNS[k][0] for k in names)
    scratch, tiles = [], 0
    if prompt:
        seq_len, pool_w, pool_scale = prompt
        assert seq_len % tm == 0 and t % seq_len == 0 and Z_COLUMNS["va"][0] == sum(Z_COLUMNS["ka"])
        tiles = seq_len // tm
        kv_t = win.columns(Z_COLUMNS["ka"][0], sum(Z_COLUMNS["va"])).T
        in_specs += [_resident(a) for a in (kv_t, pool_w, pool_scale)]
        args += [kv_t, pool_w, pool_scale]
        out_specs += [pl.BlockSpec((None, ATT_DIM, tm), lambda i: (i // tiles, 0, i % tiles))] * 2
        out_specs += [rows(POOL_DIM), pl.BlockSpec((None, POOL_HALO, POOL_DIM), lambda i: (i // tiles, 0, 0))]
        out_shape += [jax.ShapeDtypeStruct((t // seq_len, ATT_DIM, seq_len), F32)] * 2
        out_shape += [jax.ShapeDtypeStruct((t, POOL_DIM), F32), jax.ShapeDtypeStruct((t // seq_len, POOL_HALO, POOL_DIM), F32)]
        names += ["kt", "vt", "y_pool", "pool_tail"]
        scratch = [pltpu.VMEM((POOL_HALO, POOL_DIM), F32)]
    outs = pl.pallas_call(
        functools.partial(_dense_in_kernel, col_offsets=col_offsets, seq_tiles=tiles),
        grid=(t // tm,),
        in_specs=in_specs,
        out_specs=out_specs,
        out_shape=out_shape,
        scratch_shapes=scratch,
        compiler_params=_params("arbitrary" if prompt else "parallel"),
        name="dense_in",
    )(*[_operand(a) for a in args])
    return dict(zip(["x1"] + names, outs))


def _dense_out_kernel(x_ref, yp_ref, ya_ref, yr_ref, wo_ref, g2_ref, f2i_ref, f2o_ref, gf_ref, *refs, final, n_gather):
    gather_refs, o_ref, gathered_refs = refs[:n_gather], refs[n_gather], refs[n_gather + 1:]
    off, mix = 0, None
    for y_ref in (yp_ref, ya_ref, yr_ref):
        w = y_ref.shape[-1]
        part = _dot(y_ref[...].astype(BF16), wo_ref[off:off + w, :])
        mix = part if mix is None else mix + part
        off += w
    x = _swiglu_half_step(x_ref[...] + mix, g2_ref, f2i_ref, f2o_ref)
    if final:
        x = _rmsnorm(x, gf_ref[...])
    o_ref[...] = x
    per_out = n_gather // max(len(gathered_refs), 1)
    for k, out_ref in enumerate(gathered_refs):
        for l in range(per_out):
            out_ref[l] = gather_refs[k * per_out + l][...]


def _dense_out(x, yp, ya, yr, wo, g2, f2i, f2o, gf, final, stack_layers=()):
    t, d = x.shape
    tm = min(TOKEN_TILE, t)
    assert t % tm == 0
    rows = lambda w: pl.BlockSpec((tm, w), lambda i: (i, 0))
    in_specs = [rows(d), rows(yp.shape[1]), rows(ya.shape[1]), rows(yr.shape[1])] + [_resident(a) for a in (wo, g2, f2i, f2o, gf)]
    args = [x, yp, ya, yr, wo, g2, f2i, f2o, gf]
    out_specs, out_shape = [rows(d)], [jax.ShapeDtypeStruct((t, d), F32)]
    for group in stack_layers:
        n, c, seq_len = group[0].shape
        tiles = seq_len // tm
        assert seq_len % tm == 0 and n * seq_len == t and all(a.shape == group[0].shape for a in group)
        in_specs += [pl.BlockSpec((None, c, tm), lambda i: (i // tiles, 0, i % tiles))] * len(group)
        args += list(group)
        out_specs.append(pl.BlockSpec((len(group), None, c, tm), lambda i: (0, i // tiles, 0, i % tiles)))
        out_shape.append(jax.ShapeDtypeStruct((len(group), n, c, seq_len), group[0].dtype))
    outs = pl.pallas_call(
        functools.partial(_dense_out_kernel, final=final, n_gather=sum(len(g) for g in stack_layers)),
        grid=(t // tm,),
        in_specs=in_specs,
        out_specs=out_specs,
        out_shape=out_shape,
        compiler_params=_params("parallel"),
        name="dense_out",
    )(*[_operand(a) for a in args])
    return outs[0], tuple(outs[1:])


def _pool_kernel(u_ref, pre_ref, w_ref, scale_ref, y_ref, carry_ref, *, tt, pos0):
    t = pl.program_id(1)

    @pl.when(t == 0)
    def _():
        carry_ref[...] = pre_ref[...]

    pos = pos0 + t * tt + lax.broadcasted_iota(jnp.int32, (tt, 1), 0)
    for b in range(u_ref.shape[0]):
        y_ref[b], carry_ref[b] = _pool_tile(u_ref[b], carry_ref[b], pos, w_ref[...], scale_ref[...])


def _pool(u, prefix, w_blockdiag, scale, pos0):
    n, t, c = u.shape
    tt = min(POOL_TILE, t)
    nb = math.gcd(n, POOL_TILE // tt)
    assert t % tt == 0 and tt % SUBLANES == 0 and POOL_WINDOWS == (2, 4, 8, 16)
    seq = pl.BlockSpec((nb, tt, c), lambda i, j: (i, j, 0))
    return pl.pallas_call(
        functools.partial(_pool_kernel, tt=tt, pos0=pos0),
        grid=(n // nb, t // tt),
        in_specs=[seq, pl.BlockSpec((nb, POOL_HALO, c), lambda i, j: (i, 0, 0)), _resident(w_blockdiag), _resident(scale)],
        out_specs=seq,
        out_shape=jax.ShapeDtypeStruct((n, t, c), F32),
        scratch_shapes=[pltpu.VMEM((nb, POOL_HALO, c), F32)],
        compiler_params=_params("parallel", "arbitrary"),
        name="pool_mixer",
    )(u, prefix, w_blockdiag, scale)


def _prompt_tile(qi, q_ref, k_ref, vt_in_ref, o_ref, kmean_ref, vt_ref, kpos_ref, t_ref, n_blocks):
    tq = q_ref.shape[0]
    rows = HEADS_PER_TILE * tq
    dim = lax.broadcasted_iota(jnp.int32, (LANES, 1), 0)
    key = lax.broadcasted_iota(jnp.int32, (MOBA_BLOCK, rows), 0)
    col = lax.broadcasted_iota(jnp.int32, (1, rows), 1)
    blk = lax.broadcasted_iota(jnp.int32, (kmean_ref.shape[0], 1), 0)
    slopes = [jnp.where(col < tq, _alibi_slope(2 * p) * LOG2_E, _alibi_slope(2 * p + 1) * LOG2_E).astype(F32)
              for p in range(ATT_PAIRS)]

    @pl.when(qi == 0)
    def _():
        kmean_ref[...] = jnp.zeros(kmean_ref.shape, F32)
        ones = jnp.ones((BF16_SUBLANES, MOBA_BLOCK), BF16)
        for j in range(n_blocks):
            keys = slice(j * MOBA_BLOCK, (j + 1) * MOBA_BLOCK)
            kmean_ref[j:j + 1, :] = jnp.mean(k_ref[keys, :].astype(F32), axis=0, keepdims=True)
            for h in range(ATT_HEADS):
                vt_ref[j, h] = jnp.concatenate([vt_in_ref[h * HEAD_DIM:(h + 1) * HEAD_DIM, keys].astype(BF16), ones], axis=0)
        kpos_ref[...] = jnp.where(lax.broadcasted_iota(jnp.int32, kpos_ref.shape, 1) < SLOPE_PARTS,
                                  lax.broadcasted_iota(jnp.int32, kpos_ref.shape, 0), 0).astype(F32).astype(BF16)

    def tile(own):
        causal = key <= lax.rem(col, tq)
        for p in range(ATT_PAIRS):
            cols = slice(p * LANES, (p + 1) * LANES)
            q_t = (q_ref[:, cols] * (HEAD_DIM ** -0.5 * LOG2_E)).T
            qs_t = jnp.concatenate([jnp.where(dim < HEAD_DIM, q_t, 0.0), jnp.where(dim >= HEAD_DIM, q_t, 0.0)], axis=1)
            slope_rows, rest = jnp.zeros((LANES, rows), F32), slopes[p]
            for i in range(SLOPE_PARTS):
                piece = rest.astype(BF16).astype(F32)
                slope_rows, rest = jnp.where(dim == i, piece, slope_rows), rest - piece
            qs_b = jnp.concatenate([qs_t, slope_rows], axis=0).astype(BF16)
            sel = None
            if own > MOBA_TOPK:
                gate = _dot(kmean_ref[:, cols], qs_t, precision=lax.Precision.HIGHEST)
                sel = []
                for j in range(own):
                    gj = gate[j:j + 1, :]
                    ahead = jnp.where(blk < j, jnp.where(gate >= gj, 1.0, 0.0), jnp.where(gate > gj, 1.0, 0.0))
                    sel.append(jnp.sum(jnp.where(blk < own, ahead, 0.0), axis=0, keepdims=True) < MOBA_TOPK)
            block_bias = lambda j: slopes[p] * float((j - own) * MOBA_BLOCK)
            maxima = []
            for j in range(own + 1):
                k_blk = jnp.concatenate([k_ref[j * MOBA_BLOCK:(j + 1) * MOBA_BLOCK, cols], kpos_ref[...]], axis=1)
                t = _dot(k_blk, qs_b)
                if j == own:
                    t = jnp.where(causal, t, -jnp.inf)
                t_ref[p, j] = t
                m = jnp.max(t, axis=0, keepdims=True) + block_bias(j)
                maxima.append(m if sel is None or j == own else jnp.where(sel[j], m, -jnp.inf))
            m_all = functools.reduce(jnp.maximum, maxima)
            acc = [None] * HEADS_PER_TILE
            for j in range(own + 1):
                shift = m_all - block_bias(j)
                if sel is not None and j < own:
                    shift = jnp.where(sel[j], shift, jnp.inf)
                e = jnp.exp2(t_ref[p, j] - shift).astype(BF16)
                for h in range(HEADS_PER_TILE):
                    part = _dot(vt_ref[j, HEADS_PER_TILE * p + h], e[:, h * tq:(h + 1) * tq])
                    acc[h] = part if acc[h] is None else acc[h] + part
            out_t = jnp.concatenate([a[:HEAD_DIM] / a[HEAD_DIM:HEAD_DIM + 1] for a in acc], axis=0)
            o_ref[:, cols] = out_t.T

    for own in range(n_blocks):
        pl.when(qi == own)(functools.partial(tile, own))


def _head_rows(x):
    t = x.shape[0]
    lane_head = lax.broadcasted_iota(jnp.int32, (1, ATT_DIM), 1) // HEAD_DIM
    row_head = lax.broadcasted_iota(jnp.int32, (ATT_HEADS * t, 1), 0) // t
    return jnp.where(lane_head == row_head, jnp.concatenate([x] * ATT_HEADS, axis=0), 0.0)


def _own_head_columns(y, t):
    lane_head = lax.broadcasted_iota(jnp.int32, (1, ATT_DIM), 1) // HEAD_DIM
    out = jnp.zeros((t, ATT_DIM), F32)
    for h in range(ATT_HEADS):
        out = jnp.where(lane_head == h, y[h * t:(h + 1) * t], out)
    return out


def _sample_group(step, n_steps, pt_ref, qs_ref, kns_ref, vns_ref, kc_hbm, vc_hbm, os_ref, k_buf, v_buf, sem, m_ref, l_ref,
                  g_ref, acc_ref, n_past_blocks, steps_per_seq):
    seq = step // steps_per_seq
    c = lax.rem(step, steps_per_seq)
    slot = lax.rem(step, PAGE_SLOTS)
    q_ref, kn_ref, vn_ref, o_ref = qs_ref.at[seq], kns_ref.at[seq], vns_ref.at[seq], os_ref.at[seq]

    def page_copies(of_step):
        to = lax.rem(of_step, PAGE_SLOTS)
        for i in range(PAGES_PER_STEP):
            page = pt_ref[of_step * PAGES_PER_STEP + i]
            yield pltpu.make_async_copy(kc_hbm.at[page], k_buf.at[to, i], sem.at[0, to])
            yield pltpu.make_async_copy(vc_hbm.at[page], v_buf.at[to, i], sem.at[1, to])

    @pl.when(step == 0)
    def _():
        for first in range(PAGE_SLOTS - 1):
            for copy in page_copies(first):
                copy.start()

    @pl.when(step + (PAGE_SLOTS - 1) < n_steps)
    def _():
        for copy in page_copies(step + (PAGE_SLOTS - 1)):
            copy.start()

    for copy in page_copies(step):
        copy.wait()

    k_pages = [k_buf.at[slot, i] for i in range(PAGES_PER_STEP)]
    v_pages = [v_buf.at[slot, i] for i in range(PAGES_PER_STEP)]
    t = q_ref.shape[0]
    rows = ATT_HEADS * t
    pages_per_block = MOBA_BLOCK // PAGE_SIZE
    blocks_per_step = PAGES_PER_STEP // pages_per_block
    past_len = n_past_blocks * MOBA_BLOCK

    row_head = lax.broadcasted_iota(jnp.int32, (rows, 1), 0) // t
    slope = jnp.zeros((rows, 1), F32)
    for h in range(ATT_HEADS):
        slope = jnp.where(row_head == h, _alibi_slope(h), slope)
    q_f32 = _head_rows(q_ref[...] * HEAD_DIM ** -0.5)
    q_all = q_f32.astype(BF16)
    q_split = jnp.concatenate([q_all, (q_f32 - q_all.astype(F32)).astype(BF16)], axis=0)
    kcol = lax.broadcasted_iota(jnp.int32, (1, MOBA_BLOCK), 1)
    blk = lax.broadcasted_iota(jnp.int32, (1, LANES), 1)

    @pl.when(c == 0)
    def _():
        g_ref[...] = jnp.zeros(g_ref.shape, F32)
        m_ref[...] = jnp.full(m_ref.shape, -jnp.inf, F32)
        l_ref[...] = jnp.zeros(l_ref.shape, F32)

    raw = [_dot(q_split, k_ref[...].astype(BF16)) for k_ref in k_pages]
    raw = [r[:rows] + r[rows:] for r in raw]
    for b in range(blocks_per_step):
        j = c * blocks_per_step + b
        pages = range(b * pages_per_block, (b + 1) * pages_per_block)
        s = jnp.concatenate([raw[i] for i in pages], axis=1)
        g_ref[...] = jnp.where(blk == j, jnp.sum(s, axis=1, keepdims=True), g_ref[...])
        s = s + slope * (kcol + (j * MOBA_BLOCK - past_len)).astype(F32)
        m = jnp.max(s, axis=1, keepdims=True)
        e = jnp.exp(s - m)
        m_ref[...] = jnp.where(blk == j, m, m_ref[...])
        l_ref[...] = jnp.where(blk == j, jnp.sum(e, axis=1, keepdims=True), l_ref[...])
        e = e.astype(BF16)
        acc = None
        for n, i in enumerate(pages):
            part = _dot(e[:, n * PAGE_SIZE:(n + 1) * PAGE_SIZE], v_pages[i][...].astype(BF16), NT_DIMS)
            acc = part if acc is None else acc + part
        acc_ref[j] = acc

    @pl.when(c == steps_per_seq - 1)
    def _():
        pad = jnp.zeros((PAGE_SIZE - t, ATT_DIM), F32)
        k_new = jnp.concatenate([kn_ref[...], pad], axis=0).astype(BF16)
        v_new = jnp.concatenate([vn_ref[...], pad], axis=0).astype(BF16)
        ncol = lax.broadcasted_iota(jnp.int32, (1, PAGE_SIZE), 1)
        qrow = lax.rem(lax.broadcasted_iota(jnp.int32, (rows, 1), 0), t)
        s = _dot(q_all, k_new, NT_DIMS) + slope * ncol.astype(F32)
        s = jnp.where(ncol <= qrow, s, -jnp.inf)
        m = jnp.max(s, axis=1, keepdims=True)
        e = jnp.exp(s - m)
        blk_f = blk.astype(F32)
        gate = jnp.where(blk < n_past_blocks, g_ref[...], -jnp.inf)
        picked = jnp.zeros(gate.shape, F32)
        for _ in range(MOBA_TOPK):
            top = jnp.max(gate, axis=1, keepdims=True)
            first = jnp.min(jnp.where(gate == top, blk_f, float(LANES)), axis=1, keepdims=True)
            picked = jnp.where(blk_f == first, 1.0, picked)
            gate = jnp.where(blk_f == first, -jnp.inf, gate)
        m_all = jnp.maximum(m, jnp.max(jnp.where(picked > 0.0, m_ref[...], -jnp.inf), axis=1, keepdims=True))
        w = jnp.where(picked > 0.0, jnp.exp(m_ref[...] - m_all), 0.0)
        w_new = jnp.exp(m - m_all)
        den = w_new * jnp.sum(e, axis=1, keepdims=True) + jnp.sum(w * l_ref[...], axis=1, keepdims=True)
        parts = [w_new * _dot(e.astype(BF16), v_new)] + [w[:, j:j + 1] * acc_ref[j] for j in range(n_past_blocks)]
        while len(parts) > 1:
            parts = [a + b for a, b in zip(parts[::2], parts[1::2])] + parts[len(parts) - len(parts) % 2:]
        o_ref[...] = _own_head_columns(parts[0] / den, t)


def _moba_kernel(pt_ref, q_ref, k_ref, vt_in_ref, qs_ref, kns_ref, vns_ref, kc_hbm, vc_hbm, o_ref, os_ref,
                 kmean_ref, vt_ref, kpos_ref, t_ref, k_buf, v_buf, sem, m_ref, l_ref, g_ref, acc_ref, *,
                 n_blocks, n_past_blocks, steps_per_seq, groups_per_step):
    step = pl.program_id(0)
    n_groups = pl.num_programs(0) * groups_per_step
    for i in range(groups_per_step):
        _sample_group(step * groups_per_step + i, n_groups, pt_ref, qs_ref, kns_ref, vns_ref, kc_hbm, vc_hbm, os_ref,
                      k_buf, v_buf, sem, m_ref, l_ref, g_ref, acc_ref, n_past_blocks, steps_per_seq)
    _prompt_tile(lax.rem(step, n_blocks), q_ref, k_ref, vt_in_ref, o_ref, kmean_ref, vt_ref, kpos_ref, t_ref, n_blocks)


def _moba(q, k, vt, q_s, k_new, v_new, cache_kt, cache_vt, page_ids):
    n, t, c = q.shape
    n_blocks = t // MOBA_BLOCK
    gate_rows = -(-n_blocks // SUBLANES) * SUBLANES
    rows = HEADS_PER_TILE * MOBA_BLOCK
    assert t % MOBA_BLOCK == 0 and c == ATT_DIM and vt.shape == (n, c, t)
    ns, ts, _ = q_s.shape
    pages_per_seq = page_ids.shape[0] // ns
    past_len = pages_per_seq * PAGE_SIZE
    n_past_blocks = past_len // MOBA_BLOCK
    assert past_len % MOBA_BLOCK == 0 and ts <= PAGE_SIZE and ts % SUBLANES == 0
    assert pages_per_seq % PAGES_PER_STEP == 0 and MOBA_TOPK <= n_past_blocks <= LANES
    steps_per_seq = pages_per_seq // PAGES_PER_STEP
    n_steps, n_groups = n * n_blocks, ns * steps_per_seq
    assert n_groups % n_steps == 0 and n_groups >= PAGE_SLOTS - 1
    tile = pl.BlockSpec((None, MOBA_BLOCK, c), lambda i, pt: (i // n_blocks, i % n_blocks, 0))
    whole = lambda a: pl.BlockSpec(a.shape, lambda i, pt: (0,) * a.ndim)
    hbm = pl.BlockSpec(memory_space=pl.ANY)
    stat = pltpu.VMEM((ATT_HEADS * ts, LANES), F32)
    ring = pltpu.VMEM((PAGE_SLOTS, PAGES_PER_STEP, c, PAGE_SIZE), cache_kt.dtype)
    return pl.pallas_call(
        functools.partial(_moba_kernel, n_blocks=n_blocks, n_past_blocks=n_past_blocks, steps_per_seq=steps_per_seq,
                          groups_per_step=n_groups // n_steps),
        grid_spec=pltpu.PrefetchScalarGridSpec(
            num_scalar_prefetch=1,
            grid=(n_steps,),
            in_specs=[tile, pl.BlockSpec((None, t, c), lambda i, pt: (i // n_blocks, 0, 0)),
                      pl.BlockSpec((None, c, t), lambda i, pt: (i // n_blocks, 0, 0)),
                      whole(q_s), whole(k_new), whole(v_new), hbm, hbm],
            out_specs=[tile, whole(q_s)],
            scratch_shapes=[pltpu.VMEM((gate_rows, c), F32),
                            pltpu.VMEM((n_blocks, ATT_HEADS, HEAD_DIM + BF16_SUBLANES, MOBA_BLOCK), BF16),
                            pltpu.VMEM((MOBA_BLOCK, LANES), BF16),
                            pltpu.VMEM((ATT_PAIRS, n_blocks, MOBA_BLOCK, rows), F32),
                            ring, ring, pltpu.SemaphoreType.DMA((2, PAGE_SLOTS)), stat, stat, stat,
                            pltpu.VMEM((n_past_blocks, ATT_HEADS * ts, c), F32)],
        ),
        out_shape=[jax.ShapeDtypeStruct((n, t, c), F32), jax.ShapeDtypeStruct((ns, ts, c), F32)],
        compiler_params=_params("arbitrary"),
        name="moba",
    )(page_ids, q, k, vt, q_s, k_new, v_new, cache_kt, cache_vt)


def _retention_constants(c):
    lg = np.log1p(-np.exp2(-5.0 - np.arange(RET_HEADS, dtype=np.float64)))
    idx = np.arange(c, dtype=np.float64)
    diff = idx[:, None] - idx[None, :]
    d_in = np.where(diff >= 0, np.exp(lg[:, None, None] * np.maximum(diff, 0.0)), 0.0)
    q_dec = np.exp(lg[:, None] * (idx + 1.0))
    k_dec = np.exp(lg[:, None] * (c - 1.0 - idx))
    c_dec = np.exp(lg * c)
    per_lane = lambda a: np.repeat(a.reshape(RET_PAIRS, HEADS_PER_TILE, -1).transpose(0, 2, 1), HEAD_DIM, axis=2)
    block = np.kron(np.eye(HEADS_PER_TILE), np.ones((HEAD_DIM, HEAD_DIM)))
    state_dec = block[None] * np.repeat(c_dec.reshape(RET_PAIRS, HEADS_PER_TILE), HEAD_DIM, axis=1)[:, :, None]
    as_f32 = lambda a: jnp.asarray(a, F32)
    return (as_f32(d_in.reshape(RET_PAIRS, HEADS_PER_TILE * c, c)), as_f32(per_lane(q_dec)), as_f32(per_lane(k_dec)),
            as_f32(state_dec), as_f32(block))


def _retention_kernel(*refs, has_state):
    if has_state:
        q_ref, k_ref, v_ref, g_ref, s0_ref, din_ref, qd_ref, kd_ref, sd_ref, blk_ref, y_ref, sout_ref, s_ref = refs
    else:
        q_ref, k_ref, v_ref, g_ref, din_ref, qd_ref, kd_ref, sd_ref, blk_ref, y_ref, sout_ref, s_ref = refs
    ci = pl.program_id(1)
    n_seqs, c, _ = q_ref.shape

    @pl.when(ci == 0)
    def _():
        if not has_state:
            s_ref[...] = jnp.zeros(s_ref.shape, F32)
            return
        zero = jnp.zeros((HEAD_DIM, HEAD_DIM), F32)
        for b in range(n_seqs):
            for p in range(RET_PAIRS):
                even, odd = s0_ref[b, 2 * p].astype(F32), s0_ref[b, 2 * p + 1].astype(F32)
                s_ref[b, p] = jnp.concatenate([jnp.concatenate([even, zero], axis=1),
                                               jnp.concatenate([zero, odd], axis=1)], axis=0)

    lane = lax.broadcasted_iota(jnp.int32, (c, LANES), 1)
    for b in range(n_seqs):
        for p in range(RET_PAIRS):
            cols = slice(p * LANES, (p + 1) * LANES)
            q = q_ref[b, :, cols]
            k = k_ref[b, :, cols] * HEAD_DIM ** -0.5
            v = v_ref[b, :, cols].astype(BF16)
            state = s_ref[b, p]
            att = _dot(_stack_pair(q, lane).astype(BF16), k.astype(BF16), NT_DIMS) * din_ref[p]
            o = _unstack_pair(_dot(att.astype(BF16), v), lane)
            o = o + _dot((q * qd_ref[p]).astype(BF16), state.astype(BF16))
            s_ref[b, p] = state * sd_ref[p] + _dot((k * kd_ref[p]).astype(BF16), v, TN_DIMS) * blk_ref[...]
            sq = o * o
            even = jnp.sum(jnp.where(lane < HEAD_DIM, sq, 0.0), axis=1, keepdims=True)
            odd = jnp.sum(jnp.where(lane >= HEAD_DIM, sq, 0.0), axis=1, keepdims=True)
            ms = jnp.where(lane < HEAD_DIM, even, odd) * (1.0 / HEAD_DIM)
            y_ref[b, :, cols] = _silu(g_ref[b, :, cols]) * (o * lax.rsqrt(ms + EPS))

    @pl.when(ci == pl.num_programs(1) - 1)
    def _():
        for b in range(n_seqs):
            for p in range(RET_PAIRS):
                state = s_ref[b, p]
                sout_ref[b, 2 * p] = state[:HEAD_DIM, :HEAD_DIM]
                sout_ref[b, 2 * p + 1] = state[HEAD_DIM:, HEAD_DIM:]


def _retention(q, k, v, g, state):
    n, t, d = q.shape
    c = math.gcd(t, RET_CHUNK)
    nb = math.gcd(n, RET_SEQS)
    consts = _retention_constants(c)
    chunk = pl.BlockSpec((nb, c, d), lambda i, j: (i, j, 0))
    st = pl.BlockSpec((nb, RET_HEADS, HEAD_DIM, HEAD_DIM), lambda i, j: (i, 0, 0, 0))
    in_specs, args = [chunk] * 4, [q, k, v, g]
    if state is not None:
        in_specs.append(st)
        args.append(state)
    in_specs += [_resident(a) for a in consts]
    args += list(consts)
    return pl.pallas_call(
        functools.partial(_retention_kernel, has_state=state is not None),
        grid=(n // nb, t // c),
        in_specs=in_specs,
        out_specs=[chunk, st],
        out_shape=[jax.ShapeDtypeStruct((n, t, d), F32), jax.ShapeDtypeStruct((n, RET_HEADS, HEAD_DIM, HEAD_DIM), F32)],
        scratch_shapes=[pltpu.VMEM((nb, RET_PAIRS, LANES, LANES), F32)],
        compiler_params=_params("parallel", "arbitrary"),
        name="retention",
    )(*args)


def _layer(xp, n_p, xs, n_s, pool_prefix, ret_state, cache, w, final_norm, final, kv_before):
    t_p, t_s = xp.shape[0] // n_p, xs.shape[0] // n_s
    assert t_p >= POOL_PAST
    zp = _dense_in(xp, w["g1"], w["f1i"], w["f1o"], w["gm"], w["win"], prompt=(t_p, w["pool_w"], w["pool_scale"]))
    zs = _dense_in(xs, w["g1"], w["f1i"], w["f1o"], w["gm"], w["win"])
    seq_p = lambda a: a.reshape(n_p, t_p, a.shape[-1])
    seq_s = lambda a: a.reshape(n_s, t_s, a.shape[-1])
    att_p, att_s = _moba(seq_p(zp["qa"]), seq_p(zp["ka"]), zp["vt"], seq_s(zs["qa"]), seq_s(zs["ka"]), seq_s(zs["va"]),
                         cache["k"], cache["v"], cache["page_ids"])
    u = seq_s(zs["u"])
    halo = jnp.pad(pool_prefix.astype(F32), ((0, 0), (POOL_HALO - POOL_PAST, 0), (0, 0)))
    pool_s = _pool(u, halo, w["pool_w"], w["pool_scale"], cache["past_len"])
    pool_new_s = jnp.concatenate([pool_prefix.astype(F32), u], axis=1)[:, -POOL_PAST:]
    pool_new_p = zp["pool_tail"][:, POOL_HALO - POOL_PAST:]
    ret_p, state_p = _retention(seq_p(zp["qr"]), seq_p(zp["kr"]), seq_p(zp["vr"]), seq_p(zp["gr"]), None)
    ret_s, state_s = _retention(seq_s(zs["qr"]), seq_s(zs["kr"]), seq_s(zs["vr"]), seq_s(zs["gr"]), ret_state)
    flat = lambda a: a.reshape(-1, a.shape[-1])
    k_p, v_p = zp["kt"], zp["vt"]
    stack_layers = tuple([before[i] for before in kv_before] + [new] for i, new in enumerate((k_p, v_p))) if final else ()
    xp2, stacked = _dense_out(zp["x1"], zp["y_pool"], flat(att_p), flat(ret_p), w["wo"], w["g2"], w["f2i"], w["f2o"],
                              final_norm, final, stack_layers)
    if stacked:
        k_p, v_p = stacked
    xs2, _ = _dense_out(zs["x1"], flat(pool_s), flat(att_s), flat(ret_s), w["wo"], w["g2"], w["f2i"], w["f2o"],
                        final_norm, final)
    heads = lambda a: a.reshape(n_s, t_s, ATT_HEADS, HEAD_DIM)
    return xp2, xs2, (k_p, v_p, pool_new_p, state_p), (heads(zs["ka"]), heads(zs["va"]), pool_new_s, state_s)


def kernel(x_prompt, x_sample, cache_k, cache_v, state_pool, state_ret, page_table, norm_ffn1, ffn1_w_in, ffn1_w_out,
           norm_mix, w_in, pool_w, pool_scale, w_out, norm_ffn2, ffn2_w_in, ffn2_w_out, norm_final):
    depth = w_in.shape[0]
    n_p, t_p, d = x_prompt.shape
    n_s, t_s, _ = x_sample.shape
    n_pool = cache_k.shape[1]
    pages_per_seq = page_table.shape[1]
    assert cache_k.shape[2] == PAGE_SIZE and d % LANES == 0
    pages_t = lambda a: jnp.transpose(a, (0, 1, 3, 4, 2)).reshape(depth * n_pool, ATT_DIM, PAGE_SIZE)
    cache_k, cache_v = pages_t(cache_k), pages_t(cache_v)
    row = lambda a: a.reshape(1, -1).astype(F32)
    final_norm = row(norm_final)
    hp, hs = x_prompt.reshape(n_p * t_p, d), x_sample.reshape(n_s * t_s, d)
    big = {k: _to_bf16(a) for k, a in dict(f1i=ffn1_w_in, f1o=ffn1_w_out, win=w_in, wo=w_out, f2i=ffn2_w_in,
                                           f2o=ffn2_w_out).items()}
    kv_prompt, outs = [], [[] for _ in range(6)]
    for l in range(depth):
        w = dict(g1=row(norm_ffn1[l]), gm=row(norm_mix[l]), g2=row(norm_ffn2[l]),
                 pool_w=jax.scipy.linalg.block_diag(*pool_w[l]).astype(BF16), pool_scale=row(pool_scale[l]),
                 **{k: _Stacked(a, l) for k, a in big.items()})
        cache = dict(k=cache_k, v=cache_v, past_len=pages_per_seq * PAGE_SIZE,
                     page_ids=(page_table.astype(jnp.int32) + l * n_pool).reshape(-1))
        hp, hs, (kp, vp, pp, rp), (ks, vs, ps, rs) = _layer(hp, n_p, hs, n_s, state_pool[l], state_ret[l], cache, w,
                                                            final_norm, l == depth - 1, kv_prompt)
        kv_prompt.append((kp, vp))
        for lst, a in zip(outs, (pp, rp, ks, vs, ps, rs)):
            lst.append(a)
    heads = lambda a: jnp.transpose(a.reshape(depth, n_p, ATT_HEADS, HEAD_DIM, t_p), (0, 1, 4, 2, 3))
    pool_p, ret_p, k_s, v_s, pool_s, ret_s = (jnp.stack(lst) for lst in outs)
    return (hp.reshape(n_p, t_p, d), hs.reshape(n_s, t_s, d), heads(kp), heads(vp), pool_p, ret_p, k_s, v_s, pool_s, ret_s)
```

```python
import functools
import math
from typing import NamedTuple

import numpy as np
import jax
import jax.numpy as jnp
from jax import lax
from jax.experimental import pallas as pl
from jax.experimental.pallas import tpu as pltpu

F32 = jnp.float32
BF16 = jnp.bfloat16

HEAD_DIM = 64
POOL_WINDOWS = (2, 4, 8, 16)
POOL_GRP = HEAD_DIM
POOL_DIM = len(POOL_WINDOWS) * POOL_GRP
POOL_PAST = max(POOL_WINDOWS) - 1
ATT_HEADS = 6
ATT_DIM = ATT_HEADS * HEAD_DIM
RET_HEADS = 6
RET_DIM = RET_HEADS * HEAD_DIM
MOBA_BLOCK = 256
MOBA_TOPK = 3
PAGE_SIZE = 128
RET_CHUNK = 128
EPS = 1e-6

LANES = 128
SUBLANES = 8
BF16_SUBLANES = 16
SLOPE_PARTS = 3
HEADS_PER_TILE = LANES // HEAD_DIM
ATT_PAIRS = ATT_DIM // LANES
RET_PAIRS = RET_DIM // LANES
VMEM_LIMIT = 56 * 1024 * 1024

TOKEN_TILE = 512
FF_CHUNK = 256
POOL_TILE = 512
POOL_HALO = 16
LOG2_E = math.log2(math.e)
PAGES_PER_STEP = 32
PAGE_SLOTS = 2
RET_SEQS = 8
NT_DIMS = (((1,), (1,)), ((), ()))
TN_DIMS = (((0,), (0,)), ((), ()))


def _params(*semantics):
    return pltpu.CompilerParams(dimension_semantics=semantics, vmem_limit_bytes=VMEM_LIMIT)


class _Stacked(NamedTuple):
    stack: jax.Array
    layer: int

    @property
    def shape(self):
        return self.stack.shape[1:]

    def columns(self, lo, hi):
        return self.stack[self.layer, :, lo:hi]


def _resident(a):
    if isinstance(a, _Stacked):
        return pl.BlockSpec((None,) + a.shape, lambda *_: (a.layer,) + (0,) * len(a.shape), pipeline_mode=pl.Buffered(1))
    return pl.BlockSpec(a.shape, lambda *_: (0,) * a.ndim, pipeline_mode=pl.Buffered(1))


def _operand(a):
    return a.stack if isinstance(a, _Stacked) else a


def _rmsnorm(x, g):
    return x * lax.rsqrt(jnp.mean(x * x, axis=-1, keepdims=True) + EPS) * g


def _silu(x):
    return x * (1.0 / (1.0 + jnp.exp(-x)))


def _dot(a, b, dims=None, **kw):
    if dims is None:
        return jnp.dot(a, b, preferred_element_type=F32, **kw)
    return lax.dot_general(a, b, dims, preferred_element_type=F32, **kw)


def _stack_pair(x, lane):
    return jnp.concatenate([jnp.where(lane < HEAD_DIM, x, 0.0), jnp.where(lane >= HEAD_DIM, x, 0.0)], axis=0)


def _unstack_pair(y, lane):
    r = y.shape[0] // 2
    return jnp.where(lane < HEAD_DIM, y[:r], y[r:])


def _alibi_slope(h):
    return 2.0 ** (-8.0 * (h + 1) / ATT_HEADS)


def _swiglu_half_step(x, g_ref, wi_ref, wo_ref):
    d_ff = wo_ref.shape[0]
    h = _rmsnorm(x, g_ref[...]).astype(BF16)
    acc = jnp.zeros(x.shape, F32)
    for c in range(d_ff // FF_CHUNK):
        lo = c * FF_CHUNK
        gate = _dot(h, wi_ref[:, lo:lo + FF_CHUNK])
        up = _dot(h, wi_ref[:, d_ff + lo:d_ff + lo + FF_CHUNK])
        acc = acc + _dot((_silu(gate) * up).astype(BF16), wo_ref[lo:lo + FF_CHUNK, :])
    return x + 0.5 * acc


def _pool_tile(u, carry, pos, w_blockdiag, scale):
    tt = u.shape[0]
    ext = jnp.concatenate([carry, u], axis=0)
    sums, s = [], ext
    for w in POOL_WINDOWS:
        s = s + pltpu.roll(s, w // 2, axis=0)
        sums.append(s[POOL_HALO:])
    lane = lax.broadcasted_iota(jnp.int32, (1, POOL_DIM), 1)
    pooled = None
    for g, w in reversed(list(enumerate(POOL_WINDOWS))):
        mean = sums[g] * (1.0 / jnp.minimum(w, pos + 1).astype(F32))
        pooled = mean if pooled is None else jnp.where(lane < (g + 1) * POOL_GRP, mean, pooled)
    d = (pooled - u).astype(BF16)
    return _dot(d, w_blockdiag) * scale, ext[tt:]


def _dense_in_kernel(x_ref, g1_ref, f1i_ref, f1o_ref, gm_ref, win_ref, *refs, col_offsets, seq_tiles):
    if seq_tiles:
        (wt_ref, pw_ref, ps_ref), refs = refs[:3], refs[3:]
        (kt_ref, vt_ref, ypool_ref, tail_ref, carry_ref), refs = refs[-5:], refs[:-5]
    x1_ref, z_refs = refs[0], refs[1:]
    x1 = _swiglu_half_step(x_ref[...], g1_ref, f1i_ref, f1o_ref)
    x1_ref[...] = x1
    h = _rmsnorm(x1, gm_ref[...]).astype(BF16)
    for z_ref, off in zip(z_refs, col_offsets):
        z_ref[...] = _dot(h, win_ref[:, off:off + z_ref.shape[-1]]).astype(z_ref.dtype)
    if seq_tiles:
        zt = _dot(wt_ref[...], h, NT_DIMS)
        kt_ref[...] = zt[:ATT_DIM]
        vt_ref[...] = zt[ATT_DIM:]
        tile = lax.rem(pl.program_id(0), seq_tiles)
        tm = x_ref.shape[0]

        @pl.when(tile == 0)
        def _():
            carry_ref[...] = jnp.zeros(carry_ref.shape, F32)

        u = _dot(h, win_ref[:, Z_COLUMNS["u"][0]:sum(Z_COLUMNS["u"])])
        pos = tile * tm + lax.broadcasted_iota(jnp.int32, (tm, 1), 0)
        y, carry = _pool_tile(u, carry_ref[...], pos, pw_ref[...], ps_ref[...])
        ypool_ref[...] = y
        carry_ref[...] = carry
        tail_ref[...] = carry


Z_COLUMNS = dict(zip(("u", "qa", "ka", "va", "qr", "kr", "vr", "gr"),
                     ((0, POOL_DIM), (POOL_DIM, ATT_DIM), (POOL_DIM + ATT_DIM, ATT_DIM), (POOL_DIM + 2 * ATT_DIM, ATT_DIM))
                     + tuple((POOL_DIM + 3 * ATT_DIM + i * RET_DIM, RET_DIM) for i in range(4))))


def _dense_in(x, g1, f1i, f1o, gm, win, prompt=None):
    t, d = x.shape
    tm = min(TOKEN_TILE, t)
    assert t % tm == 0 and win.shape[1] == sum(w for _, w in Z_COLUMNS.values()) and f1o.shape[0] % FF_CHUNK == 0
    rows = lambda w: pl.BlockSpec((tm, w), lambda i: (i, 0))
    names = [k for k in Z_COLUMNS if not (prompt and k in ("u", "va"))]
    dtypes = [BF16 if (prompt and k == "ka") else F32 for k in names]
    in_specs = [rows(d)] + [_resident(a) for a in (g1, f1i, f1o, gm, win)]
    args = [x, g1, f1i, f1o, gm, win]
    out_specs = [rows(d)] + [rows(Z_COLUMNS[k][1]) for k in names]
    out_shape = [jax.ShapeDtypeStruct((t, d), F32)] + [jax.ShapeDtypeStruct((t, Z_COLUMNS[k][1]), dt)
                                                      for k, dt in zip(names, dtypes)]
    col_offsets = tuple(Z_COLUMNS[k][0] for k in names)
    scratch, tiles = [], 0
    if prompt:
        seq_len, pool_w, pool_scale = prompt
        assert seq_len % tm == 0 and t % seq_len == 0 and Z_COLUMNS["va"][0] == sum(Z_COLUMNS["ka"])
        tiles = seq_len // tm
        kv_t = win.columns(Z_COLUMNS["ka"][0], sum(Z_COLUMNS["va"])).T
        in_specs += [_resident(a) for a in (kv_t, pool_w, pool_scale)]
        args += [kv_t, pool_w, pool_scale]
        out_specs += [pl.BlockSpec((None, ATT_DIM, tm), lambda i: (i // tiles, 0, i % tiles))] * 2
        out_specs += [rows(POOL_DIM), pl.BlockSpec((None, POOL_HALO, POOL_DIM), lambda i: (i // tiles, 0, 0))]
        out_shape += [jax.ShapeDtypeStruct((t // seq_len, ATT_DIM, seq_len), F32)] * 2
        out_shape += [jax.ShapeDtypeStruct((t, POOL_DIM), F32), jax.ShapeDtypeStruct((t // seq_len, POOL_HALO, POOL_DIM), F32)]
        names += ["kt", "vt", "y_pool", "pool_tail"]
        scratch = [pltpu.VMEM((POOL_HALO, POOL_DIM), F32)]
    outs = pl.pallas_call(
        functools.partial(_dense_in_kernel, col_offsets=col_offsets, seq_tiles=tiles),
        grid=(t // tm,),
        in_specs=in_specs,
        out_specs=out_specs,
        out_shape=out_shape,
        scratch_shapes=scratch,
        compiler_params=_params("arbitrary" if prompt else "parallel"),
        name="dense_in",
    )(*[_operand(a) for a in args])
    return dict(zip(["x1"] + names, outs))


def _dense_out_kernel(x_ref, yp_ref, ya_ref, yr_ref, wo_ref, g2_ref, f2i_ref, f2o_ref, gf_ref, *refs, final, n_gather):
    gather_refs, o_ref, gathered_refs = refs[:n_gather], refs[n_gather], refs[n_gather + 1:]
    off, mix = 0, None
    for y_ref in (yp_ref, ya_ref, yr_ref):
        w = y_ref.shape[-1]
        part = _dot(y_ref[...].astype(BF16), wo_ref[off:off + w, :])
        mix = part if mix is None else mix + part
        off += w
    x = _swiglu_half_step(x_ref[...] + mix, g2_ref, f2i_ref, f2o_ref)
    if final:
        x = _rmsnorm(x, gf_ref[...])
    o_ref[...] = x
    per_out = n_gather // max(len(gathered_refs), 1)
    for k, out_ref in enumerate(gathered_refs):
        for l in range(per_out):
            out_ref[l] = gather_refs[k * per_out + l][...]


def _dense_out(x, yp, ya, yr, wo, g2, f2i, f2o, gf, final, stack_layers=()):
    t, d = x.shape
    tm = min(TOKEN_TILE, t)
    assert t % tm == 0
    rows = lambda w: pl.BlockSpec((tm, w), lambda i: (i, 0))
    in_specs = [rows(d), rows(yp.shape[1]), rows(ya.shape[1]), rows(yr.shape[1])] + [_resident(a) for a in (wo, g2, f2i, f2o, gf)]
    args = [x, yp, ya, yr, wo, g2, f2i, f2o, gf]
    out_specs, out_shape = [rows(d)], [jax.ShapeDtypeStruct((t, d), F32)]
    for group in stack_layers:
        n, c, seq_len = group[0].shape
        tiles = seq_len // tm
        assert seq_len % tm == 0 and n * seq_len == t and all(a.shape == group[0].shape for a in group)
        in_specs += [pl.BlockSpec((None, c, tm), lambda i: (i // tiles, 0, i % tiles))] * len(group)
        args += list(group)
        out_specs.append(pl.BlockSpec((len(group), None, c, tm), lambda i: (0, i // tiles, 0, i % tiles)))
        out_shape.append(jax.ShapeDtypeStruct((len(group), n, c, seq_len), group[0].dtype))
    outs = pl.pallas_call(
        functools.partial(_dense_out_kernel, final=final, n_gather=sum(len(g) for g in stack_layers)),
        grid=(t // tm,),
        in_specs=in_specs,
        out_specs=out_specs,
        out_shape=out_shape,
        compiler_params=_params("parallel"),
        name="dense_out",
    )(*[_operand(a) for a in args])
    return outs[0], tuple(outs[1:])


def _pool_kernel(u_ref, pre_ref, w_ref, scale_ref, y_ref, carry_ref, *, tt, pos0):
    t = pl.program_id(1)

    @pl.when(t == 0)
    def _():
        carry_ref[...] = pre_ref[...]

    pos = pos0 + t * tt + lax.broadcasted_iota(jnp.int32, (tt, 1), 0)
    for b in range(u_ref.shape[0]):
        y_ref[b], carry_ref[b] = _pool_tile(u_ref[b], carry_ref[b], pos, w_ref[...], scale_ref[...])


def _pool(u, prefix, w_blockdiag, scale, pos0):
    n, t, c = u.shape
    tt = min(POOL_TILE, t)
    nb = math.gcd(n, POOL_TILE // tt)
    assert t % tt == 0 and tt % SUBLANES == 0 and POOL_WINDOWS == (2, 4, 8, 16)
    seq = pl.BlockSpec((nb, tt, c), lambda i, j: (i, j, 0))
    return pl.pallas_call(
        functools.partial(_pool_kernel, tt=tt, pos0=pos0),
        grid=(n // nb, t // tt),
        in_specs=[seq, pl.BlockSpec((nb, POOL_HALO, c), lambda i, j: (i, 0, 0)), _resident(w_blockdiag), _resident(scale)],
        out_specs=seq,
        out_shape=jax.ShapeDtypeStruct((n, t, c), F32),
        scratch_shapes=[pltpu.VMEM((nb, POOL_HALO, c), F32)],
        compiler_params=_params("parallel", "arbitrary"),
        name="pool_mixer",
    )(u, prefix, w_blockdiag, scale)


def _prompt_tile(qi, q_ref, k_ref, vt_in_ref, o_ref, kmean_ref, vt_ref, kpos_ref, t_ref, n_blocks):
    tq = q_ref.shape[0]
    rows = HEADS_PER_TILE * tq
    dim = lax.broadcasted_iota(jnp.int32, (LANES, 1), 0)
    key = lax.broadcasted_iota(jnp.int32, (MOBA_BLOCK, rows), 0)
    col = lax.broadcasted_iota(jnp.int32, (1, rows), 1)
    blk = lax.broadcasted_iota(jnp.int32, (kmean_ref.shape[0], 1), 0)
    slopes = [jnp.where(col < tq, _alibi_slope(2 * p) * LOG2_E, _alibi_slope(2 * p + 1) * LOG2_E).astype(F32)
              for p in range(ATT_PAIRS)]

    @pl.when(qi == 0)
    def _():
        kmean_ref[...] = jnp.zeros(kmean_ref.shape, F32)
        ones = jnp.ones((BF16_SUBLANES, MOBA_BLOCK), BF16)
        for j in range(n_blocks):
            keys = slice(j * MOBA_BLOCK, (j + 1) * MOBA_BLOCK)
            kmean_ref[j:j + 1, :] = jnp.mean(k_ref[keys, :].astype(F32), axis=0, keepdims=True)
            for h in range(ATT_HEADS):
                vt_ref[j, h] = jnp.concatenate([vt_in_ref[h * HEAD_DIM:(h + 1) * HEAD_DIM, keys].astype(BF16), ones], axis=0)
        kpos_ref[...] = jnp.where(lax.broadcasted_iota(jnp.int32, kpos_ref.shape, 1) < SLOPE_PARTS,
                                  lax.broadcasted_iota(jnp.int32, kpos_ref.shape, 0), 0).astype(F32).astype(BF16)

    def tile(own):
        causal = key <= lax.rem(col, tq)
        for p in range(ATT_PAIRS):
            cols = slice(p * LANES, (p + 1) * LANES)
            q_t = (q_ref[:, cols] * (HEAD_DIM ** -0.5 * LOG2_E)).T
            qs_t = jnp.concatenate([jnp.where(dim < HEAD_DIM, q_t, 0.0), jnp.where(dim >= HEAD_DIM, q_t, 0.0)], axis=1)
            slope_rows, rest = jnp.zeros((LANES, rows), F32), slopes[p]
            for i in range(SLOPE_PARTS):
                piece = rest.astype(BF16).astype(F32)
                slope_rows, rest = jnp.where(dim == i, piece, slope_rows), rest - piece
            qs_b = jnp.concatenate([qs_t, slope_rows], axis=0).astype(BF16)
            sel = None
            if own > MOBA_TOPK:
                gate = _dot(kmean_ref[:, cols], qs_t, precision=lax.Precision.HIGHEST)
                sel = []
                for j in range(own):
                    gj = gate[j:j + 1, :]
                    ahead = jnp.where(blk < j, jnp.where(gate >= gj, 1.0, 0.0), jnp.where(gate > gj, 1.0, 0.0))
                    sel.append(jnp.sum(jnp.where(blk < own, ahead, 0.0), axis=0, keepdims=True) < MOBA_TOPK)
            block_bias = lambda j: slopes[p] * float((j - own) * MOBA_BLOCK)
            maxima = []
            for j in range(own + 1):
                k_blk = jnp.concatenate([k_ref[j * MOBA_BLOCK:(j + 1) * MOBA_BLOCK, cols], kpos_ref[...]], axis=1)
                t = _dot(k_blk, qs_b)
                if j == own:
                    t = jnp.where(causal, t, -jnp.inf)
                t_ref[p, j] = t
                m = jnp.max(t, axis=0, keepdims=True) + block_bias(j)
                maxima.append(m if sel is None or j == own else jnp.where(sel[j], m, -jnp.inf))
            m_all = functools.reduce(jnp.maximum, maxima)
            acc = [None] * HEADS_PER_TILE
            for j in range(own + 1):
                shift = m_all - block_bias(j)
                if sel is not None and j < own:
                    shift = jnp.where(sel[j], shift, jnp.inf)
                e = jnp.exp2(t_ref[p, j] - shift).astype(BF16)
                for h in range(HEADS_PER_TILE):
                    part = _dot(vt_ref[j, HEADS_PER_TILE * p + h], e[:, h * tq:(h + 1) * tq])
                    acc[h] = part if acc[h] is None else acc[h] + part
            out_t = jnp.concatenate([a[:HEAD_DIM] / a[HEAD_DIM:HEAD_DIM + 1] for a in acc], axis=0)
            o_ref[:, cols] = out_t.T

    for own in range(n_blocks):
        pl.when(qi == own)(functools.partial(tile, own))


def _head_rows(x):
    t = x.shape[0]
    lane_head = lax.broadcasted_iota(jnp.int32, (1, ATT_DIM), 1) // HEAD_DIM
    row_head = lax.broadcasted_iota(jnp.int32, (ATT_HEADS * t, 1), 0) // t
    return jnp.where(lane_head == row_head, jnp.concatenate([x] * ATT_HEADS, axis=0), 0.0)


def _own_head_columns(y, t):
    lane_head = lax.broadcasted_iota(jnp.int32, (1, ATT_DIM), 1) // HEAD_DIM
    out = jnp.zeros((t, ATT_DIM), F32)
    for h in range(ATT_HEADS):
        out = jnp.where(lane_head == h, y[h * t:(h + 1) * t], out)
    return out


def _sample_group(step, n_steps, pt_ref, qs_ref, kns_ref, vns_ref, kc_hbm, vc_hbm, os_ref, k_buf, v_buf, sem, m_ref, l_ref,
                  g_ref, acc_ref, n_past_blocks, steps_per_seq):
    seq = step // steps_per_seq
    c = lax.rem(step, steps_per_seq)
    slot = lax.rem(step, PAGE_SLOTS)
    q_ref, kn_ref, vn_ref, o_ref = qs_ref.at[seq], kns_ref.at[seq], vns_ref.at[seq], os_ref.at[seq]

    def page_copies(of_step):
        to = lax.rem(of_step, PAGE_SLOTS)
        for i in range(PAGES_PER_STEP):
            page = pt_ref[of_step * PAGES_PER_STEP + i]
            yield pltpu.make_async_copy(kc_hbm.at[page], k_buf.at[to, i], sem.at[0, to])
            yield pltpu.make_async_copy(vc_hbm.at[page], v_buf.at[to, i], sem.at[1, to])

    @pl.when(step == 0)
    def _():
        for first in range(PAGE_SLOTS - 1):
            for copy in page_copies(first):
                copy.start()

    @pl.when(step + (PAGE_SLOTS - 1) < n_steps)
    def _():
        for copy in page_copies(step + (PAGE_SLOTS - 1)):
            copy.start()

    for copy in page_copies(step):
        copy.wait()

    k_pages = [k_buf.at[slot, i] for i in range(PAGES_PER_STEP)]
    v_pages = [v_buf.at[slot, i] for i in range(PAGES_PER_STEP)]
    t = q_ref.shape[0]
    rows = ATT_HEADS * t
    pages_per_block = MOBA_BLOCK // PAGE_SIZE
    blocks_per_step = PAGES_PER_STEP // pages_per_block
    past_len = n_past_blocks * MOBA_BLOCK

    row_head = lax.broadcasted_iota(jnp.int32, (rows, 1), 0) // t
    slope = jnp.zeros((rows, 1), F32)
    for h in range(ATT_HEADS):
        slope = jnp.where(row_head == h, _alibi_slope(h), slope)
    q_f32 = _head_rows(q_ref[...] * HEAD_DIM ** -0.5)
    q_all = q_f32.astype(BF16)
    q_split = jnp.concatenate([q_all, (q_f32 - q_all.astype(F32)).astype(BF16)], axis=0)
    kcol = lax.broadcasted_iota(jnp.int32, (1, MOBA_BLOCK), 1)
    blk = lax.broadcasted_iota(jnp.int32, (1, LANES), 1)

    @pl.when(c == 0)
    def _():
        g_ref[...] = jnp.zeros(g_ref.shape, F32)
        m_ref[...] = jnp.full(m_ref.shape, -jnp.inf, F32)
        l_ref[...] = jnp.zeros(l_ref.shape, F32)

    raw = [_dot(q_split, k_ref[...].astype(BF16)) for k_ref in k_pages]
    raw = [r[:rows] + r[rows:] for r in raw]
    for b in range(blocks_per_step):
        j = c * blocks_per_step + b
        pages = range(b * pages_per_block, (b + 1) * pages_per_block)
        s = jnp.concatenate([raw[i] for i in pages], axis=1)
        g_ref[...] = jnp.where(blk == j, jnp.sum(s, axis=1, keepdims=True), g_ref[...])
        s = s + slope * (kcol + (j * MOBA_BLOCK - past_len)).astype(F32)
        m = jnp.max(s, axis=1, keepdims=True)
        e = jnp.exp(s - m)
        m_ref[...] = jnp.where(blk == j, m, m_ref[...])
        l_ref[...] = jnp.where(blk == j, jnp.sum(e, axis=1, keepdims=True), l_ref[...])
        e = e.astype(BF16)
        acc = None
        for n, i in enumerate(pages):
            part = _dot(e[:, n * PAGE_SIZE:(n + 1) * PAGE_SIZE], v_pages[i][...].astype(BF16), NT_DIMS)
            acc = part if acc is None else acc + part
        acc_ref[j] = acc

    @pl.when(c == steps_per_seq - 1)
    def _():
        pad = jnp.zeros((PAGE_SIZE - t, ATT_DIM), F32)
        k_new = jnp.concatenate([kn_ref[...], pad], axis=0).astype(BF16)
        v_new = jnp.concatenate([vn_ref[...], pad], axis=0).astype(BF16)
        ncol = lax.broadcasted_iota(jnp.int32, (1, PAGE_SIZE), 1)
        qrow = lax.rem(lax.broadcasted_iota(jnp.int32, (rows, 1), 0), t)
        s = _dot(q_all, k_new, NT_DIMS) + slope * ncol.astype(F32)
        s = jnp.where(ncol <= qrow, s, -jnp.inf)
        m = jnp.max(s, axis=1, keepdims=True)
        e = jnp.exp(s - m)
        blk_f = blk.astype(F32)
        gate = jnp.where(blk < n_past_blocks, g_ref[...], -jnp.inf)
        picked = jnp.zeros(gate.shape, F32)
        for _ in range(MOBA_TOPK):
            top = jnp.max(gate, axis=1, keepdims=True)
            first = jnp.min(jnp.where(gate == top, blk_f, float(LANES)), axis=1, keepdims=True)
            picked = jnp.where(blk_f == first, 1.0, picked)
            gate = jnp.where(blk_f == first, -jnp.inf, gate)
        m_all = jnp.maximum(m, jnp.max(jnp.where(picked > 0.0, m_ref[...], -jnp.inf), axis=1, keepdims=True))
        w = jnp.where(picked > 0.0, jnp.exp(m_ref[...] - m_all), 0.0)
        w_new = jnp.exp(m - m_all)
        den = w_new * jnp.sum(e, axis=1, keepdims=True) + jnp.sum(w * l_ref[...], axis=1, keepdims=True)
        parts = [w_new * _dot(e.astype(BF16), v_new)] + [w[:, j:j + 1] * acc_ref[j] for j in range(n_past_blocks)]
        while len(parts) > 1:
            parts = [a + b for a, b in zip(parts[::2], parts[1::2])] + parts[len(parts) - len(parts) % 2:]
        o_ref[...] = _own_head_columns(parts[0] / den, t)


def _moba_kernel(pt_ref, q_ref, k_ref, vt_in_ref, qs_ref, kns_ref, vns_ref, kc_hbm, vc_hbm, o_ref, os_ref,
                 kmean_ref, vt_ref, kpos_ref, t_ref, k_buf, v_buf, sem, m_ref, l_ref, g_ref, acc_ref, *,
                 n_blocks, n_past_blocks, steps_per_seq, groups_per_step):
    step = pl.program_id(0)
    n_groups = pl.num_programs(0) * groups_per_step
    for i in range(groups_per_step):
        _sample_group(step * groups_per_step + i, n_groups, pt_ref, qs_ref, kns_ref, vns_ref, kc_hbm, vc_hbm, os_ref,
                      k_buf, v_buf, sem, m_ref, l_ref, g_ref, acc_ref, n_past_blocks, steps_per_seq)
    _prompt_tile(lax.rem(step, n_blocks), q_ref, k_ref, vt_in_ref, o_ref, kmean_ref, vt_ref, kpos_ref, t_ref, n_blocks)


def _moba(q, k, vt, q_s, k_new, v_new, cache_kt, cache_vt, page_ids):
    n, t, c = q.shape
    n_blocks = t // MOBA_BLOCK
    gate_rows = -(-n_blocks // SUBLANES) * SUBLANES
    rows = HEADS_PER_TILE * MOBA_BLOCK
    assert t % MOBA_BLOCK == 0 and c == ATT_DIM and vt.shape == (n, c, t)
    ns, ts, _ = q_s.shape
    pages_per_seq = page_ids.shape[0] // ns
    past_len = pages_per_seq * PAGE_SIZE
    n_past_blocks = past_len // MOBA_BLOCK
    assert past_len % MOBA_BLOCK == 0 and ts <= PAGE_SIZE and ts % SUBLANES == 0
    assert pages_per_seq % PAGES_PER_STEP == 0 and MOBA_TOPK <= n_past_blocks <= LANES
    steps_per_seq = pages_per_seq // PAGES_PER_STEP
    n_steps, n_groups = n * n_blocks, ns * steps_per_seq
    assert n_groups % n_steps == 0 and n_groups >= PAGE_SLOTS - 1
    tile = pl.BlockSpec((None, MOBA_BLOCK, c), lambda i, pt: (i // n_blocks, i % n_blocks, 0))
    whole = lambda a: pl.BlockSpec(a.shape, lambda i, pt: (0,) * a.ndim)
    hbm = pl.BlockSpec(memory_space=pl.ANY)
    stat = pltpu.VMEM((ATT_HEADS * ts, LANES), F32)
    ring = pltpu.VMEM((PAGE_SLOTS, PAGES_PER_STEP, c, PAGE_SIZE), cache_kt.dtype)
    return pl.pallas_call(
        functools.partial(_moba_kernel, n_blocks=n_blocks, n_past_blocks=n_past_blocks, steps_per_seq=steps_per_seq,
                          groups_per_step=n_groups // n_steps),
        grid_spec=pltpu.PrefetchScalarGridSpec(
            num_scalar_prefetch=1,
            grid=(n_steps,),
            in_specs=[tile, pl.BlockSpec((None, t, c), lambda i, pt: (i // n_blocks, 0, 0)),
                      pl.BlockSpec((None, c, t), lambda i, pt: (i // n_blocks, 0, 0)),
                      whole(q_s), whole(k_new), whole(v_new), hbm, hbm],
            out_specs=[tile, whole(q_s)],
            scratch_shapes=[pltpu.VMEM((gate_rows, c), F32),
                            pltpu.VMEM((n_blocks, ATT_HEADS, HEAD_DIM + BF16_SUBLANES, MOBA_BLOCK), BF16),
                            pltpu.VMEM((MOBA_BLOCK, LANES), BF16),
                            pltpu.VMEM((ATT_PAIRS, n_blocks, MOBA_BLOCK, rows), F32),
                            ring, ring, pltpu.SemaphoreType.DMA((2, PAGE_SLOTS)), stat, stat, stat,
                            pltpu.VMEM((n_past_blocks, ATT_HEADS * ts, c), F32)],
        ),
        out_shape=[jax.ShapeDtypeStruct((n, t, c), F32), jax.ShapeDtypeStruct((ns, ts, c), F32)],
        compiler_params=_params("arbitrary"),
        name="moba",
    )(page_ids, q, k, vt, q_s, k_new, v_new, cache_kt, cache_vt)


def _retention_constants(c):
    lg = np.log1p(-np.exp2(-5.0 - np.arange(RET_HEADS, dtype=np.float64)))
    idx = np.arange(c, dtype=np.float64)
    diff = idx[:, None] - idx[None, :]
    d_in = np.where(diff >= 0, np.exp(lg[:, None, None] * np.maximum(diff, 0.0)), 0.0)
    q_dec = np.exp(lg[:, None] * (idx + 1.0))
    k_dec = np.exp(lg[:, None] * (c - 1.0 - idx))
    c_dec = np.exp(lg * c)
    per_lane = lambda a: np.repeat(a.reshape(RET_PAIRS, HEADS_PER_TILE, -1).transpose(0, 2, 1), HEAD_DIM, axis=2)
    block = np.kron(np.eye(HEADS_PER_TILE), np.ones((HEAD_DIM, HEAD_DIM)))
    state_dec = block[None] * np.repeat(c_dec.reshape(RET_PAIRS, HEADS_PER_TILE), HEAD_DIM, axis=1)[:, :, None]
    as_f32 = lambda a: jnp.asarray(a, F32)
    return (as_f32(d_in.reshape(RET_PAIRS, HEADS_PER_TILE * c, c)), as_f32(per_lane(q_dec)), as_f32(per_lane(k_dec)),
            as_f32(state_dec), as_f32(block))


def _retention_kernel(*refs, has_state):
    if has_state:
        q_ref, k_ref, v_ref, g_ref, s0_ref, din_ref, qd_ref, kd_ref, sd_ref, blk_ref, y_ref, sout_ref, s_ref = refs
    else:
        q_ref, k_ref, v_ref, g_ref, din_ref, qd_ref, kd_ref, sd_ref, blk_ref, y_ref, sout_ref, s_ref = refs
    ci = pl.program_id(1)
    n_seqs, c, _ = q_ref.shape

    @pl.when(ci == 0)
    def _():
        if not has_state:
            s_ref[...] = jnp.zeros(s_ref.shape, F32)
            return
        zero = jnp.zeros((HEAD_DIM, HEAD_DIM), F32)
        for b in range(n_seqs):
            for p in range(RET_PAIRS):
                even, odd = s0_ref[b, 2 * p].astype(F32), s0_ref[b, 2 * p + 1].astype(F32)
                s_ref[b, p] = jnp.concatenate([jnp.concatenate([even, zero], axis=1),
                                               jnp.concatenate([zero, odd], axis=1)], axis=0)

    lane = lax.broadcasted_iota(jnp.int32, (c, LANES), 1)
    for b in range(n_seqs):
        for p in range(RET_PAIRS):
            cols = slice(p * LANES, (p + 1) * LANES)
            q = q_ref[b, :, cols]
            k = k_ref[b, :, cols] * HEAD_DIM ** -0.5
            v = v_ref[b, :, cols].astype(BF16)
            state = s_ref[b, p]
            att = _dot(_stack_pair(q, lane).astype(BF16), k.astype(BF16), NT_DIMS) * din_ref[p]
            o = _unstack_pair(_dot(att.astype(BF16), v), lane)
            o = o + _dot((q * qd_ref[p]).astype(BF16), state.astype(BF16))
            s_ref[b, p] = state * sd_ref[p] + _dot((k * kd_ref[p]).astype(BF16), v, TN_DIMS) * blk_ref[...]
            sq = o * o
            even = jnp.sum(jnp.where(lane < HEAD_DIM, sq, 0.0), axis=1, keepdims=True)
            odd = jnp.sum(jnp.where(lane >= HEAD_DIM, sq, 0.0), axis=1, keepdims=True)
            ms = jnp.where(lane < HEAD_DIM, even, odd) * (1.0 / HEAD_DIM)
            y_ref[b, :, cols] = _silu(g_ref[b, :, cols]) * (o * lax.rsqrt(ms + EPS))

    @pl.when(ci == pl.num_programs(1) - 1)
    def _():
        for b in range(n_seqs):
            for p in range(RET_PAIRS):
                state = s_ref[b, p]
                sout_ref[b, 2 * p] = state[:HEAD_DIM, :HEAD_DIM]
                sout_ref[b, 2 * p + 1] = state[HEAD_DIM:, HEAD_DIM:]


def _retention(q, k, v, g, state):
    n, t, d = q.shape
    c = math.gcd(t, RET_CHUNK)
    nb = math.gcd(n, RET_SEQS)
    consts = _retention_constants(c)
    chunk = pl.BlockSpec((nb, c, d), lambda i, j: (i, j, 0))
    st = pl.BlockSpec((nb, RET_HEADS, HEAD_DIM, HEAD_DIM), lambda i, j: (i, 0, 0, 0))
    in_specs, args = [chunk] * 4, [q, k, v, g]
    if state is not None:
        in_specs.append(pl.BlockSpec((None, nb, RET_HEADS, HEAD_DIM, HEAD_DIM), lambda i, j: (state.layer, i, 0, 0, 0)))
        args.append(state.stack)
    in_specs += [_resident(a) for a in consts]
    args += list(consts)
    return pl.pallas_call(
        functools.partial(_retention_kernel, has_state=state is not None),
        grid=(n // nb, t // c),
        in_specs=in_specs,
        out_specs=[chunk, st],
        out_shape=[jax.ShapeDtypeStruct((n, t, d), F32), jax.ShapeDtypeStruct((n, RET_HEADS, HEAD_DIM, HEAD_DIM), F32)],
        scratch_shapes=[pltpu.VMEM((nb, RET_PAIRS, LANES, LANES), F32)],
        compiler_params=_params("parallel", "arbitrary"),
        name="retention",
    )(*args)


def _layer(xp, n_p, xs, n_s, pool_prefix, ret_state, cache, w, final_norm, final, kv_before):
    t_p, t_s = xp.shape[0] // n_p, xs.shape[0] // n_s
    assert t_p >= POOL_PAST
    zp = _dense_in(xp, w["g1"], w["f1i"], w["f1o"], w["gm"], w["win"], prompt=(t_p, w["pool_w"], w["pool_scale"]))
    zs = _dense_in(xs, w["g1"], w["f1i"], w["f1o"], w["gm"], w["win"])
    seq_p = lambda a: a.reshape(n_p, t_p, a.shape[-1])
    seq_s = lambda a: a.reshape(n_s, t_s, a.shape[-1])
    att_p, att_s = _moba(seq_p(zp["qa"]), seq_p(zp["ka"]), zp["vt"], seq_s(zs["qa"]), seq_s(zs["ka"]), seq_s(zs["va"]),
                         cache["k"], cache["v"], cache["page_ids"])
    u = seq_s(zs["u"])
    halo = jnp.pad(pool_prefix.astype(F32), ((0, 0), (POOL_HALO - POOL_PAST, 0), (0, 0)))
    pool_s = _pool(u, halo, w["pool_w"], w["pool_scale"], cache["past_len"])
    pool_new_s = jnp.concatenate([pool_prefix.astype(F32), u], axis=1)[:, -POOL_PAST:]
    pool_new_p = zp["pool_tail"][:, POOL_HALO - POOL_PAST:]
    ret_p, state_p = _retention(seq_p(zp["qr"]), seq_p(zp["kr"]), seq_p(zp["vr"]), seq_p(zp["gr"]), None)
    ret_s, state_s = _retention(seq_s(zs["qr"]), seq_s(zs["kr"]), seq_s(zs["vr"]), seq_s(zs["gr"]), ret_state)
    flat = lambda a: a.reshape(-1, a.shape[-1])
    k_p, v_p = zp["kt"], zp["vt"]
    stack_layers = tuple([before[i] for before in kv_before] + [new] for i, new in enumerate((k_p, v_p))) if final else ()
    xp2, stacked = _dense_out(zp["x1"], zp["y_pool"], flat(att_p), flat(ret_p), w["wo"], w["g2"], w["f2i"], w["f2o"],
                              final_norm, final, stack_layers)
    if stacked:
        k_p, v_p = stacked
    xs2, _ = _dense_out(zs["x1"], flat(pool_s), flat(att_s), flat(ret_s), w["wo"], w["g2"], w["f2i"], w["f2o"],
                        final_norm, final)
    heads = lambda a: a.reshape(n_s, t_s, ATT_HEADS, HEAD_DIM)
    return xp2, xs2, (k_p, v_p, pool_new_p, state_p), (heads(zs["ka"]), heads(zs["va"]), pool_new_s, state_s)


def kernel(x_prompt, x_sample, cache_k, cache_v, state_pool, state_ret, page_table, norm_ffn1, ffn1_w_in, ffn1_w_out,
           norm_mix, w_in, pool_w, pool_scale, w_out, norm_ffn2, ffn2_w_in, ffn2_w_out, norm_final):
    depth = w_in.shape[0]
    n_p, t_p, d = x_prompt.shape
    n_s, t_s, _ = x_sample.shape
    n_pool = cache_k.shape[1]
    pages_per_seq = page_table.shape[1]
    assert cache_k.shape[2] == PAGE_SIZE and d % LANES == 0
    pages_t = lambda a: jnp.transpose(a, (0, 1, 3, 4, 2)).reshape(depth * n_pool, ATT_DIM, PAGE_SIZE)
    cache_k, cache_v = pages_t(cache_k), pages_t(cache_v)
    row = lambda a: a.reshape(1, -1).astype(F32)
    final_norm = row(norm_final)
    hp, hs = x_prompt.reshape(n_p * t_p, d), x_sample.reshape(n_s * t_s, d)
    big = {k: a.astype(BF16) for k, a in dict(f1i=ffn1_w_in, f1o=ffn1_w_out, win=w_in, wo=w_out, f2i=ffn2_w_in,
                                              f2o=ffn2_w_out).items()}
    kv_prompt, outs = [], [[] for _ in range(6)]
    for l in range(depth):
        w = dict(g1=row(norm_ffn1[l]), gm=row(norm_mix[l]), g2=row(norm_ffn2[l]),
                 pool_w=jax.scipy.linalg.block_diag(*pool_w[l]).astype(BF16), pool_scale=row(pool_scale[l]),
                 **{k: _Stacked(a, l) for k, a in big.items()})
        cache = dict(k=cache_k, v=cache_v, past_len=pages_per_seq * PAGE_SIZE,
                     page_ids=(page_table.astype(jnp.int32) + l * n_pool).reshape(-1))
        hp, hs, (kp, vp, pp, rp), (ks, vs, ps, rs) = _layer(hp, n_p, hs, n_s, state_pool[l], _Stacked(state_ret, l), cache, w,
                                                            final_norm, l == depth - 1, kv_prompt)
        kv_prompt.append((kp, vp))
        for lst, a in zip(outs, (pp, rp, ks, vs, ps, rs)):
            lst.append(a)
    heads = lambda a: jnp.transpose(a.reshape(depth, n_p, ATT_HEADS, HEAD_DIM, t_p), (0, 1, 4, 2, 3))
    pool_p, ret_p, k_s, v_s, pool_s, ret_s = (jnp.stack(lst) for lst in outs)
    return (hp.reshape(n_p, t_p, d), hs.reshape(n_s, t_s, d), heads(kp), heads(vp), pool_p, ret_p, k_s, v_s, pool_s, ret_s)
```

```python
import functools
import math
from typing import NamedTuple

import numpy as np
import jax
import jax.numpy as jnp
from jax import lax
from jax.experimental import pallas as pl
from jax.experimental.pallas import tpu as pltpu

F32 = jnp.float32
BF16 = jnp.bfloat16

HEAD_DIM = 64
POOL_WINDOWS = (2, 4, 8, 16)
POOL_GRP = HEAD_DIM
POOL_DIM = len(POOL_WINDOWS) * POOL_GRP
POOL_PAST = max(POOL_WINDOWS) - 1
ATT_HEADS = 6
ATT_DIM = ATT_HEADS * HEAD_DIM
RET_HEADS = 6
RET_DIM = RET_HEADS * HEAD_DIM
MOBA_BLOCK = 256
MOBA_TOPK = 3
PAGE_SIZE = 128
RET_CHUNK = 128
EPS = 1e-6

LANES = 128
SUBLANES = 8
BF16_SUBLANES = 16
SLOPE_PARTS = 3
HEADS_PER_TILE = LANES // HEAD_DIM
ATT_PAIRS = ATT_DIM // LANES
RET_PAIRS = RET_DIM // LANES
VMEM_LIMIT = 56 * 1024 * 1024

TOKEN_TILE = 512
FF_CHUNK = 256
POOL_TILE = 512
POOL_HALO = 16
LOG2_E = math.log2(math.e)
PAGES_PER_STEP = 32
PAGE_SLOTS = 2
RET_SEQS = 8
NT_DIMS = (((1,), (1,)), ((), ()))
TN_DIMS = (((0,), (0,)), ((), ()))


def _params(*semantics):
    return pltpu.CompilerParams(dimension_semantics=semantics, vmem_limit_bytes=VMEM_LIMIT)


class _Stacked(NamedTuple):
    stack: jax.Array
    layer: int

    @property
    def shape(self):
        return self.stack.shape[1:]

    def columns(self, lo, hi):
        return self.stack[self.layer, :, lo:hi]


def _resident(a):
    if isinstance(a, _Stacked):
        return pl.BlockSpec((None,) + a.shape, lambda *_: (a.layer,) + (0,) * len(a.shape), pipeline_mode=pl.Buffered(1))
    return pl.BlockSpec(a.shape, lambda *_: (0,) * a.ndim, pipeline_mode=pl.Buffered(1))


def _operand(a):
    return a.stack if isinstance(a, _Stacked) else a


def _rmsnorm(x, g):
    return x * lax.rsqrt(jnp.mean(x * x, axis=-1, keepdims=True) + EPS) * g


def _silu(x):
    return x * (1.0 / (1.0 + jnp.exp(-x)))


def _dot(a, b, dims=None, **kw):
    if dims is None:
        return jnp.dot(a, b, preferred_element_type=F32, **kw)
    return lax.dot_general(a, b, dims, preferred_element_type=F32, **kw)


def _stack_pair(x, lane):
    return jnp.concatenate([jnp.where(lane < HEAD_DIM, x, 0.0), jnp.where(lane >= HEAD_DIM, x, 0.0)], axis=0)


def _unstack_pair(y, lane):
    r = y.shape[0] // 2
    return jnp.where(lane < HEAD_DIM, y[:r], y[r:])


def _alibi_slope(h):
    return 2.0 ** (-8.0 * (h + 1) / ATT_HEADS)


def _swiglu_half_step(x, g_ref, wi_ref, wo_ref):
    d_ff = wo_ref.shape[0]
    h = _rmsnorm(x, g_ref[...]).astype(BF16)
    acc = jnp.zeros(x.shape, F32)
    for c in range(d_ff // FF_CHUNK):
        lo = c * FF_CHUNK
        gate = _dot(h, wi_ref[:, lo:lo + FF_CHUNK])
        up = _dot(h, wi_ref[:, d_ff + lo:d_ff + lo + FF_CHUNK])
        acc = acc + _dot((_silu(gate) * up).astype(BF16), wo_ref[lo:lo + FF_CHUNK, :])
    return x + 0.5 * acc


def _pool_tile(u, carry, pos, w_blockdiag, scale):
    tt = u.shape[0]
    ext = jnp.concatenate([carry, u], axis=0)
    sums, s = [], ext
    for w in POOL_WINDOWS:
        s = s + pltpu.roll(s, w // 2, axis=0)
        sums.append(s[POOL_HALO:])
    lane = lax.broadcasted_iota(jnp.int32, (1, POOL_DIM), 1)
    pooled = None
    for g, w in reversed(list(enumerate(POOL_WINDOWS))):
        mean = sums[g] * (1.0 / jnp.minimum(w, pos + 1).astype(F32))
        pooled = mean if pooled is None else jnp.where(lane < (g + 1) * POOL_GRP, mean, pooled)
    d = (pooled - u).astype(BF16)
    return _dot(d, w_blockdiag) * scale, ext[tt:]


def _dense_in_kernel(x_ref, g1_ref, f1i_ref, f1o_ref, gm_ref, win_ref, *refs, col_offsets, seq_tiles):
    if seq_tiles:
        (wt_ref, pw_ref, ps_ref), refs = refs[:3], refs[3:]
        (kt_ref, vt_ref, ypool_ref, tail_ref, carry_ref), refs = refs[-5:], refs[:-5]
    x1_ref, z_refs = refs[0], refs[1:]
    if seq_tiles:
        tile = lax.rem(pl.program_id(0), seq_tiles)
        tm = x_ref.shape[0]

        @pl.when(tile == 0)
        def _():
            carry_ref[...] = jnp.zeros(carry_ref.shape, F32)

    x1 = _swiglu_half_step(x_ref[...], g1_ref, f1i_ref, f1o_ref)
    x1_ref[...] = x1
    h = _rmsnorm(x1, gm_ref[...]).astype(BF16)
    if seq_tiles:
        u = _dot(h, win_ref[:, Z_COLUMNS["u"][0]:sum(Z_COLUMNS["u"])])
        pos = tile * tm + lax.broadcasted_iota(jnp.int32, (tm, 1), 0)
        y, carry = _pool_tile(u, carry_ref[...], pos, pw_ref[...], ps_ref[...])
        ypool_ref[...] = y
        carry_ref[...] = carry
        tail_ref[...] = carry
    for z_ref, off in zip(z_refs, col_offsets):
        z_ref[...] = _dot(h, win_ref[:, off:off + z_ref.shape[-1]]).astype(z_ref.dtype)
    if seq_tiles:
        zt = _dot(wt_ref[...], h, NT_DIMS)
        kt_ref[...] = zt[:ATT_DIM]
        vt_ref[...] = zt[ATT_DIM:]


Z_COLUMNS = dict(zip(("u", "qa", "ka", "va", "qr", "kr", "vr", "gr"),
                     ((0, POOL_DIM), (POOL_DIM, ATT_DIM), (POOL_DIM + ATT_DIM, ATT_DIM), (POOL_DIM + 2 * ATT_DIM, ATT_DIM))
                     + tuple((POOL_DIM + 3 * ATT_DIM + i * RET_DIM, RET_DIM) for i in range(4))))


def _dense_in(x, g1, f1i, f1o, gm, win, prompt=None):
    t, d = x.shape
    tm = min(TOKEN_TILE, t)
    assert t % tm == 0 and win.shape[1] == sum(w for _, w in Z_COLUMNS.values()) and f1o.shape[0] % FF_CHUNK == 0
    rows = lambda w: pl.BlockSpec((tm, w), lambda i: (i, 0))
    names = [k for k in Z_COLUMNS if not (prompt and k in ("u", "va"))]
    dtypes = [BF16 if (prompt and k == "ka") else F32 for k in names]
    in_specs = [rows(d)] + [_resident(a) for a in (g1, f1i, f1o, gm, win)]
    args = [x, g1, f1i, f1o, gm, win]
    out_specs = [rows(d)] + [rows(Z_COLUMNS[k][1]) for k in names]
    out_shape = [jax.ShapeDtypeStruct((t, d), F32)] + [jax.ShapeDtypeStruct((t, Z_COLUMNS[k][1]), dt)
                                                      for k, dt in zip(names, dtypes)]
    col_offsets = tuple(Z_COLUMNS[k][0] for k in names)
    scratch, tiles = [], 0
    if prompt:
        seq_len, pool_w, pool_scale = prompt
        assert seq_len % tm == 0 and t % seq_len == 0 and Z_COLUMNS["va"][0] == sum(Z_COLUMNS["ka"])
        tiles = seq_len // tm
        kv_t = win.columns(Z_COLUMNS["ka"][0], sum(Z_COLUMNS["va"])).T
        in_specs += [_resident(a) for a in (kv_t, pool_w, pool_scale)]
        args += [kv_t, pool_w, pool_scale]
        out_specs += [pl.BlockSpec((None, ATT_DIM, tm), lambda i: (i // tiles, 0, i % tiles))] * 2
        out_specs += [rows(POOL_DIM), pl.BlockSpec((None, POOL_HALO, POOL_DIM), lambda i: (i // tiles, 0, 0))]
        out_shape += [jax.ShapeDtypeStruct((t // seq_len, ATT_DIM, seq_len), F32)] * 2
        out_shape += [jax.ShapeDtypeStruct((t, POOL_DIM), F32), jax.ShapeDtypeStruct((t // seq_len, POOL_HALO, POOL_DIM), F32)]
        names += ["kt", "vt", "y_pool", "pool_tail"]
        scratch = [pltpu.VMEM((POOL_HALO, POOL_DIM), F32)]
    outs = pl.pallas_call(
        functools.partial(_dense_in_kernel, col_offsets=col_offsets, seq_tiles=tiles),
        grid=(t // tm,),
        in_specs=in_specs,
        out_specs=out_specs,
        out_shape=out_shape,
        scratch_shapes=scratch,
        compiler_params=_params("arbitrary" if prompt else "parallel"),
        name="dense_in",
    )(*[_operand(a) for a in args])
    return dict(zip(["x1"] + names, outs))


def _dense_out_kernel(x_ref, yp_ref, ya_ref, yr_ref, wo_ref, g2_ref, f2i_ref, f2o_ref, gf_ref, *refs, final, n_gather):
    gather_refs, o_ref, gathered_refs = refs[:n_gather], refs[n_gather], refs[n_gather + 1:]
    per_out = n_gather // max(len(gathered_refs), 1)
    for k, out_ref in enumerate(gathered_refs):
        for l in range(per_out):
            out_ref[l] = gather_refs[k * per_out + l][...]
    off, mix = 0, None
    for y_ref in (yp_ref, ya_ref, yr_ref):
        w = y_ref.shape[-1]
        part = _dot(y_ref[...].astype(BF16), wo_ref[off:off + w, :])
        mix = part if mix is None else mix + part
        off += w
    x = _swiglu_half_step(x_ref[...] + mix, g2_ref, f2i_ref, f2o_ref)
    if final:
        x = _rmsnorm(x, gf_ref[...])
    o_ref[...] = x


def _dense_out(x, yp, ya, yr, wo, g2, f2i, f2o, gf, final, stack_layers=()):
    t, d = x.shape
    tm = min(TOKEN_TILE, t)
    assert t % tm == 0
    rows = lambda w: pl.BlockSpec((tm, w), lambda i: (i, 0))
    in_specs = [rows(d), rows(yp.shape[1]), rows(ya.shape[1]), rows(yr.shape[1])] + [_resident(a) for a in (wo, g2, f2i, f2o, gf)]
    args = [x, yp, ya, yr, wo, g2, f2i, f2o, gf]
    out_specs, out_shape = [rows(d)], [jax.ShapeDtypeStruct((t, d), F32)]
    for group in stack_layers:
        n, c, seq_len = group[0].shape
        tiles = seq_len // tm
        assert seq_len % tm == 0 and n * seq_len == t and all(a.shape == group[0].shape for a in group)
        in_specs += [pl.BlockSpec((None, c, tm), lambda i: (i // tiles, 0, i % tiles))] * len(group)
        args += list(group)
        out_specs.append(pl.BlockSpec((len(group), None, c, tm), lambda i: (0, i // tiles, 0, i % tiles)))
        out_shape.append(jax.ShapeDtypeStruct((len(group), n, c, seq_len), group[0].dtype))
    outs = pl.pallas_call(
        functools.partial(_dense_out_kernel, final=final, n_gather=sum(len(g) for g in stack_layers)),
        grid=(t // tm,),
        in_specs=in_specs,
        out_specs=out_specs,
        out_shape=out_shape,
        compiler_params=_params("parallel"),
        name="dense_out",
    )(*[_operand(a) for a in args])
    return outs[0], tuple(outs[1:])


def _pool_kernel(u_ref, pre_ref, w_ref, scale_ref, y_ref, carry_ref, *, tt, pos0):
    t = pl.program_id(1)

    @pl.when(t == 0)
    def _():
        carry_ref[...] = pre_ref[...]

    pos = pos0 + t * tt + lax.broadcasted_iota(jnp.int32, (tt, 1), 0)
    for b in range(u_ref.shape[0]):
        y_ref[b], carry_ref[b] = _pool_tile(u_ref[b], carry_ref[b], pos, w_ref[...], scale_ref[...])


def _pool(u, prefix, w_blockdiag, scale, pos0):
    n, t, c = u.shape
    tt = min(POOL_TILE, t)
    nb = math.gcd(n, POOL_TILE // tt)
    assert t % tt == 0 and tt % SUBLANES == 0 and POOL_WINDOWS == (2, 4, 8, 16)
    seq = pl.BlockSpec((nb, tt, c), lambda i, j: (i, j, 0))
    return pl.pallas_call(
        functools.partial(_pool_kernel, tt=tt, pos0=pos0),
        grid=(n // nb, t // tt),
        in_specs=[seq, pl.BlockSpec((nb, POOL_HALO, c), lambda i, j: (i, 0, 0)), _resident(w_blockdiag), _resident(scale)],
        out_specs=seq,
        out_shape=jax.ShapeDtypeStruct((n, t, c), F32),
        scratch_shapes=[pltpu.VMEM((nb, POOL_HALO, c), F32)],
        compiler_params=_params("parallel", "arbitrary"),
        name="pool_mixer",
    )(u, prefix, w_blockdiag, scale)


def _prompt_tile(qi, q_ref, k_ref, vt_in_ref, o_ref, kmean_ref, vt_ref, kpos_ref, t_ref, n_blocks):
    tq = q_ref.shape[0]
    rows = HEADS_PER_TILE * tq
    dim = lax.broadcasted_iota(jnp.int32, (LANES, 1), 0)
    key = lax.broadcasted_iota(jnp.int32, (MOBA_BLOCK, rows), 0)
    col = lax.broadcasted_iota(jnp.int32, (1, rows), 1)
    blk = lax.broadcasted_iota(jnp.int32, (kmean_ref.shape[0], 1), 0)
    slopes = [jnp.where(col < tq, _alibi_slope(2 * p) * LOG2_E, _alibi_slope(2 * p + 1) * LOG2_E).astype(F32)
              for p in range(ATT_PAIRS)]

    @pl.when(qi == 0)
    def _():
        kmean_ref[...] = jnp.zeros(kmean_ref.shape, F32)
        ones = jnp.ones((BF16_SUBLANES, MOBA_BLOCK), BF16)
        for j in range(n_blocks):
            keys = slice(j * MOBA_BLOCK, (j + 1) * MOBA_BLOCK)
            kmean_ref[j:j + 1, :] = jnp.mean(k_ref[keys, :].astype(F32), axis=0, keepdims=True)
            for h in range(ATT_HEADS):
                vt_ref[j, h] = jnp.concatenate([vt_in_ref[h * HEAD_DIM:(h + 1) * HEAD_DIM, keys].astype(BF16), ones], axis=0)
        kpos_ref[...] = jnp.where(lax.broadcasted_iota(jnp.int32, kpos_ref.shape, 1) < SLOPE_PARTS,
                                  lax.broadcasted_iota(jnp.int32, kpos_ref.shape, 0), 0).astype(F32).astype(BF16)

    def tile(own):
        causal = key <= lax.rem(col, tq)
        for p in range(ATT_PAIRS):
            cols = slice(p * LANES, (p + 1) * LANES)
            q_t = (q_ref[:, cols] * (HEAD_DIM ** -0.5 * LOG2_E)).T
            qs_t = jnp.concatenate([jnp.where(dim < HEAD_DIM, q_t, 0.0), jnp.where(dim >= HEAD_DIM, q_t, 0.0)], axis=1)
            slope_rows, rest = jnp.zeros((LANES, rows), F32), slopes[p]
            for i in range(SLOPE_PARTS):
                piece = rest.astype(BF16).astype(F32)
                slope_rows, rest = jnp.where(dim == i, piece, slope_rows), rest - piece
            qs_b = jnp.concatenate([qs_t, slope_rows], axis=0).astype(BF16)
            sel = None
            if own > MOBA_TOPK:
                gate = _dot(kmean_ref[:, cols], qs_t, precision=lax.Precision.HIGHEST)
                sel = []
                for j in range(own):
                    gj = gate[j:j + 1, :]
                    ahead = jnp.where(blk < j, jnp.where(gate >= gj, 1.0, 0.0), jnp.where(gate > gj, 1.0, 0.0))
                    sel.append(jnp.sum(jnp.where(blk < own, ahead, 0.0), axis=0, keepdims=True) < MOBA_TOPK)
            block_bias = lambda j: slopes[p] * float((j - own) * MOBA_BLOCK)
            maxima = []
            for j in range(own + 1):
                k_blk = jnp.concatenate([k_ref[j * MOBA_BLOCK:(j + 1) * MOBA_BLOCK, cols], kpos_ref[...]], axis=1)
                t = _dot(k_blk, qs_b)
                if j == own:
                    t = jnp.where(causal, t, -jnp.inf)
                t_ref[p, j] = t
                m = jnp.max(t, axis=0, keepdims=True) + block_bias(j)
                maxima.append(m if sel is None or j == own else jnp.where(sel[j], m, -jnp.inf))
            m_all = functools.reduce(jnp.maximum, maxima)
            acc = [None] * HEADS_PER_TILE
            for j in range(own + 1):
                shift = m_all - block_bias(j)
                if sel is not None and j < own:
                    shift = jnp.where(sel[j], shift, jnp.inf)
                e = jnp.exp2(t_ref[p, j] - shift).astype(BF16)
                for h in range(HEADS_PER_TILE):
                    part = _dot(vt_ref[j, HEADS_PER_TILE * p + h], e[:, h * tq:(h + 1) * tq])
                    acc[h] = part if acc[h] is None else acc[h] + part
            out_t = jnp.concatenate([a[:HEAD_DIM] / a[HEAD_DIM:HEAD_DIM + 1] for a in acc], axis=0)
            o_ref[:, cols] = out_t.T

    for own in range(n_blocks):
        pl.when(qi == own)(functools.partial(tile, own))


def _head_rows(x):
    t = x.shape[0]
    lane_head = lax.broadcasted_iota(jnp.int32, (1, ATT_DIM), 1) // HEAD_DIM
    row_head = lax.broadcasted_iota(jnp.int32, (ATT_HEADS * t, 1), 0) // t
    return jnp.where(lane_head == row_head, jnp.concatenate([x] * ATT_HEADS, axis=0), 0.0)


def _own_head_columns(y, t):
    lane_head = lax.broadcasted_iota(jnp.int32, (1, ATT_DIM), 1) // HEAD_DIM
    out = jnp.zeros((t, ATT_DIM), F32)
    for h in range(ATT_HEADS):
        out = jnp.where(lane_head == h, y[h * t:(h + 1) * t], out)
    return out


def _sample_group(step, n_steps, pt_ref, qs_ref, kns_ref, vns_ref, kc_hbm, vc_hbm, os_ref, k_buf, v_buf, sem, m_ref, l_ref,
                  g_ref, acc_ref, n_past_blocks, steps_per_seq):
    seq = step // steps_per_seq
    c = lax.rem(step, steps_per_seq)
    slot = lax.rem(step, PAGE_SLOTS)
    q_ref, kn_ref, vn_ref, o_ref = qs_ref.at[seq], kns_ref.at[seq], vns_ref.at[seq], os_ref.at[seq]

    def page_copies(of_step):
        to = lax.rem(of_step, PAGE_SLOTS)
        for i in range(PAGES_PER_STEP):
            page = pt_ref[of_step * PAGES_PER_STEP + i]
            yield pltpu.make_async_copy(kc_hbm.at[page], k_buf.at[to, i], sem.at[0, to])
            yield pltpu.make_async_copy(vc_hbm.at[page], v_buf.at[to, i], sem.at[1, to])

    @pl.when(step == 0)
    def _():
        for first in range(PAGE_SLOTS - 1):
            for copy in page_copies(first):
                copy.start()

    @pl.when(step + (PAGE_SLOTS - 1) < n_steps)
    def _():
        for copy in page_copies(step + (PAGE_SLOTS - 1)):
            copy.start()

    for copy in page_copies(step):
        copy.wait()

    k_pages = [k_buf.at[slot, i] for i in range(PAGES_PER_STEP)]
    v_pages = [v_buf.at[slot, i] for i in range(PAGES_PER_STEP)]
    t = q_ref.shape[0]
    rows = ATT_HEADS * t
    pages_per_block = MOBA_BLOCK // PAGE_SIZE
    blocks_per_step = PAGES_PER_STEP // pages_per_block
    past_len = n_past_blocks * MOBA_BLOCK

    row_head = lax.broadcasted_iota(jnp.int32, (rows, 1), 0) // t
    slope = jnp.zeros((rows, 1), F32)
    for h in range(ATT_HEADS):
        slope = jnp.where(row_head == h, _alibi_slope(h), slope)
    q_f32 = _head_rows(q_ref[...] * HEAD_DIM ** -0.5)
    q_all = q_f32.astype(BF16)
    q_split = jnp.concatenate([q_all, (q_f32 - q_all.astype(F32)).astype(BF16)], axis=0)
    kcol = lax.broadcasted_iota(jnp.int32, (1, MOBA_BLOCK), 1)
    blk = lax.broadcasted_iota(jnp.int32, (1, LANES), 1)

    @pl.when(c == 0)
    def _():
        g_ref[...] = jnp.zeros(g_ref.shape, F32)
        m_ref[...] = jnp.full(m_ref.shape, -jnp.inf, F32)
        l_ref[...] = jnp.zeros(l_ref.shape, F32)

    raw = [_dot(q_split, k_ref[...].astype(BF16)) for k_ref in k_pages]
    raw = [r[:rows] + r[rows:] for r in raw]
    for b in range(blocks_per_step):
        j = c * blocks_per_step + b
        pages = range(b * pages_per_block, (b + 1) * pages_per_block)
        s = jnp.concatenate([raw[i] for i in pages], axis=1)
        g_ref[...] = jnp.where(blk == j, jnp.sum(s, axis=1, keepdims=True), g_ref[...])
        s = s + slope * (kcol + (j * MOBA_BLOCK - past_len)).astype(F32)
        m = jnp.max(s, axis=1, keepdims=True)
        e = jnp.exp(s - m)
        m_ref[...] = jnp.where(blk == j, m, m_ref[...])
        l_ref[...] = jnp.where(blk == j, jnp.sum(e, axis=1, keepdims=True), l_ref[...])
        e = e.astype(BF16)
        acc = None
        for n, i in enumerate(pages):
            part = _dot(e[:, n * PAGE_SIZE:(n + 1) * PAGE_SIZE], v_pages[i][...].astype(BF16), NT_DIMS)
            acc = part if acc is None else acc + part
        acc_ref[j] = acc

    @pl.when(c == steps_per_seq - 1)
    def _():
        pad = jnp.zeros((PAGE_SIZE - t, ATT_DIM), F32)
        k_new = jnp.concatenate([kn_ref[...], pad], axis=0).astype(BF16)
        v_new = jnp.concatenate([vn_ref[...], pad], axis=0).astype(BF16)
        ncol = lax.broadcasted_iota(jnp.int32, (1, PAGE_SIZE), 1)
        qrow = lax.rem(lax.broadcasted_iota(jnp.int32, (rows, 1), 0), t)
        s = _dot(q_all, k_new, NT_DIMS) + slope * ncol.astype(F32)
        s = jnp.where(ncol <= qrow, s, -jnp.inf)
        m = jnp.max(s, axis=1, keepdims=True)
        e = jnp.exp(s - m)
        blk_f = blk.astype(F32)
        gate = jnp.where(blk < n_past_blocks, g_ref[...], -jnp.inf)
        picked = jnp.zeros(gate.shape, F32)
        for _ in range(MOBA_TOPK):
            top = jnp.max(gate, axis=1, keepdims=True)
            first = jnp.min(jnp.where(gate == top, blk_f, float(LANES)), axis=1, keepdims=True)
            picked = jnp.where(blk_f == first, 1.0, picked)
            gate = jnp.where(blk_f == first, -jnp.inf, gate)
        m_all = jnp.maximum(m, jnp.max(jnp.where(picked > 0.0, m_ref[...], -jnp.inf), axis=1, keepdims=True))
        w = jnp.where(picked > 0.0, jnp.exp(m_ref[...] - m_all), 0.0)
        w_new = jnp.exp(m - m_all)
        den = w_new * jnp.sum(e, axis=1, keepdims=True) + jnp.sum(w * l_ref[...], axis=1, keepdims=True)
        parts = [w_new * _dot(e.astype(BF16), v_new)] + [w[:, j:j + 1] * acc_ref[j] for j in range(n_past_blocks)]
        while len(parts) > 1:
            parts = [a + b for a, b in zip(parts[::2], parts[1::2])] + parts[len(parts) - len(parts) % 2:]
        o_ref[...] = _own_head_columns(parts[0] / den, t)


def _moba_kernel(pt_ref, q_ref, k_ref, vt_in_ref, qs_ref, kns_ref, vns_ref, kc_hbm, vc_hbm, o_ref, os_ref,
                 kmean_ref, vt_ref, kpos_ref, t_ref, k_buf, v_buf, sem, m_ref, l_ref, g_ref, acc_ref, *,
                 n_blocks, n_past_blocks, steps_per_seq, groups_per_step):
    step = pl.program_id(0)
    n_groups = pl.num_programs(0) * groups_per_step
    for i in range(groups_per_step):
        _sample_group(step * groups_per_step + i, n_groups, pt_ref, qs_ref, kns_ref, vns_ref, kc_hbm, vc_hbm, os_ref,
                      k_buf, v_buf, sem, m_ref, l_ref, g_ref, acc_ref, n_past_blocks, steps_per_seq)
    _prompt_tile(lax.rem(step, n_blocks), q_ref, k_ref, vt_in_ref, o_ref, kmean_ref, vt_ref, kpos_ref, t_ref, n_blocks)


def _moba(q, k, vt, q_s, k_new, v_new, cache_kt, cache_vt, page_ids):
    n, t, c = q.shape
    n_blocks = t // MOBA_BLOCK
    gate_rows = -(-n_blocks // SUBLANES) * SUBLANES
    rows = HEADS_PER_TILE * MOBA_BLOCK
    assert t % MOBA_BLOCK == 0 and c == ATT_DIM and vt.shape == (n, c, t)
    ns, ts, _ = q_s.shape
    pages_per_seq = page_ids.shape[0] // ns
    past_len = pages_per_seq * PAGE_SIZE
    n_past_blocks = past_len // MOBA_BLOCK
    assert past_len % MOBA_BLOCK == 0 and ts <= PAGE_SIZE and ts % SUBLANES == 0
    assert pages_per_seq % PAGES_PER_STEP == 0 and MOBA_TOPK <= n_past_blocks <= LANES
    steps_per_seq = pages_per_seq // PAGES_PER_STEP
    n_steps, n_groups = n * n_blocks, ns * steps_per_seq
    assert n_groups % n_steps == 0 and n_groups >= PAGE_SLOTS - 1
    tile = pl.BlockSpec((None, MOBA_BLOCK, c), lambda i, pt: (i // n_blocks, i % n_blocks, 0))
    whole = lambda a: pl.BlockSpec(a.shape, lambda i, pt: (0,) * a.ndim)
    hbm = pl.BlockSpec(memory_space=pl.ANY)
    stat = pltpu.VMEM((ATT_HEADS * ts, LANES), F32)
    ring = pltpu.VMEM((PAGE_SLOTS, PAGES_PER_STEP, c, PAGE_SIZE), cache_kt.dtype)
    return pl.pallas_call(
        functools.partial(_moba_kernel, n_blocks=n_blocks, n_past_blocks=n_past_blocks, steps_per_seq=steps_per_seq,
                          groups_per_step=n_groups // n_steps),
        grid_spec=pltpu.PrefetchScalarGridSpec(
            num_scalar_prefetch=1,
            grid=(n_steps,),
            in_specs=[tile, pl.BlockSpec((None, t, c), lambda i, pt: (i // n_blocks, 0, 0)),
                      pl.BlockSpec((None, c, t), lambda i, pt: (i // n_blocks, 0, 0)),
                      whole(q_s), whole(k_new), whole(v_new), hbm, hbm],
            out_specs=[tile, whole(q_s)],
            scratch_shapes=[pltpu.VMEM((gate_rows, c), F32),
                            pltpu.VMEM((n_blocks, ATT_HEADS, HEAD_DIM + BF16_SUBLANES, MOBA_BLOCK), BF16),
                            pltpu.VMEM((MOBA_BLOCK, LANES), BF16),
                            pltpu.VMEM((ATT_PAIRS, n_blocks, MOBA_BLOCK, rows), F32),
                            ring, ring, pltpu.SemaphoreType.DMA((2, PAGE_SLOTS)), stat, stat, stat,
                            pltpu.VMEM((n_past_blocks, ATT_HEADS * ts, c), F32)],
        ),
        out_shape=[jax.ShapeDtypeStruct((n, t, c), F32), jax.ShapeDtypeStruct((ns, ts, c), F32)],
        compiler_params=_params("arbitrary"),
        name="moba",
    )(page_ids, q, k, vt, q_s, k_new, v_new, cache_kt, cache_vt)


def _retention_constants(c):
    lg = np.log1p(-np.exp2(-5.0 - np.arange(RET_HEADS, dtype=np.float64)))
    idx = np.arange(c, dtype=np.float64)
    diff = idx[:, None] - idx[None, :]
    d_in = np.where(diff >= 0, np.exp(lg[:, None, None] * np.maximum(diff, 0.0)), 0.0)
    q_dec = np.exp(lg[:, None] * (idx + 1.0))
    k_dec = np.exp(lg[:, None] * (c - 1.0 - idx))
    c_dec = np.exp(lg * c)
    per_lane = lambda a: np.repeat(a.reshape(RET_PAIRS, HEADS_PER_TILE, -1).transpose(0, 2, 1), HEAD_DIM, axis=2)
    block = np.kron(np.eye(HEADS_PER_TILE), np.ones((HEAD_DIM, HEAD_DIM)))
    state_dec = block[None] * np.repeat(c_dec.reshape(RET_PAIRS, HEADS_PER_TILE), HEAD_DIM, axis=1)[:, :, None]
    as_f32 = lambda a: jnp.asarray(a, F32)
    return (as_f32(d_in.reshape(RET_PAIRS, HEADS_PER_TILE * c, c)), as_f32(per_lane(q_dec)), as_f32(per_lane(k_dec)),
            as_f32(state_dec), as_f32(block))


def _retention_kernel(*refs, has_state):
    if has_state:
        q_ref, k_ref, v_ref, g_ref, s0_ref, din_ref, qd_ref, kd_ref, sd_ref, blk_ref, y_ref, sout_ref, s_ref = refs
    else:
        q_ref, k_ref, v_ref, g_ref, din_ref, qd_ref, kd_ref, sd_ref, blk_ref, y_ref, sout_ref, s_ref = refs
    ci = pl.program_id(1)
    n_seqs, c, _ = q_ref.shape

    @pl.when(ci == 0)
    def _():
        if not has_state:
            s_ref[...] = jnp.zeros(s_ref.shape, F32)
            return
        zero = jnp.zeros((HEAD_DIM, HEAD_DIM), F32)
        for b in range(n_seqs):
            for p in range(RET_PAIRS):
                even, odd = s0_ref[b, 2 * p].astype(F32), s0_ref[b, 2 * p + 1].astype(F32)
                s_ref[b, p] = jnp.concatenate([jnp.concatenate([even, zero], axis=1),
                                               jnp.concatenate([zero, odd], axis=1)], axis=0)

    lane = lax.broadcasted_iota(jnp.int32, (c, LANES), 1)
    for b in range(n_seqs):
        for p in range(RET_PAIRS):
            cols = slice(p * LANES, (p + 1) * LANES)
            q = q_ref[b, :, cols]
            k = k_ref[b, :, cols] * HEAD_DIM ** -0.5
            v = v_ref[b, :, cols].astype(BF16)
            state = s_ref[b, p]
            att = _dot(_stack_pair(q, lane).astype(BF16), k.astype(BF16), NT_DIMS) * din_ref[p]
            o = _unstack_pair(_dot(att.astype(BF16), v), lane)
            o = o + _dot((q * qd_ref[p]).astype(BF16), state.astype(BF16))
            s_ref[b, p] = state * sd_ref[p] + _dot((k * kd_ref[p]).astype(BF16), v, TN_DIMS) * blk_ref[...]
            sq = o * o
            even = jnp.sum(jnp.where(lane < HEAD_DIM, sq, 0.0), axis=1, keepdims=True)
            odd = jnp.sum(jnp.where(lane >= HEAD_DIM, sq, 0.0), axis=1, keepdims=True)
            ms = jnp.where(lane < HEAD_DIM, even, odd) * (1.0 / HEAD_DIM)
            y_ref[b, :, cols] = _silu(g_ref[b, :, cols]) * (o * lax.rsqrt(ms + EPS))

    @pl.when(ci == pl.num_programs(1) - 1)
    def _():
        for b in range(n_seqs):
            for p in range(RET_PAIRS):
                state = s_ref[b, p]
                sout_ref[b, 2 * p] = state[:HEAD_DIM, :HEAD_DIM]
                sout_ref[b, 2 * p + 1] = state[HEAD_DIM:, HEAD_DIM:]


def _retention(q, k, v, g, state):
    n, t, d = q.shape
    c = math.gcd(t, RET_CHUNK)
    nb = math.gcd(n, RET_SEQS)
    consts = _retention_constants(c)
    chunk = pl.BlockSpec((nb, c, d), lambda i, j: (i, j, 0))
    st = pl.BlockSpec((nb, RET_HEADS, HEAD_DIM, HEAD_DIM), lambda i, j: (i, 0, 0, 0))
    in_specs, args = [chunk] * 4, [q, k, v, g]
    if state is not None:
        in_specs.append(pl.BlockSpec((None, nb, RET_HEADS, HEAD_DIM, HEAD_DIM), lambda i, j: (state.layer, i, 0, 0, 0)))
        args.append(state.stack)
    in_specs += [_resident(a) for a in consts]
    args += list(consts)
    return pl.pallas_call(
        functools.partial(_retention_kernel, has_state=state is not None),
        grid=(n // nb, t // c),
        in_specs=in_specs,
        out_specs=[chunk, st],
        out_shape=[jax.ShapeDtypeStruct((n, t, d), F32), jax.ShapeDtypeStruct((n, RET_HEADS, HEAD_DIM, HEAD_DIM), F32)],
        scratch_shapes=[pltpu.VMEM((nb, RET_PAIRS, LANES, LANES), F32)],
        compiler_params=_params("parallel", "arbitrary"),
        name="retention",
    )(*args)


def _layer(xp, n_p, xs, n_s, pool_prefix, ret_state, cache, w, final_norm, final, kv_before):
    t_p, t_s = xp.shape[0] // n_p, xs.shape[0] // n_s
    assert t_p >= POOL_PAST
    zp = _dense_in(xp, w["g1"], w["f1i"], w["f1o"], w["gm"], w["win"], prompt=(t_p, w["pool_w"], w["pool_scale"]))
    zs = _dense_in(xs, w["g1"], w["f1i"], w["f1o"], w["gm"], w["win"])
    seq_p = lambda a: a.reshape(n_p, t_p, a.shape[-1])
    seq_s = lambda a: a.reshape(n_s, t_s, a.shape[-1])
    att_p, att_s = _moba(seq_p(zp["qa"]), seq_p(zp["ka"]), zp["vt"], seq_s(zs["qa"]), seq_s(zs["ka"]), seq_s(zs["va"]),
                         cache["k"], cache["v"], cache["page_ids"])
    u = seq_s(zs["u"])
    halo = jnp.pad(pool_prefix.astype(F32), ((0, 0), (POOL_HALO - POOL_PAST, 0), (0, 0)))
    pool_s = _pool(u, halo, w["pool_w"], w["pool_scale"], cache["past_len"])
    pool_new_s = jnp.concatenate([pool_prefix.astype(F32), u], axis=1)[:, -POOL_PAST:]
    pool_new_p = zp["pool_tail"][:, POOL_HALO - POOL_PAST:]
    ret_p, state_p = _retention(seq_p(zp["qr"]), seq_p(zp["kr"]), seq_p(zp["vr"]), seq_p(zp["gr"]), None)
    ret_s, state_s = _retention(seq_s(zs["qr"]), seq_s(zs["kr"]), seq_s(zs["vr"]), seq_s(zs["gr"]), ret_state)
    flat = lambda a: a.reshape(-1, a.shape[-1])
    k_p, v_p = zp["kt"], zp["vt"]
    stack_layers = tuple([before[i] for before in kv_before] + [new] for i, new in enumerate((k_p, v_p))) if final else ()
    xp2, stacked = _dense_out(zp["x1"], zp["y_pool"], flat(att_p), flat(ret_p), w["wo"], w["g2"], w["f2i"], w["f2o"],
                              final_norm, final, stack_layers)
    if stacked:
        k_p, v_p = stacked
    xs2, _ = _dense_out(zs["x1"], flat(pool_s), flat(att_s), flat(ret_s), w["wo"], w["g2"], w["f2i"], w["f2o"],
                        final_norm, final)
    heads = lambda a: a.reshape(n_s, t_s, ATT_HEADS, HEAD_DIM)
    return xp2, xs2, (k_p, v_p, pool_new_p, state_p), (heads(zs["ka"]), heads(zs["va"]), pool_new_s, state_s)


def kernel(x_prompt, x_sample, cache_k, cache_v, state_pool, state_ret, page_table, norm_ffn1, ffn1_w_in, ffn1_w_out,
           norm_mix, w_in, pool_w, pool_scale, w_out, norm_ffn2, ffn2_w_in, ffn2_w_out, norm_final):
    depth = w_in.shape[0]
    n_p, t_p, d = x_prompt.shape
    n_s, t_s, _ = x_sample.shape
    n_pool = cache_k.shape[1]
    pages_per_seq = page_table.shape[1]
    assert cache_k.shape[2] == PAGE_SIZE and d % LANES == 0
    pages_t = lambda a: jnp.transpose(a, (0, 1, 3, 4, 2)).reshape(depth * n_pool, ATT_DIM, PAGE_SIZE)
    cache_k, cache_v = pages_t(cache_k), pages_t(cache_v)
    row = lambda a: a.reshape(1, -1).astype(F32)
    final_norm = row(norm_final)
    hp, hs = x_prompt.reshape(n_p * t_p, d), x_sample.reshape(n_s * t_s, d)
    big = {k: a.astype(BF16) for k, a in dict(f1i=ffn1_w_in, f1o=ffn1_w_out, win=w_in, wo=w_out, f2i=ffn2_w_in,
                                              f2o=ffn2_w_out).items()}
    kv_prompt, outs = [], [[] for _ in range(6)]
    for l in range(depth):
        w = dict(g1=row(norm_ffn1[l]), gm=row(norm_mix[l]), g2=row(norm_ffn2[l]),
                 pool_w=jax.scipy.linalg.block_diag(*pool_w[l]).astype(BF16), pool_scale=row(pool_scale[l]),
                 **{k: _Stacked(a, l) for k, a in big.items()})
        cache = dict(k=cache_k, v=cache_v, past_len=pages_per_seq * PAGE_SIZE,
                     page_ids=(page_table.astype(jnp.int32) + l * n_pool).reshape(-1))
        hp, hs, (kp, vp, pp, rp), (ks, vs, ps, rs) = _layer(hp, n_p, hs, n_s, state_pool[l], _Stacked(state_ret, l), cache, w,
                                                            final_norm, l == depth - 1, kv_prompt)
        kv_prompt.append((kp, vp))
        for lst, a in zip(outs, (pp, rp, ks, vs, ps, rs)):
            lst.append(a)
    heads = lambda a: jnp.transpose(a.reshape(depth, n_p, ATT_HEADS, HEAD_DIM, t_p), (0, 1, 4, 2, 3))
    pool_p, ret_p, k_s, v_s, pool_s, ret_s = (jnp.stack(lst) for lst in outs)
    return (hp.reshape(n_p, t_p, d), hs.reshape(n_s, t_s, d), heads(kp), heads(vp), pool_p, ret_p, k_s, v_s, pool_s, ret_s)
```
